```python
import jax, jax.numpy as jnp
from jax import lax
import numpy as np

D_MODEL = 1024
BATCH = 8
SEQ = 8192
DEPTH = 4

GMLP_CHUNK = 128
GMLP_GROUPS = 8
GMLP_GROUP_DIM = 128
GMLP_WIDTH = GMLP_GROUPS * GMLP_GROUP_DIM
HGRN_HEADS = 8
HGRN_DK = 128
HGRN_DV = 128
HGRN_KWIDTH = HGRN_HEADS * HGRN_DK
HGRN_VWIDTH = HGRN_HEADS * HGRN_DV
HGRN_CHUNK = 64
FFN_HIDDEN = 2816
N_ADA = 9
EPS = 1e-6
IN_SPLITS = (GMLP_WIDTH, GMLP_WIDTH,
             HGRN_KWIDTH, HGRN_KWIDTH,
             HGRN_VWIDTH, HGRN_VWIDTH,
             D_MODEL, D_MODEL)
IN_WIDTH = sum(IN_SPLITS)

kernel_name = 'hybrid_gmlp_hgrn2_macaron_adaln'


def _rms_norm(x, w):
    xf = x.astype(jnp.float32)
    y = xf * lax.rsqrt(jnp.mean(xf * xf, axis=-1, keepdims=True) + EPS)
    return (y * w.astype(jnp.float32)).astype(x.dtype)


def _split_cols(z, widths):
    idx = [int(v) for v in np.cumsum(widths)[:-1]]
    return jnp.split(z, idx, axis=-1)


def _swiglu(xn, w13, w2):
    a, b = jnp.split(xn @ w13, 2, axis=-1)
    return (jax.nn.silu(a) * b) @ w2


def _gmlp_branch(u, v, ws, bs):
    b, s, _ = v.shape
    nc = s // GMLP_CHUNK
    vg = v.reshape(b, nc, GMLP_CHUNK, GMLP_GROUPS, GMLP_GROUP_DIM).astype(jnp.float32)
    mu = jnp.mean(vg, axis=-1, keepdims=True)
    var = jnp.mean(jnp.square(vg - mu), axis=-1, keepdims=True)
    vn = ((vg - mu) * lax.rsqrt(var + EPS)).astype(v.dtype)
    ws_causal = jnp.tril(ws)
    mixed = jnp.einsum('gts,bnsgc->bntgc', ws_causal, vn) + bs.T[None, None, :, :, None]
    return u * mixed.reshape(b, s, GMLP_WIDTH)


def _to_chunks(t, heads, d):
    b, s, _ = t.shape
    return t.reshape(b, s // HGRN_CHUNK, HGRN_CHUNK, heads, d).transpose(1, 0, 3, 2, 4)


def _hgrn2_branch(q_pre, f_pre, i_in, og, lb):
    b, s, _ = q_pre.shape
    lb = lb.astype(jnp.float32)
    fz = f_pre.astype(jnp.float32)
    q = jax.nn.silu(q_pre.astype(jnp.float32))
    logf = jnp.logaddexp(jnp.log(lb), jnp.log1p(-lb) + jax.nn.log_sigmoid(fz))
    k = (1.0 - lb) * jax.nn.sigmoid(-fz)
    v = i_in.astype(jnp.float32)

    qc = _to_chunks(q, HGRN_HEADS, HGRN_DK)
    kc = _to_chunks(k, HGRN_HEADS, HGRN_DK)
    vc = _to_chunks(v, HGRN_HEADS, HGRN_DV)
    bcum = jnp.cumsum(_to_chunks(logf, HGRN_HEADS, HGRN_DK), axis=3)
    causal = jnp.tril(jnp.ones((HGRN_CHUNK, HGRN_CHUNK), dtype=bool))

    def step(state, inp):
        q_t, k_t, v_t, b_t = inp
        diff = b_t[:, :, :, None, :] - b_t[:, :, None, :, :]
        decay = jnp.exp(jnp.where(causal[:, :, None], diff, -jnp.inf))
        scores = jnp.einsum('bhtk,bhsk,bhtsk->bhts', q_t, k_t, decay)
        o = jnp.einsum('bhts,bhsv->bhtv', scores, v_t) + \
            jnp.einsum('bhtk,bhkv->bhtv', q_t * jnp.exp(b_t), state)
        b_last = b_t[:, :, -1:, :]
        new_state = jnp.exp(b_last[:, :, 0, :])[..., None] * state + \
            jnp.einsum('bhsk,bhsv->bhkv', k_t * jnp.exp(b_last - b_t), v_t)
        return new_state, o

    s0 = jnp.zeros((b, HGRN_HEADS, HGRN_DK, HGRN_DV), jnp.float32)
    _, o = lax.scan(step, s0, (qc, kc, vc, bcum))
    o = o.transpose(1, 0, 3, 2, 4).reshape(b, s, HGRN_HEADS, HGRN_DV)
    o = o * lax.rsqrt(jnp.mean(o * o, axis=-1, keepdims=True) + EPS)
    o = o.reshape(b, s, HGRN_VWIDTH) * jax.nn.sigmoid(og.astype(jnp.float32))
    return o.astype(q_pre.dtype)


def _mixer(hn, w_in, ws, bs, lb, w_a, w_b, w_o):
    z = hn @ w_in
    u, v, q, fz, i_in, og, ga, gb = _split_cols(z, IN_SPLITS)
    ya = _gmlp_branch(jax.nn.gelu(u), jax.nn.gelu(v), ws, bs) @ w_a
    yb = _hgrn2_branch(q, fz, i_in, og, lb) @ w_b
    m = jax.nn.sigmoid(ga) * ya + jax.nn.sigmoid(gb) * yb
    return m @ w_o


def setup_inputs(seed: int = 0) -> dict:
    key = jax.random.key(seed)
    ks = jax.random.split(key, 15)
    nrm = jax.random.normal
    f32 = jnp.float32
    return {
        'x': nrm(ks[0], (BATCH, SEQ, D_MODEL), f32),
        'c': nrm(ks[1], (BATCH, D_MODEL), f32),
        'w_in': nrm(ks[2], (DEPTH, D_MODEL, IN_WIDTH), f32) * D_MODEL ** -0.5,
        'gmlp_ws': nrm(ks[3], (DEPTH, GMLP_GROUPS, GMLP_CHUNK, GMLP_CHUNK), f32) * (0.5 * GMLP_CHUNK ** -0.5),
        'gmlp_b': 1.0 + 0.1 * nrm(ks[4], (DEPTH, GMLP_GROUPS, GMLP_CHUNK), f32),
        'hgrn_lb_logits': 0.5 * nrm(ks[5], (DEPTH, HGRN_KWIDTH), f32),
        'w_branch_a': nrm(ks[6], (DEPTH, GMLP_WIDTH, D_MODEL), f32) * GMLP_WIDTH ** -0.5,
        'w_branch_b': nrm(ks[7], (DEPTH, HGRN_VWIDTH, D_MODEL), f32) * HGRN_VWIDTH ** -0.5,
        'w_out': nrm(ks[8], (DEPTH, D_MODEL, D_MODEL), f32) * D_MODEL ** -0.5,
        'ffn_w13': nrm(ks[9], (DEPTH, 2, D_MODEL, 2 * FFN_HIDDEN), f32) * D_MODEL ** -0.5,
        'ffn_w2': nrm(ks[10], (DEPTH, 2, FFN_HIDDEN, D_MODEL), f32) * FFN_HIDDEN ** -0.5,
        'norm_w': 1.0 + 0.05 * nrm(ks[11], (DEPTH, 3, D_MODEL), f32),
        'ada_w': nrm(ks[12], (DEPTH, D_MODEL, N_ADA * D_MODEL), f32) * (0.1 * D_MODEL ** -0.5),
        'ada_b': 0.02 * nrm(ks[13], (DEPTH, N_ADA * D_MODEL), f32),
        'final_norm_w': 1.0 + 0.05 * nrm(ks[14], (D_MODEL,), f32),
    }


def reference(x, c, w_in, gmlp_ws, gmlp_b, hgrn_lb_logits, w_branch_a, w_branch_b, w_out,
              ffn_w13, ffn_w2, norm_w, ada_w, ada_b, final_norm_w):
    lb_all = jnp.cumsum(jax.nn.softmax(hgrn_lb_logits.astype(jnp.float32), axis=0), axis=0)
    lb_all = lb_all - lb_all[0:1]
    c_act = jax.nn.silu(c)
    h = x
    for l in range(DEPTH):
        mod = (c_act @ ada_w[l] + ada_b[l]).reshape(c.shape[0], N_ADA, 1, D_MODEL)
        sh1, sc1, g1, sh2, sc2, g2, sh3, sc3, g3 = [mod[:, j] for j in range(N_ADA)]
        hn = _rms_norm(h, norm_w[l, 0]) * (1.0 + sc1) + sh1
        h = h + 0.5 * (1.0 + g1) * _swiglu(hn, ffn_w13[l, 0], ffn_w2[l, 0])
        hn = _rms_norm(h, norm_w[l, 1]) * (1.0 + sc2) + sh2
        h = h + (1.0 + g2) * _mixer(hn, w_in[l], gmlp_ws[l], gmlp_b[l], lb_all[l],
                                    w_branch_a[l], w_branch_b[l], w_out[l])
        hn = _rms_norm(h, norm_w[l, 2]) * (1.0 + sc3) + sh3
        h = h + 0.5 * (1.0 + g3) * _swiglu(hn, ffn_w13[l, 1], ffn_w2[l, 1])
    return _rms_norm(h, final_norm_w)
```

```python
import functools

import jax
import jax.numpy as jnp
from jax import lax
from jax.experimental import pallas as pl
from jax.experimental.pallas import tpu as pltpu

F32 = jnp.float32
BF16 = jnp.bfloat16

D_MODEL = 1024
DEPTH = 4
FFN_HIDDEN = 2816
N_ADA = 9
EPS = 1e-6
GMLP_CHUNK = 128
GMLP_GROUPS = 8
HEAD_DIM = 128
N_PIECES = 8
HGRN_CHUNK = 16
HEADS_PER_STEP = 2
LANES_PER_STEP = HEADS_PER_STEP * HEAD_DIM

VMEM_LIMIT = 56 * 1024 * 1024


def _sigmoid(x):
    return 1.0 / (1.0 + jnp.exp(-x))


def _gelu_tanh(x):
    c = 0.7978845608028654
    return 0.5 * x * (1.0 + jnp.tanh(c * (x + 0.044715 * (x * x * x))))


def _rms_modulate(x, nw, scale, shift):
    ms = jnp.mean(x * x, axis=-1, keepdims=True)
    return (x * lax.rsqrt(ms + EPS) * nw) * (1.0 + scale) + shift


def _ada_kernel(c_ref, w_ref, b_ref, o_ref):
    c = c_ref[...]
    ca = c * _sigmoid(c)
    o_ref[...] = jnp.dot(ca, w_ref[...], preferred_element_type=F32,
                         precision=lax.Precision.HIGHEST) + b_ref[...]


def _ada_mod(c, ada_w, ada_b):
    b = c.shape[0]
    nblk = N_ADA
    return pl.pallas_call(
        _ada_kernel,
        grid=(DEPTH, nblk),
        in_specs=[
            pl.BlockSpec((b, D_MODEL), lambda l, j: (0, 0)),
            pl.BlockSpec((None, D_MODEL, D_MODEL), lambda l, j: (l, 0, j)),
            pl.BlockSpec((None, 1, D_MODEL), lambda l, j: (l, 0, j)),
        ],
        out_specs=pl.BlockSpec((None, b, D_MODEL), lambda l, j: (l, 0, j)),
        out_shape=jax.ShapeDtypeStruct((DEPTH, b, N_ADA * D_MODEL), F32),
        name="ada_mod",
    )(c, ada_w, ada_b.reshape(DEPTH, 1, N_ADA * D_MODEL))


def _lb_kernel(x_ref, o_ref):
    x = x_ref[...]
    rows = [x[i:i + 1, :] for i in range(DEPTH)]
    m = rows[0]
    for r in rows[1:]:
        m = jnp.maximum(m, r)
    e = [jnp.exp(r - m) for r in rows]
    tot = e[0]
    for t in e[1:]:
        tot = tot + t
    p = [t / tot for t in e]
    cum = p[0]
    first = cum
    for i in range(DEPTH):
        if i > 0:
            cum = cum + p[i]
        o_ref[i:i + 1, :] = cum - first


def _lower_bounds(logits):
    return pl.pallas_call(
        _lb_kernel,
        out_shape=jax.ShapeDtypeStruct(logits.shape, F32),
        name="hgrn_lower_bounds",
    )(logits.astype(F32))


FFN_TM = 512
FFN_COL_CHUNKS = ((0, 1024), (1024, 2048), (2048, FFN_HIDDEN))


def _ffn_kernel(x_ref, mod_ref, nw_ref, w13_ref, w2_ref, *rest, mod_off, final):
    if final:
        fnw_ref, o_ref = rest
    else:
        (o_ref,) = rest
    x = x_ref[...]
    shift = mod_ref[mod_off:mod_off + 1, :]
    scale = mod_ref[mod_off + 1:mod_off + 2, :]
    gate = mod_ref[mod_off + 2:mod_off + 3, :]
    hb = _rms_modulate(x, nw_ref[...], scale, shift).astype(BF16)
    acc = None
    for c0, c1 in FFN_COL_CHUNKS:
        a = jnp.dot(hb, w13_ref[:, c0:c1], preferred_element_type=F32)
        b = jnp.dot(hb, w13_ref[:, FFN_HIDDEN + c0:FFN_HIDDEN + c1],
                    preferred_element_type=F32)
        act = (a * _sigmoid(a) * b).astype(BF16)
        part = jnp.dot(act, w2_ref[c0:c1, :], preferred_element_type=F32)
        acc = part if acc is None else acc + part
    y = x + (0.5 * (1.0 + gate)) * acc
    if final:
        ms = jnp.mean(y * y, axis=-1, keepdims=True)
        y = y * lax.rsqrt(ms + EPS) * fnw_ref[...]
    o_ref[...] = y


def _ffn(h, mod, nw, w13, w2, *, mod_off, seq, final_w=None):
    n = h.shape[0]
    tm = FFN_TM
    tiles_per_seq = seq // tm
    final = final_w is not None
    in_specs = [
        pl.BlockSpec((tm, D_MODEL), lambda i: (i, 0)),
        pl.BlockSpec((None, N_ADA, D_MODEL), lambda i: (i // tiles_per_seq, 0, 0)),
        pl.BlockSpec((1, D_MODEL), lambda i: (0, 0)),
        pl.BlockSpec((D_MODEL, 2 * FFN_HIDDEN), lambda i: (0, 0),
                     pipeline_mode=pl.Buffered(1)),
        pl.BlockSpec((FFN_HIDDEN, D_MODEL), lambda i: (0, 0),
                     pipeline_mode=pl.Buffered(1)),
    ]
    args = [h, mod, nw, w13, w2]
    if final:
        in_specs.append(pl.BlockSpec((1, D_MODEL), lambda i: (0, 0)))
        args.append(final_w)
    return pl.pallas_call(
        functools.partial(_ffn_kernel, mod_off=mod_off, final=final),
        grid=(n // tm,),
        in_specs=in_specs,
        out_specs=pl.BlockSpec((tm, D_MODEL), lambda i: (i, 0)),
        out_shape=jax.ShapeDtypeStruct((n, D_MODEL), F32),
        compiler_params=pltpu.CompilerParams(
            dimension_semantics=("parallel",), vmem_limit_bytes=VMEM_LIMIT),
        name="ffn_half_step",
    )(*args)


M1_TM = 256


def _mix_in_kernel(x_ref, mod_ref, nw_ref, win_ref, ws_ref, bs_ref, lb_ref,
                   yap_ref, q_ref, k_ref, f_ref, v_ref, sog_ref, sga_ref, sgb_ref,
                   gu_scr):
    tm = x_ref.shape[0]
    nblk = tm // GMLP_CHUNK
    x = x_ref[...]
    shift = mod_ref[3:4, :]
    scale = mod_ref[4:5, :]
    hb = _rms_modulate(x, nw_ref[...], scale, shift).astype(BF16)

    def piece(p):
        return jnp.dot(hb, win_ref[:, p * D_MODEL:(p + 1) * D_MODEL],
                       preferred_element_type=F32)

    gu_scr[...] = _gelu_tanh(piece(0))
    gv = _gelu_tanh(piece(1))

    row = lax.broadcasted_iota(jnp.int32, (GMLP_CHUNK, GMLP_CHUNK), 0)
    col = lax.broadcasted_iota(jnp.int32, (GMLP_CHUNK, GMLP_CHUNK), 1)
    causal = row >= col
    for g in range(GMLP_GROUPS):
        ls = slice(g * HEAD_DIM, (g + 1) * HEAD_DIM)
        xg = gv[:, ls]
        mu = jnp.mean(xg, axis=-1, keepdims=True)
        xc = xg - mu
        var = jnp.mean(xc * xc, axis=-1, keepdims=True)
        vn = (xc * lax.rsqrt(var + EPS)).astype(BF16)
        rhs = jnp.concatenate(
            [vn[n * GMLP_CHUNK:(n + 1) * GMLP_CHUNK, :] for n in range(nblk)], axis=1)
        wc = jnp.where(causal, ws_ref[g], 0.0).astype(BF16)
        mixed = jnp.dot(wc, rhs, preferred_element_type=F32)
        bias = jnp.broadcast_to(bs_ref[:, g:g + 1], (GMLP_CHUNK, GMLP_CHUNK))
        for n in range(nblk):
            rs = slice(n * GMLP_CHUNK, (n + 1) * GMLP_CHUNK)
            mx = mixed[:, n * GMLP_CHUNK:(n + 1) * GMLP_CHUNK] + bias
            yap_ref[rs, ls] = (gu_scr[rs, ls] * mx).astype(BF16)

    def store_heads(ref, val):
        for hd in range(D_MODEL // HEAD_DIM):
            ref[hd] = val[:, hd * HEAD_DIM:(hd + 1) * HEAD_DIM]

    zq = piece(2)
    store_heads(q_ref, zq * _sigmoid(zq))

    zf = piece(3)
    lb = lb_ref[...]
    e = jnp.exp(-jnp.abs(zf))
    r = 1.0 / (1.0 + e)
    er = e * r
    pos = zf >= 0.0
    store_heads(f_ref, lb + (1.0 - lb) * jnp.where(pos, r, er))
    store_heads(k_ref, (1.0 - lb) * jnp.where(pos, er, r))

    store_heads(v_ref, piece(4))
    sog_ref[...] = _sigmoid(piece(5)).astype(BF16)
    sga_ref[...] = _sigmoid(piece(6)).astype(BF16)
    sgb_ref[...] = _sigmoid(piece(7)).astype(BF16)


def _mix_in(h, mod, nw, w_in, ws, bs_t, lb, *, seq):
    n = h.shape[0]
    tm = M1_TM
    tiles_per_seq = seq // tm
    tile = pl.BlockSpec((tm, D_MODEL), lambda i: (i, 0))
    const2 = lambda i: (0, 0)
    nheads = D_MODEL // HEAD_DIM
    head_tile = pl.BlockSpec((nheads, tm, HEAD_DIM), lambda i: (0, i, 0))
    f32_out = jax.ShapeDtypeStruct((nheads, n, HEAD_DIM), F32)
    bf_out = jax.ShapeDtypeStruct((n, D_MODEL), BF16)
    return pl.pallas_call(
        _mix_in_kernel,
        grid=(n // tm,),
        in_specs=[
            tile,
            pl.BlockSpec((None, N_ADA, D_MODEL), lambda i: (i // tiles_per_seq, 0, 0)),
            pl.BlockSpec((1, D_MODEL), const2),
            pl.BlockSpec((D_MODEL, N_PIECES * D_MODEL), const2,
                         pipeline_mode=pl.Buffered(1)),
            pl.BlockSpec((GMLP_GROUPS, GMLP_CHUNK, GMLP_CHUNK), lambda i: (0, 0, 0)),
            pl.BlockSpec((GMLP_CHUNK, GMLP_GROUPS), const2),
            pl.BlockSpec((1, D_MODEL), const2),
        ],
        out_specs=[tile] + [head_tile] * 4 + [tile] * 3,
        out_shape=[bf_out, f32_out, f32_out, f32_out, f32_out, bf_out, bf_out, bf_out],
        scratch_shapes=[pltpu.VMEM((tm, D_MODEL), F32)],
        compiler_params=pltpu.CompilerParams(
            dimension_semantics=("parallel",), vmem_limit_bytes=VMEM_LIMIT),
        name="mixer_in_proj",
    )(h, mod, nw, w_in, ws, bs_t, lb)


HG_TOKENS = 512
HG_NC = HG_TOKENS // HGRN_CHUNK
N_PAIRS = HGRN_CHUNK * (HGRN_CHUNK + 1) // 2


def _pair_row(j, i):
    return (j * (j + 1) // 2 + i) * HG_NC


def _hgrn_kernel(q_ref, k_ref, f_ref, v_ref, o_ref,
                 st_ref, qn_ref, kn_ref, g_ref, tm_ref, r_ref, oi_ref):
    c16 = HGRN_CHUNK
    nc = HG_NC

    @pl.when(pl.program_id(2) == 0)
    def _():
        st_ref[...] = jnp.zeros_like(st_ref)

    def pos(j):
        return pl.ds(j, nc, stride=c16)

    for h in range(HEADS_PER_STEP):
        ls = slice(h * HEAD_DIM, (h + 1) * HEAD_DIM)
        gcum = None
        for j in range(c16):
            fj = f_ref[h, pos(j), :]
            gcum = fj if gcum is None else gcum * fj
            qn_ref[h, pos(j), :] = q_ref[h, pos(j), :] * gcum
        g_ref[h] = gcum
        hsuf = None
        for j in range(c16 - 1, -1, -1):
            kj = k_ref[h, pos(j), :]
            if hsuf is None:
                kn_ref[h, pos(j), :] = kj
                hsuf = f_ref[h, pos(j), :]
            else:
                kn_ref[h, pos(j), :] = kj * hsuf
                hsuf = hsuf * f_ref[h, pos(j), :]

        for j in range(c16):
            qe = q_ref[h, pos(j), :]
            for i in range(j, -1, -1):
                if i < j:
                    qe = qe * f_ref[h, pos(i + 1), :]
                tm_ref[pl.ds(_pair_row(j, i), nc), ls] = (
                    qe * k_ref[h, pos(i), :]).astype(BF16)

    hr = lax.broadcasted_iota(jnp.int32, (LANES_PER_STEP, LANES_PER_STEP), 0) // HEAD_DIM
    hc = lax.broadcasted_iota(jnp.int32, (LANES_PER_STEP, LANES_PER_STEP), 1) // HEAD_DIM
    ones_bd = jnp.where(hr == hc, 1.0, 0.0).astype(BF16)
    r_ref[...] = jnp.dot(tm_ref[...], ones_bd, preferred_element_type=F32)

    for h in range(HEADS_PER_STEP):
        ls = slice(h * HEAD_DIM, (h + 1) * HEAD_DIM)
        for j in range(c16):
            acc = None
            for i in range(j + 1):
                t = r_ref[pl.ds(_pair_row(j, i), nc), ls] * v_ref[h, pos(i), :]
                acc = t if acc is None else acc + t
            o_ref[h, pos(j), :] = acc

    def chunk_step(c, carry):
        rows = pl.ds(pl.multiple_of(c * c16, c16), c16)
        for h in range(HEADS_PER_STEP):
            st = st_ref[h]
            qc = qn_ref[h, rows, :].astype(BF16)
            oi_ref[h, rows, :] = lax.dot_general(
                qc, st.astype(BF16), (((1,), (1,)), ((), ())),
                preferred_element_type=F32)
            vc = v_ref[h, rows, :].astype(BF16)
            kc = kn_ref[h, rows, :].astype(BF16)
            upd = lax.dot_general(vc, kc, (((0,), (0,)), ((), ())),
                                  preferred_element_type=F32)
            st_ref[h] = st * g_ref[h, pl.ds(c, 1), :] + upd
        return carry

    lax.fori_loop(0, nc, chunk_step, 0)
    o_ref[...] = o_ref[...] + oi_ref[...]


def _hgrn(q, k, f, v, *, batch, seq):
    nheads, n, _ = q.shape
    tiles_per_seq = seq // HG_TOKENS
    blk = pl.BlockSpec((HEADS_PER_STEP, HG_TOKENS, HEAD_DIM),
                       lambda b, hp, t: (hp, b * tiles_per_seq + t, 0))
    per_head = (HEADS_PER_STEP, HG_TOKENS, HEAD_DIM)
    return pl.pallas_call(
        _hgrn_kernel,
        grid=(batch, nheads // HEADS_PER_STEP, tiles_per_seq),
        in_specs=[blk] * 4,
        out_specs=blk,
        out_shape=jax.ShapeDtypeStruct((nheads, n, HEAD_DIM), F32),
        scratch_shapes=[
            pltpu.VMEM((HEADS_PER_STEP, HEAD_DIM, HEAD_DIM), F32),
            pltpu.VMEM(per_head, F32),
            pltpu.VMEM(per_head, F32),
            pltpu.VMEM((HEADS_PER_STEP, HG_NC, HEAD_DIM), F32),
            pltpu.VMEM((N_PAIRS * HG_NC, LANES_PER_STEP), BF16),
            pltpu.VMEM((N_PAIRS * HG_NC, LANES_PER_STEP), F32),
            pltpu.VMEM(per_head, F32),
        ],
        compiler_params=pltpu.CompilerParams(
            dimension_semantics=("parallel", "parallel", "arbitrary"),
            vmem_limit_bytes=VMEM_LIMIT),
        name="hgrn2_recurrence",
    )(q, k, f, v)


M3_TM = 512


def _mix_out_kernel(x_ref, mod_ref, yap_ref, o_ref, sog_ref, sga_ref, sgb_ref,
                    wa_ref, wb_ref, wo_ref, out_ref):
    gate = mod_ref[5:6, :]
    parts = []
    for h in range(D_MODEL // HEAD_DIM):
        oh = o_ref[h]
        ms = jnp.mean(oh * oh, axis=-1, keepdims=True)
        parts.append(oh * lax.rsqrt(ms + EPS))
    on = (jnp.concatenate(parts, axis=1) * sog_ref[...].astype(F32)).astype(BF16)
    ya = jnp.dot(yap_ref[...], wa_ref[...], preferred_element_type=F32)
    yb = jnp.dot(on, wb_ref[...], preferred_element_type=F32)
    m = (sga_ref[...].astype(F32) * ya + sgb_ref[...].astype(F32) * yb).astype(BF16)
    out = jnp.dot(m, wo_ref[...], preferred_element_type=F32)
    out_ref[...] = x_ref[...] + (1.0 + gate) * out


def _mix_out(h, mod, yap, o, sog, sga, sgb, w_a, w_b, w_o, *, seq):
    n = h.shape[0]
    tm = M3_TM
    tiles_per_seq = seq // tm
    tile = pl.BlockSpec((tm, D_MODEL), lambda i: (i, 0))
    wspec = pl.BlockSpec((D_MODEL, D_MODEL), lambda i: (0, 0),
                         pipeline_mode=pl.Buffered(1))
    return pl.pallas_call(
        _mix_out_kernel,
        grid=(n // tm,),
        in_specs=[
            tile,
            pl.BlockSpec((None, N_ADA, D_MODEL), lambda i: (i // tiles_per_seq, 0, 0)),
            tile,
            pl.BlockSpec((D_MODEL // HEAD_DIM, tm, HEAD_DIM), lambda i: (0, i, 0)),
            tile, tile, tile,
            wspec, wspec, wspec,
        ],
        out_specs=tile,
        out_shape=jax.ShapeDtypeStruct((n, D_MODEL), F32),
        compiler_params=pltpu.CompilerParams(
            dimension_semantics=("parallel",), vmem_limit_bytes=VMEM_LIMIT),
        name="mixer_out_proj",
    )(h, mod, yap, o, sog, sga, sgb, w_a, w_b, w_o)


def kernel(x, c, w_in, gmlp_ws, gmlp_b, hgrn_lb_logits, w_branch_a, w_branch_b, w_out,
           ffn_w13, ffn_w2, norm_w, ada_w, ada_b, final_norm_w):
    batch, seq, d = x.shape
    assert d == D_MODEL and seq % HG_TOKENS == 0 and seq % FFN_TM == 0
    n = batch * seq

    mod_all = _ada_mod(c, ada_w, ada_b).reshape(DEPTH, batch, N_ADA, D_MODEL)
    lb_all = _lower_bounds(hgrn_lb_logits)

    h = x.reshape(n, D_MODEL)
    for l in range(DEPTH):
        mod = mod_all[l]
        nw = norm_w[l].reshape(3, 1, D_MODEL)
        h = _ffn(h, mod, nw[0], ffn_w13[l, 0].astype(BF16), ffn_w2[l, 0].astype(BF16),
                 mod_off=0, seq=seq)
        yap, q, k, f, v, sog, sga, sgb = _mix_in(
            h, mod, nw[1], w_in[l].astype(BF16), gmlp_ws[l], gmlp_b[l].T,
            lb_all[l].reshape(1, D_MODEL), seq=seq)
        o = _hgrn(q, k, f, v, batch=batch, seq=seq)
        h = _mix_out(h, mod, yap, o, sog, sga, sgb, w_branch_a[l].astype(BF16),
                     w_branch_b[l].astype(BF16), w_out[l].astype(BF16), seq=seq)
        final_w = final_norm_w.reshape(1, D_MODEL) if l == DEPTH - 1 else None
        h = _ffn(h, mod, nw[2], ffn_w13[l, 1].astype(BF16), ffn_w2[l, 1].astype(BF16),
                 mod_off=6, seq=seq, final_w=final_w)
    return h.reshape(batch, seq, D_MODEL)
```

```python
import functools

import jax
import jax.numpy as jnp
from jax import lax
from jax.experimental import pallas as pl
from jax.experimental.pallas import tpu as pltpu

F32 = jnp.float32
BF16 = jnp.bfloat16

D_MODEL = 1024
DEPTH = 4
FFN_HIDDEN = 2816
N_ADA = 9
EPS = 1e-6
GMLP_GROUPS = 8
HEAD_DIM = 128
N_HEADS = D_MODEL // HEAD_DIM
N_PIECES = 8

BLOCK = 128
POSITIONS = 16
CHUNKS = BLOCK // POSITIONS
CHUNK_WIDTH = POSITIONS * D_MODEL

VMEM_LIMIT = 56 * 1024 * 1024


def _sigmoid(x):
    return 1.0 / (1.0 + jnp.exp(-x))


def _gelu_tanh(x):
    c = 0.7978845608028654
    return 0.5 * x * (1.0 + jnp.tanh(c * (x + 0.044715 * (x * x * x))))


def _rms_modulate(x, nw, scale, shift):
    ms = jnp.mean(x * x, axis=-1, keepdims=True)
    return (x * lax.rsqrt(ms + EPS) * nw) * (1.0 + scale) + shift


def _block_token(p):
    return (p % CHUNKS) * POSITIONS + p // CHUNKS


def _gather_block_rows(x_ref, n_blocks):
    pieces = [x_ref[n * CHUNKS:(n + 1) * CHUNKS, j * D_MODEL:(j + 1) * D_MODEL]
              for n in range(n_blocks) for j in range(POSITIONS)]
    return jnp.concatenate(pieces, axis=0)


def _scatter_block_rows(out_ref, y, n_blocks):
    for n in range(n_blocks):
        for j in range(POSITIONS):
            r = (n * POSITIONS + j) * CHUNKS
            out_ref[n * CHUNKS:(n + 1) * CHUNKS, j * D_MODEL:(j + 1) * D_MODEL] = y[r:r + CHUNKS, :]


def _ada_kernel(c_ref, w_ref, b_ref, o_ref):
    c = c_ref[...]
    ca = c * _sigmoid(c)
    o_ref[...] = jnp.dot(ca, w_ref[...], preferred_element_type=F32,
                         precision=lax.Precision.HIGHEST) + b_ref[...]


def _ada_mod(c, ada_w, ada_b):
    b = c.shape[0]
    return pl.pallas_call(
        _ada_kernel,
        grid=(DEPTH, N_ADA),
        in_specs=[
            pl.BlockSpec((b, D_MODEL), lambda l, j: (0, 0)),
            pl.BlockSpec((None, D_MODEL, D_MODEL), lambda l, j: (l, 0, j)),
            pl.BlockSpec((None, 1, D_MODEL), lambda l, j: (l, 0, j)),
        ],
        out_specs=pl.BlockSpec((None, b, D_MODEL), lambda l, j: (l, 0, j)),
        out_shape=jax.ShapeDtypeStruct((DEPTH, b, N_ADA * D_MODEL), F32),
        name="ada_mod",
    )(c, ada_w, ada_b.reshape(DEPTH, 1, N_ADA * D_MODEL))


def _lb_kernel(x_ref, o_ref):
    x = x_ref[...]
    rows = [x[i:i + 1, :] for i in range(DEPTH)]
    m = rows[0]
    for r in rows[1:]:
        m = jnp.maximum(m, r)
    e = [jnp.exp(r - m) for r in rows]
    tot = e[0]
    for t in e[1:]:
        tot = tot + t
    p = [t / tot for t in e]
    cum = p[0]
    first = cum
    for i in range(DEPTH):
        if i > 0:
            cum = cum + p[i]
        o_ref[i:i + 1, :] = cum - first


def _lower_bounds(logits):
    return pl.pallas_call(
        _lb_kernel,
        out_shape=jax.ShapeDtypeStruct(logits.shape, F32),
        name="hgrn_lower_bounds",
    )(logits.astype(F32))


FFN_TM = 512
FFN_COL_CHUNKS = ((0, 1024), (1024, 2048), (2048, FFN_HIDDEN))


def _ffn_kernel(x_ref, mod_ref, nw_ref, w13_ref, w2_ref, *rest, mod_off, final):
    if final:
        fnw_ref, o_ref = rest
    else:
        (o_ref,) = rest
    x = x_ref[...]
    shift = mod_ref[mod_off:mod_off + 1, :]
    scale = mod_ref[mod_off + 1:mod_off + 2, :]
    gate = mod_ref[mod_off + 2:mod_off + 3, :]
    hb = _rms_modulate(x, nw_ref[...], scale, shift).astype(BF16)
    acc = None
    for c0, c1 in FFN_COL_CHUNKS:
        a = jnp.dot(hb, w13_ref[:, c0:c1], preferred_element_type=F32)
        b = jnp.dot(hb, w13_ref[:, FFN_HIDDEN + c0:FFN_HIDDEN + c1],
                    preferred_element_type=F32)
        act = (a * _sigmoid(a) * b).astype(BF16)
        part = jnp.dot(act, w2_ref[c0:c1, :], preferred_element_type=F32)
        acc = part if acc is None else acc + part
    y = x + (0.5 * (1.0 + gate)) * acc
    if final:
        ms = jnp.mean(y * y, axis=-1, keepdims=True)
        y = y * lax.rsqrt(ms + EPS) * fnw_ref[...]
    o_ref[...] = y


def _ffn(h, mod, nw, w13, w2, *, mod_off, seq, final_w=None):
    n = h.shape[0]
    tm = FFN_TM
    tiles_per_seq = seq // tm
    final = final_w is not None
    in_specs = [
        pl.BlockSpec((tm, D_MODEL), lambda i: (i, 0)),
        pl.BlockSpec((None, N_ADA, D_MODEL), lambda i: (i // tiles_per_seq, 0, 0)),
        pl.BlockSpec((1, D_MODEL), lambda i: (0, 0)),
        pl.BlockSpec((D_MODEL, 2 * FFN_HIDDEN), lambda i: (0, 0),
                     pipeline_mode=pl.Buffered(1)),
        pl.BlockSpec((FFN_HIDDEN, D_MODEL), lambda i: (0, 0),
                     pipeline_mode=pl.Buffered(1)),
    ]
    args = [h, mod, nw, w13, w2]
    if final:
        in_specs.append(pl.BlockSpec((1, D_MODEL), lambda i: (0, 0)))
        args.append(final_w)
    return pl.pallas_call(
        functools.partial(_ffn_kernel, mod_off=mod_off, final=final),
        grid=(n // tm,),
        in_specs=in_specs,
        out_specs=pl.BlockSpec((tm, D_MODEL), lambda i: (i, 0)),
        out_shape=jax.ShapeDtypeStruct((n, D_MODEL), F32),
        compiler_params=pltpu.CompilerParams(
            dimension_semantics=("parallel",), vmem_limit_bytes=VMEM_LIMIT),
        name="ffn_half_step",
    )(*args)


M1_TM = 256


def _mix_in_kernel(x_ref, mod_ref, nw_ref, win_ref, ws_ref, bs_ref, lb_ref,
                   yap_ref, q_ref, k_ref, f_ref, v_ref, sog_ref, sga_ref, sgb_ref,
                   gu_scr):
    tm = yap_ref.shape[0]
    nblk = tm // BLOCK
    x = _gather_block_rows(x_ref, nblk)
    shift = mod_ref[3:4, :]
    scale = mod_ref[4:5, :]
    hb = _rms_modulate(x, nw_ref[...], scale, shift).astype(BF16)

    def piece(p):
        return jnp.dot(hb, win_ref[:, p * D_MODEL:(p + 1) * D_MODEL],
                       preferred_element_type=F32)

    gu_scr[...] = _gelu_tanh(piece(0))
    gv = _gelu_tanh(piece(1))

    row = _block_token(lax.broadcasted_iota(jnp.int32, (BLOCK, BLOCK), 0))
    col = _block_token(lax.broadcasted_iota(jnp.int32, (BLOCK, BLOCK), 1))
    causal = row >= col
    for g in range(GMLP_GROUPS):
        ls = slice(g * HEAD_DIM, (g + 1) * HEAD_DIM)
        xg = gv[:, ls]
        mu = jnp.mean(xg, axis=-1, keepdims=True)
        xc = xg - mu
        var = jnp.mean(xc * xc, axis=-1, keepdims=True)
        vn = (xc * lax.rsqrt(var + EPS)).astype(BF16)
        rhs = jnp.concatenate(
            [vn[n * BLOCK:(n + 1) * BLOCK, :] for n in range(nblk)], axis=1)
        wc = jnp.where(causal, ws_ref[g], 0.0).astype(BF16)
        mixed = jnp.dot(wc, rhs, preferred_element_type=F32)
        bias = jnp.broadcast_to(bs_ref[:, g:g + 1], (BLOCK, BLOCK))
        for n in range(nblk):
            rs = slice(n * BLOCK, (n + 1) * BLOCK)
            mx = mixed[:, n * BLOCK:(n + 1) * BLOCK] + bias
            yap_ref[rs, ls] = (gu_scr[rs, ls] * mx).astype(BF16)

    zq = piece(2)
    q_ref[...] = zq * _sigmoid(zq)

    zf = piece(3)
    lb = lb_ref[...]
    e = jnp.exp(-jnp.abs(zf))
    r = 1.0 / (1.0 + e)
    er = e * r
    pos = zf >= 0.0
    f_ref[...] = lb + (1.0 - lb) * jnp.where(pos, r, er)
    k_ref[...] = (1.0 - lb) * jnp.where(pos, er, r)

    v_ref[...] = piece(4).astype(BF16)
    sog_ref[...] = _sigmoid(piece(5)).astype(BF16)
    sga_ref[...] = _sigmoid(piece(6)).astype(BF16)
    sgb_ref[...] = _sigmoid(piece(7)).astype(BF16)


def _mix_in(h16, mod, nw, w_in, ws_p, bs_p, lb, *, seq):
    n = h16.shape[0] * POSITIONS
    tm = M1_TM
    tiles_per_seq = seq // tm
    tile = pl.BlockSpec((tm, D_MODEL), lambda i: (i, 0))
    const2 = lambda i: (0, 0)
    f32_out = jax.ShapeDtypeStruct((n, D_MODEL), F32)
    bf_out = jax.ShapeDtypeStruct((n, D_MODEL), BF16)
    return pl.pallas_call(
        _mix_in_kernel,
        grid=(n // tm,),
        in_specs=[
            pl.BlockSpec((tm // POSITIONS, CHUNK_WIDTH), lambda i: (i, 0)),
            pl.BlockSpec((None, N_ADA, D_MODEL), lambda i: (i // tiles_per_seq, 0, 0)),
            pl.BlockSpec((1, D_MODEL), const2),
            pl.BlockSpec((D_MODEL, N_PIECES * D_MODEL), const2,
                         pipeline_mode=pl.Buffered(1)),
            pl.BlockSpec((GMLP_GROUPS, BLOCK, BLOCK), lambda i: (0, 0, 0)),
            pl.BlockSpec((BLOCK, GMLP_GROUPS), const2),
            pl.BlockSpec((1, D_MODEL), const2),
        ],
        out_specs=[tile] * 8,
        out_shape=[bf_out, f32_out, f32_out, f32_out, bf_out, bf_out, bf_out, bf_out],
        scratch_shapes=[pltpu.VMEM((tm, D_MODEL), F32)],
        compiler_params=pltpu.CompilerParams(
            dimension_semantics=("parallel",), vmem_limit_bytes=VMEM_LIMIT),
        name="mixer_in_proj",
    )(h16, mod, nw, w_in, ws_p, bs_p, lb)


HG_TM = 512
N_MASKS = 7


def _seg_scan_rows(x, rowid, width, reverse):
    d = 1
    y = x
    while d < width:
        if reverse:
            shifted = pltpu.roll(y, CHUNKS - d, axis=0)
            ok = (rowid & (width - 1)) < width - d
        else:
            shifted = pltpu.roll(y, d, axis=0)
            ok = (rowid & (width - 1)) >= d
        y = y * jnp.where(ok, shifted, 1.0)
        d *= 2
    return y


def _hgrn_kernel(q_ref, k_ref, f_ref, v_ref, o_ref, st_ref, mask_ref):
    n_blocks = q_ref.shape[0] // BLOCK

    @pl.when(pl.program_id(1) == 0)
    def _():
        st_ref[...] = jnp.zeros_like(st_ref)

    pr = lax.broadcasted_iota(jnp.int32, (BLOCK, BLOCK), 0)
    pc = lax.broadcasted_iota(jnp.int32, (BLOCK, BLOCK), 1)
    same_chunk = ((pr ^ pc) & (CHUNKS - 1)) == 0
    mask_ref[0] = jnp.where(pr == pc, 1.0, 0.0)
    for beta in range(4):
        sh = 3 + beta + 1
        mask_ref[1 + beta] = jnp.where(same_chunk & ((pr >> sh) == (pc >> sh)), 1.0, 0.0)
    for gamma in range(2):
        sh = gamma + 1
        mask_ref[5 + gamma] = jnp.where(
            ((pr & (CHUNKS - 1)) >> sh) == ((pc & (CHUNKS - 1)) >> sh), 1.0, 0.0)

    rowid = lax.broadcasted_iota(jnp.int32, (CHUNKS, HEAD_DIM), 0)
    zero = jnp.zeros((CHUNKS, HEAD_DIM), F32)
    lane_contract = (((1,), (1,)), ((), ()))
    row_contract = (((0,), (0,)), ((), ()))

    def stack(slabs):
        return jnp.concatenate(slabs, axis=0).astype(BF16)

    def score(q_slabs, k_slabs):
        return lax.dot_general(stack(q_slabs), stack(k_slabs), lane_contract,
                               preferred_element_type=F32)

    def block_step(n, carry):
        r0 = pl.multiple_of(n * BLOCK, BLOCK)
        for h in range(N_HEADS):
            ls = slice(h * HEAD_DIM, (h + 1) * HEAD_DIM)

            def slabs(ref):
                return [ref[pl.ds(r0 + j * CHUNKS, CHUNKS), ls] for j in range(POSITIONS)]

            fs, qs, ks = slabs(f_ref), slabs(q_ref), slabs(k_ref)
            vb = v_ref[pl.ds(r0, BLOCK), ls]

            p = score(qs, ks) * mask_ref[0]

            for beta in range(4):
                w = 1 << beta
                a = [None] * POSITIONS
                for j in range(POSITIONS):
                    if j & w:
                        a[j] = fs[j] if (j & (w - 1)) == 0 else a[j - 1] * fs[j]
                b = [None] * POSITIONS
                for j in range(POSITIONS - 1, -1, -1):
                    if not (j & w) and (j & (w - 1)) != w - 1:
                        b[j] = fs[j + 1] if b[j + 1] is None else b[j + 1] * fs[j + 1]
                ql = [qs[j] * a[j] if (j & w) else zero for j in range(POSITIONS)]
                kl = [zero if (j & w) else (ks[j] if b[j] is None else ks[j] * b[j])
                      for j in range(POSITIONS)]
                p = p + score(ql, kl) * mask_ref[1 + beta]

            gcum = [fs[0]]
            for j in range(1, POSITIONS):
                gcum.append(gcum[-1] * fs[j])
            hsuf = [None] * POSITIONS
            for j in range(POSITIONS - 2, -1, -1):
                hsuf[j] = fs[j + 1] if hsuf[j + 1] is None else hsuf[j + 1] * fs[j + 1]
            g = gcum[-1]
            qg = [qs[j] * gcum[j] for j in range(POSITIONS)]
            kh = [ks[j] if hsuf[j] is None else ks[j] * hsuf[j] for j in range(POSITIONS)]
            g_prev = pltpu.roll(g, 1, axis=0)
            g_next = pltpu.roll(g, CHUNKS - 1, axis=0)

            for gamma in range(3):
                w = 1 << gamma
                low = rowid & (w - 1)
                ca = _seg_scan_rows(jnp.where(low == 0, 1.0, g_prev), rowid, w, False)
                ca = jnp.where((rowid & w) != 0, ca, 0.0)
                cb = _seg_scan_rows(jnp.where(low == w - 1, 1.0, g_next), rowid, w, True)
                cb = jnp.where((rowid & w) == 0, cb, 0.0)
                s = score([x * ca for x in qg], [x * cb for x in kh])
                p = p + (s * mask_ref[5 + gamma] if gamma < 2 else s)

            cp = _seg_scan_rows(jnp.where(rowid == 0, 1.0, g_prev), rowid, CHUNKS, False)
            cs = _seg_scan_rows(jnp.where(rowid == CHUNKS - 1, 1.0, g_next), rowid, CHUNKS, True)
            total = (cp * g)[CHUNKS - 1:CHUNKS, :]
            st = st_ref[h]
            o_state = lax.dot_general(stack([x * cp for x in qg]), st.astype(BF16),
                                      lane_contract, preferred_element_type=F32)
            upd = lax.dot_general(vb, stack([x * cs for x in kh]), row_contract,
                                  preferred_element_type=F32)
            st_ref[h] = st * total + upd
            o_ref[pl.ds(r0, BLOCK), ls] = o_state + jnp.dot(
                p.astype(BF16), vb, preferred_element_type=F32)
        return carry

    lax.fori_loop(0, n_blocks, block_step, 0)


def _hgrn(q, k, f, v, *, batch, seq):
    n = q.shape[0]
    tiles_per_seq = seq // HG_TM
    tile = pl.BlockSpec((HG_TM, D_MODEL), lambda b, t: (b * tiles_per_seq + t, 0))
    return pl.pallas_call(
        _hgrn_kernel,
        grid=(batch, tiles_per_seq),
        in_specs=[tile] * 4,
        out_specs=tile,
        out_shape=jax.ShapeDtypeStruct((n, D_MODEL), F32),
        scratch_shapes=[
            pltpu.VMEM((N_HEADS, HEAD_DIM, HEAD_DIM), F32),
            pltpu.VMEM((N_MASKS, BLOCK, BLOCK), F32),
        ],
        compiler_params=pltpu.CompilerParams(
            dimension_semantics=("parallel", "arbitrary"),
            vmem_limit_bytes=VMEM_LIMIT),
        name="hgrn2_recurrence",
    )(q, k, f, v)


M3_TM = 512


def _mix_out_kernel(x_ref, mod_ref, yap_ref, o_ref, sog_ref, sga_ref, sgb_ref,
                    wa_ref, wb_ref, wo_ref, out_ref):
    nblk = yap_ref.shape[0] // BLOCK
    gate = mod_ref[5:6, :]
    o = o_ref[...]
    parts = []
    for h in range(N_HEADS):
        oh = o[:, h * HEAD_DIM:(h + 1) * HEAD_DIM]
        ms = jnp.mean(oh * oh, axis=-1, keepdims=True)
        parts.append(oh * lax.rsqrt(ms + EPS))
    on = (jnp.concatenate(parts, axis=1) * sog_ref[...].astype(F32)).astype(BF16)
    ya = jnp.dot(yap_ref[...], wa_ref[...], preferred_element_type=F32)
    yb = jnp.dot(on, wb_ref[...], preferred_element_type=F32)
    m = (sga_ref[...].astype(F32) * ya + sgb_ref[...].astype(F32) * yb).astype(BF16)
    out = jnp.dot(m, wo_ref[...], preferred_element_type=F32)
    y = _gather_block_rows(x_ref, nblk) + (1.0 + gate) * out
    _scatter_block_rows(out_ref, y, nblk)


def _mix_out(h16, mod, yap, o, sog, sga, sgb, w_a, w_b, w_o, *, seq):
    n = h16.shape[0] * POSITIONS
    tm = M3_TM
    tiles_per_seq = seq // tm
    tile = pl.BlockSpec((tm, D_MODEL), lambda i: (i, 0))
    tile16 = pl.BlockSpec((tm // POSITIONS, CHUNK_WIDTH), lambda i: (i, 0))
    wspec = pl.BlockSpec((D_MODEL, D_MODEL), lambda i: (0, 0),
                         pipeline_mode=pl.Buffered(1))
    return pl.pallas_call(
        _mix_out_kernel,
        grid=(n // tm,),
        in_specs=[
            tile16,
            pl.BlockSpec((None, N_ADA, D_MODEL), lambda i: (i // tiles_per_seq, 0, 0)),
            tile, tile, tile, tile, tile,
            wspec, wspec, wspec,
        ],
        out_specs=tile16,
        out_shape=jax.ShapeDtypeStruct(h16.shape, F32),
        compiler_params=pltpu.CompilerParams(
            dimension_semantics=("parallel",), vmem_limit_bytes=VMEM_LIMIT),
        name="mixer_out_proj",
    )(h16, mod, yap, o, sog, sga, sgb, w_a, w_b, w_o)


def kernel(x, c, w_in, gmlp_ws, gmlp_b, hgrn_lb_logits, w_branch_a, w_branch_b, w_out,
           ffn_w13, ffn_w2, norm_w, ada_w, ada_b, final_norm_w):
    batch, seq, d = x.shape
    assert d == D_MODEL and seq % HG_TM == 0 and seq % FFN_TM == 0
    n = batch * seq

    mod_all = _ada_mod(c, ada_w, ada_b).reshape(DEPTH, batch, N_ADA, D_MODEL)
    lb_all = _lower_bounds(hgrn_lb_logits)
    perm = jnp.array([_block_token(p) for p in range(BLOCK)], dtype=jnp.int32)
    ws_p = gmlp_ws[:, :, perm, :][:, :, :, perm]
    bs_p = jnp.swapaxes(gmlp_b[:, :, perm], 1, 2)

    h = x.reshape(n, D_MODEL)
    for l in range(DEPTH):
        mod = mod_all[l]
        nw = norm_w[l].reshape(3, 1, D_MODEL)
        h = _ffn(h, mod, nw[0], ffn_w13[l, 0].astype(BF16), ffn_w2[l, 0].astype(BF16),
                 mod_off=0, seq=seq)
        h16 = h.reshape(n // POSITIONS, CHUNK_WIDTH)
        yap, q, k, f, v, sog, sga, sgb = _mix_in(
            h16, mod, nw[1], w_in[l].astype(BF16), ws_p[l], bs_p[l],
            lb_all[l].reshape(1, D_MODEL), seq=seq)
        o = _hgrn(q, k, f, v, batch=batch, seq=seq)
        h16 = _mix_out(h16, mod, yap, o, sog, sga, sgb, w_branch_a[l].astype(BF16),
                       w_branch_b[l].astype(BF16), w_out[l].astype(BF16), seq=seq)
        h = h16.reshape(n, D_MODEL)
        final_w = final_norm_w.reshape(1, D_MODEL) if l == DEPTH - 1 else None
        h = _ffn(h, mod, nw[2], ffn_w13[l, 1].astype(BF16), ffn_w2[l, 1].astype(BF16),
                 mod_off=6, seq=seq, final_w=final_w)
    return h.reshape(batch, seq, D_MODEL)
```

```python
import functools

import jax
import jax.numpy as jnp
from jax import lax
from jax.experimental import pallas as pl
from jax.experimental.pallas import tpu as pltpu

F32 = jnp.float32
BF16 = jnp.bfloat16

D_MODEL = 1024
DEPTH = 4
FFN_HIDDEN = 2816
N_ADA = 9
EPS = 1e-6
GMLP_GROUPS = 8
HEAD_DIM = 128
N_HEADS = D_MODEL // HEAD_DIM
N_PIECES = 8

BLOCK = 128
POSITIONS = 16
CHUNKS = BLOCK // POSITIONS
CHUNK_WIDTH = POSITIONS * D_MODEL

VMEM_LIMIT = 56 * 1024 * 1024


def _sigmoid(x):
    return 1.0 / (1.0 + jnp.exp(-x))


def _gelu_tanh(x):
    c = 0.7978845608028654
    return 0.5 * x * (1.0 + jnp.tanh(c * (x + 0.044715 * (x * x * x))))


def _rms_modulate(x, nw, scale, shift):
    ms = jnp.mean(x * x, axis=-1, keepdims=True)
    return (x * lax.rsqrt(ms + EPS) * nw) * (1.0 + scale) + shift


def _block_token(p):
    return (p % CHUNKS) * POSITIONS + p // CHUNKS


def _gather_block_rows(x_ref, n_blocks):
    pieces = [x_ref[n * CHUNKS:(n + 1) * CHUNKS, j * D_MODEL:(j + 1) * D_MODEL]
              for n in range(n_blocks) for j in range(POSITIONS)]
    return jnp.concatenate(pieces, axis=0)


def _scatter_block_rows(out_ref, y, n_blocks):
    for n in range(n_blocks):
        for j in range(POSITIONS):
            r = (n * POSITIONS + j) * CHUNKS
            out_ref[n * CHUNKS:(n + 1) * CHUNKS, j * D_MODEL:(j + 1) * D_MODEL] = y[r:r + CHUNKS, :]


def _ada_kernel(c_ref, w_ref, b_ref, o_ref):
    c = c_ref[...]
    ca = c * _sigmoid(c)
    o_ref[...] = jnp.dot(ca, w_ref[...], preferred_element_type=F32,
                         precision=lax.Precision.HIGHEST) + b_ref[...]


def _ada_mod(c, ada_w, ada_b):
    b = c.shape[0]
    return pl.pallas_call(
        _ada_kernel,
        grid=(DEPTH, N_ADA),
        in_specs=[
            pl.BlockSpec((b, D_MODEL), lambda l, j: (0, 0)),
            pl.BlockSpec((None, D_MODEL, D_MODEL), lambda l, j: (l, 0, j)),
            pl.BlockSpec((None, 1, D_MODEL), lambda l, j: (l, 0, j)),
        ],
        out_specs=pl.BlockSpec((None, b, D_MODEL), lambda l, j: (l, 0, j)),
        out_shape=jax.ShapeDtypeStruct((DEPTH, b, N_ADA * D_MODEL), F32),
        name="ada_mod",
    )(c, ada_w, ada_b.reshape(DEPTH, 1, N_ADA * D_MODEL))


def _lb_kernel(x_ref, o_ref):
    x = x_ref[...]
    rows = [x[i:i + 1, :] for i in range(DEPTH)]
    m = rows[0]
    for r in rows[1:]:
        m = jnp.maximum(m, r)
    e = [jnp.exp(r - m) for r in rows]
    tot = e[0]
    for t in e[1:]:
        tot = tot + t
    p = [t / tot for t in e]
    cum = p[0]
    first = cum
    for i in range(DEPTH):
        if i > 0:
            cum = cum + p[i]
        o_ref[i:i + 1, :] = cum - first


def _lower_bounds(logits):
    return pl.pallas_call(
        _lb_kernel,
        out_shape=jax.ShapeDtypeStruct(logits.shape, F32),
        name="hgrn_lower_bounds",
    )(logits.astype(F32))


FFN_TM = 512
FFN_COL_CHUNKS = ((0, 1024), (1024, 2048), (2048, FFN_HIDDEN))


def _ffn_kernel(x_ref, mod_ref, nw_ref, w13_ref, w2_ref, *rest, mod_off, final):
    if final:
        fnw_ref, o_ref = rest
    else:
        (o_ref,) = rest
    x = jnp.concatenate(
        [x_ref[:, j * D_MODEL:(j + 1) * D_MODEL] for j in range(POSITIONS)], axis=0)
    rows = x_ref.shape[0]
    shift = mod_ref[mod_off:mod_off + 1, :]
    scale = mod_ref[mod_off + 1:mod_off + 2, :]
    gate = mod_ref[mod_off + 2:mod_off + 3, :]
    hb = _rms_modulate(x, nw_ref[...], scale, shift).astype(BF16)
    acc = None
    for c0, c1 in FFN_COL_CHUNKS:
        a = jnp.dot(hb, w13_ref[:, c0:c1], preferred_element_type=F32)
        b = jnp.dot(hb, w13_ref[:, FFN_HIDDEN + c0:FFN_HIDDEN + c1],
                    preferred_element_type=F32)
        act = (a * _sigmoid(a) * b).astype(BF16)
        part = jnp.dot(act, w2_ref[c0:c1, :], preferred_element_type=F32)
        acc = part if acc is None else acc + part
    y = x + (0.5 * (1.0 + gate)) * acc
    if final:
        ms = jnp.mean(y * y, axis=-1, keepdims=True)
        y = y * lax.rsqrt(ms + EPS) * fnw_ref[...]
    for j in range(POSITIONS):
        o_ref[:, j * D_MODEL:(j + 1) * D_MODEL] = y[j * rows:(j + 1) * rows, :]


def _ffn(h16, mod, nw, w13, w2, *, mod_off, seq, final_w=None):
    n = h16.shape[0] * POSITIONS
    tm = FFN_TM
    tiles_per_seq = seq // tm
    final = final_w is not None
    tile16 = pl.BlockSpec((tm // POSITIONS, CHUNK_WIDTH), lambda i: (i, 0))
    in_specs = [
        tile16,
        pl.BlockSpec((None, N_ADA, D_MODEL), lambda i: (i // tiles_per_seq, 0, 0)),
        pl.BlockSpec((1, D_MODEL), lambda i: (0, 0)),
        pl.BlockSpec((D_MODEL, 2 * FFN_HIDDEN), lambda i: (0, 0),
                     pipeline_mode=pl.Buffered(1)),
        pl.BlockSpec((FFN_HIDDEN, D_MODEL), lambda i: (0, 0),
                     pipeline_mode=pl.Buffered(1)),
    ]
    args = [h16, mod, nw, w13, w2]
    if final:
        in_specs.append(pl.BlockSpec((1, D_MODEL), lambda i: (0, 0)))
        args.append(final_w)
    return pl.pallas_call(
        functools.partial(_ffn_kernel, mod_off=mod_off, final=final),
        grid=(n // tm,),
        in_specs=in_specs,
        out_specs=tile16,
        out_shape=jax.ShapeDtypeStruct(h16.shape, F32),
        compiler_params=pltpu.CompilerParams(
            dimension_semantics=("parallel",), vmem_limit_bytes=VMEM_LIMIT),
        name="ffn_half_step",
    )(*args)


M1_TM = 512


def _mix_in_kernel(x_ref, mod_ref, nw_ref, win_ref, ws_ref, bs_ref, lb_ref,
                   yap_ref, q_ref, k_ref, f_ref, v_ref, sog_ref, sga_ref, sgb_ref,
                   gu_scr):
    tm = yap_ref.shape[0]
    nblk = tm // BLOCK
    x = _gather_block_rows(x_ref, nblk)
    shift = mod_ref[3:4, :]
    scale = mod_ref[4:5, :]
    hb = _rms_modulate(x, nw_ref[...], scale, shift).astype(BF16)

    def piece(p):
        return jnp.dot(hb, win_ref[:, p * D_MODEL:(p + 1) * D_MODEL],
                       preferred_element_type=F32)

    gu_scr[...] = _gelu_tanh(piece(0))
    gv = _gelu_tanh(piece(1))

    row = _block_token(lax.broadcasted_iota(jnp.int32, (BLOCK, BLOCK), 0))
    col = _block_token(lax.broadcasted_iota(jnp.int32, (BLOCK, BLOCK), 1))
    causal = row >= col
    for g in range(GMLP_GROUPS):
        ls = slice(g * HEAD_DIM, (g + 1) * HEAD_DIM)
        xg = gv[:, ls]
        mu = jnp.mean(xg, axis=-1, keepdims=True)
        xc = xg - mu
        var = jnp.mean(xc * xc, axis=-1, keepdims=True)
        vn = (xc * lax.rsqrt(var + EPS)).astype(BF16)
        rhs = jnp.concatenate(
            [vn[n * BLOCK:(n + 1) * BLOCK, :] for n in range(nblk)], axis=1)
        wc = jnp.where(causal, ws_ref[g], 0.0).astype(BF16)
        mixed = jnp.dot(wc, rhs, preferred_element_type=F32)
        bias = jnp.broadcast_to(bs_ref[:, g:g + 1], (BLOCK, BLOCK))
        for n in range(nblk):
            rs = slice(n * BLOCK, (n + 1) * BLOCK)
            mx = mixed[:, n * BLOCK:(n + 1) * BLOCK] + bias
            yap_ref[rs, ls] = (gu_scr[rs, ls] * mx).astype(BF16)

    zq = piece(2)
    q_ref[...] = (zq * _sigmoid(zq)).astype(BF16)

    zf = piece(3)
    lb = lb_ref[...]
    e = jnp.exp(-jnp.abs(zf))
    r = 1.0 / (1.0 + e)
    er = e * r
    pos = zf >= 0.0
    f_ref[...] = lb + (1.0 - lb) * jnp.where(pos, r, er)
    k_ref[...] = ((1.0 - lb) * jnp.where(pos, er, r)).astype(BF16)

    v_ref[...] = piece(4).astype(BF16)
    sog_ref[...] = _sigmoid(piece(5)).astype(BF16)
    sga_ref[...] = _sigmoid(piece(6)).astype(BF16)
    sgb_ref[...] = _sigmoid(piece(7)).astype(BF16)


def _mix_in(h16, mod, nw, w_in, ws_p, bs_p, lb, *, seq):
    n = h16.shape[0] * POSITIONS
    tm = M1_TM
    tiles_per_seq = seq // tm
    tile = pl.BlockSpec((tm, D_MODEL), lambda i: (i, 0))
    const2 = lambda i: (0, 0)
    f32_out = jax.ShapeDtypeStruct((n, D_MODEL), F32)
    bf_out = jax.ShapeDtypeStruct((n, D_MODEL), BF16)
    return pl.pallas_call(
        _mix_in_kernel,
        grid=(n // tm,),
        in_specs=[
            pl.BlockSpec((tm // POSITIONS, CHUNK_WIDTH), lambda i: (i, 0)),
            pl.BlockSpec((None, N_ADA, D_MODEL), lambda i: (i // tiles_per_seq, 0, 0)),
            pl.BlockSpec((1, D_MODEL), const2),
            pl.BlockSpec((D_MODEL, N_PIECES * D_MODEL), const2,
                         pipeline_mode=pl.Buffered(1)),
            pl.BlockSpec((GMLP_GROUPS, BLOCK, BLOCK), lambda i: (0, 0, 0)),
            pl.BlockSpec((BLOCK, GMLP_GROUPS), const2),
            pl.BlockSpec((1, D_MODEL), const2),
        ],
        out_specs=[tile] * 8,
        out_shape=[bf_out, bf_out, bf_out, f32_out, bf_out, bf_out, bf_out, bf_out],
        scratch_shapes=[pltpu.VMEM((tm, D_MODEL), F32)],
        compiler_params=pltpu.CompilerParams(
            dimension_semantics=("parallel",), vmem_limit_bytes=VMEM_LIMIT),
        name="mixer_in_proj",
    )(h16, mod, nw, w_in, ws_p, bs_p, lb)


HG_TM = 512
N_MASKS = 7


def _seg_scan_rows(x, rowid, width, reverse):
    d = 1
    y = x
    while d < width:
        if reverse:
            shifted = pltpu.roll(y, CHUNKS - d, axis=0)
            ok = (rowid & (width - 1)) < width - d
        else:
            shifted = pltpu.roll(y, d, axis=0)
            ok = (rowid & (width - 1)) >= d
        y = y * jnp.where(ok, shifted, 1.0)
        d *= 2
    return y


def _hgrn_kernel(q_ref, k_ref, f_ref, v_ref, o_ref, st_ref, mask_ref):
    n_blocks = q_ref.shape[0] // BLOCK

    @pl.when(pl.program_id(1) == 0)
    def _():
        st_ref[...] = jnp.zeros_like(st_ref)

    pr = lax.broadcasted_iota(jnp.int32, (BLOCK, BLOCK), 0)
    pc = lax.broadcasted_iota(jnp.int32, (BLOCK, BLOCK), 1)
    same_chunk = ((pr ^ pc) & (CHUNKS - 1)) == 0
    mask_ref[0] = jnp.where(pr == pc, 1.0, 0.0)
    for beta in range(4):
        sh = 3 + beta + 1
        mask_ref[1 + beta] = jnp.where(same_chunk & ((pr >> sh) == (pc >> sh)), 1.0, 0.0)
    for gamma in range(2):
        sh = gamma + 1
        mask_ref[5 + gamma] = jnp.where(
            ((pr & (CHUNKS - 1)) >> sh) == ((pc & (CHUNKS - 1)) >> sh), 1.0, 0.0)

    rowid = lax.broadcasted_iota(jnp.int32, (CHUNKS, HEAD_DIM), 0)
    zero = jnp.zeros((CHUNKS, HEAD_DIM), F32)
    lane_contract = (((1,), (1,)), ((), ()))
    row_contract = (((0,), (0,)), ((), ()))

    def stack(slabs):
        return jnp.concatenate(slabs, axis=0).astype(BF16)

    def score(q_slabs, k_slabs):
        return lax.dot_general(stack(q_slabs), stack(k_slabs), lane_contract,
                               preferred_element_type=F32)

    def block_step(n, carry):
        r0 = pl.multiple_of(n * BLOCK, BLOCK)
        for h in range(N_HEADS):
            ls = slice(h * HEAD_DIM, (h + 1) * HEAD_DIM)

            def slabs(ref):
                val = ref[pl.ds(r0, BLOCK), ls].astype(F32)
                return [val[j * CHUNKS:(j + 1) * CHUNKS, :] for j in range(POSITIONS)]

            fs, qs, ks = slabs(f_ref), slabs(q_ref), slabs(k_ref)
            vb = v_ref[pl.ds(r0, BLOCK), ls]

            p = score(qs, ks) * mask_ref[0]

            for beta in range(4):
                w = 1 << beta
                a = [None] * POSITIONS
                for j in range(POSITIONS):
                    if j & w:
                        a[j] = fs[j] if (j & (w - 1)) == 0 else a[j - 1] * fs[j]
                b = [None] * POSITIONS
                for j in range(POSITIONS - 1, -1, -1):
                    if not (j & w) and (j & (w - 1)) != w - 1:
                        b[j] = fs[j + 1] if b[j + 1] is None else b[j + 1] * fs[j + 1]
                ql = [qs[j] * a[j] if (j & w) else zero for j in range(POSITIONS)]
                kl = [zero if (j & w) else (ks[j] if b[j] is None else ks[j] * b[j])
                      for j in range(POSITIONS)]
                p = p + score(ql, kl) * mask_ref[1 + beta]

            gcum = [fs[0]]
            for j in range(1, POSITIONS):
                gcum.append(gcum[-1] * fs[j])
            hsuf = [None] * POSITIONS
            for j in range(POSITIONS - 2, -1, -1):
                hsuf[j] = fs[j + 1] if hsuf[j + 1] is None else hsuf[j + 1] * fs[j + 1]
            g = gcum[-1]
            qg = [qs[j] * gcum[j] for j in range(POSITIONS)]
            kh = [ks[j] if hsuf[j] is None else ks[j] * hsuf[j] for j in range(POSITIONS)]
            g_prev = pltpu.roll(g, 1, axis=0)
            g_next = pltpu.roll(g, CHUNKS - 1, axis=0)

            for gamma in range(3):
                w = 1 << gamma
                low = rowid & (w - 1)
                ca = _seg_scan_rows(jnp.where(low == 0, 1.0, g_prev), rowid, w, False)
                ca = jnp.where((rowid & w) != 0, ca, 0.0)
                cb = _seg_scan_rows(jnp.where(low == w - 1, 1.0, g_next), rowid, w, True)
                cb = jnp.where((rowid & w) == 0, cb, 0.0)
                s = score([x * ca for x in qg], [x * cb for x in kh])
                p = p + (s * mask_ref[5 + gamma] if gamma < 2 else s)

            cp = _seg_scan_rows(jnp.where(rowid == 0, 1.0, g_prev), rowid, CHUNKS, False)
            cs = _seg_scan_rows(jnp.where(rowid == CHUNKS - 1, 1.0, g_next), rowid, CHUNKS, True)
            total = (cp * g)[CHUNKS - 1:CHUNKS, :]
            st = st_ref[h]
            o_state = lax.dot_general(stack([x * cp for x in qg]), st.astype(BF16),
                                      lane_contract, preferred_element_type=F32)
            upd = lax.dot_general(vb, stack([x * cs for x in kh]), row_contract,
                                  preferred_element_type=F32)
            st_ref[h] = st * total + upd
            o_ref[pl.ds(r0, BLOCK), ls] = o_state + jnp.dot(
                p.astype(BF16), vb, preferred_element_type=F32)
        return carry

    lax.fori_loop(0, n_blocks, block_step, 0)


def _hgrn(q, k, f, v, *, batch, seq):
    n = q.shape[0]
    tiles_per_seq = seq // HG_TM
    tile = pl.BlockSpec((HG_TM, D_MODEL), lambda b, t: (b * tiles_per_seq + t, 0))
    return pl.pallas_call(
        _hgrn_kernel,
        grid=(batch, tiles_per_seq),
        in_specs=[tile] * 4,
        out_specs=tile,
        out_shape=jax.ShapeDtypeStruct((n, D_MODEL), F32),
        scratch_shapes=[
            pltpu.VMEM((N_HEADS, HEAD_DIM, HEAD_DIM), F32),
            pltpu.VMEM((N_MASKS, BLOCK, BLOCK), F32),
        ],
        compiler_params=pltpu.CompilerParams(
            dimension_semantics=("parallel", "arbitrary"),
            vmem_limit_bytes=VMEM_LIMIT),
        name="hgrn2_recurrence",
    )(q, k, f, v)


M3_TM = 512


def _mix_out_kernel(x_ref, mod_ref, yap_ref, o_ref, sog_ref, sga_ref, sgb_ref,
                    wa_ref, wb_ref, wo_ref, out_ref):
    nblk = yap_ref.shape[0] // BLOCK
    gate = mod_ref[5:6, :]
    o = o_ref[...]
    parts = []
    for h in range(N_HEADS):
        oh = o[:, h * HEAD_DIM:(h + 1) * HEAD_DIM]
        ms = jnp.mean(oh * oh, axis=-1, keepdims=True)
        parts.append(oh * lax.rsqrt(ms + EPS))
    on = (jnp.concatenate(parts, axis=1) * sog_ref[...].astype(F32)).astype(BF16)
    ya = jnp.dot(yap_ref[...], wa_ref[...], preferred_element_type=F32)
    yb = jnp.dot(on, wb_ref[...], preferred_element_type=F32)
    m = (sga_ref[...].astype(F32) * ya + sgb_ref[...].astype(F32) * yb).astype(BF16)
    out = jnp.dot(m, wo_ref[...], preferred_element_type=F32)
    y = _gather_block_rows(x_ref, nblk) + (1.0 + gate) * out
    _scatter_block_rows(out_ref, y, nblk)


def _mix_out(h16, mod, yap, o, sog, sga, sgb, w_a, w_b, w_o, *, seq):
    n = h16.shape[0] * POSITIONS
    tm = M3_TM
    tiles_per_seq = seq // tm
    tile = pl.BlockSpec((tm, D_MODEL), lambda i: (i, 0))
    tile16 = pl.BlockSpec((tm // POSITIONS, CHUNK_WIDTH), lambda i: (i, 0))
    wspec = pl.BlockSpec((D_MODEL, D_MODEL), lambda i: (0, 0),
                         pipeline_mode=pl.Buffered(1))
    return pl.pallas_call(
        _mix_out_kernel,
        grid=(n // tm,),
        in_specs=[
            tile16,
            pl.BlockSpec((None, N_ADA, D_MODEL), lambda i: (i // tiles_per_seq, 0, 0)),
            tile, tile, tile, tile, tile,
            wspec, wspec, wspec,
        ],
        out_specs=tile16,
        out_shape=jax.ShapeDtypeStruct(h16.shape, F32),
        compiler_params=pltpu.CompilerParams(
            dimension_semantics=("parallel",), vmem_limit_bytes=VMEM_LIMIT),
        name="mixer_out_proj",
    )(h16, mod, yap, o, sog, sga, sgb, w_a, w_b, w_o)


def kernel(x, c, w_in, gmlp_ws, gmlp_b, hgrn_lb_logits, w_branch_a, w_branch_b, w_out,
           ffn_w13, ffn_w2, norm_w, ada_w, ada_b, final_norm_w):
    batch, seq, d = x.shape
    assert d == D_MODEL and seq % HG_TM == 0 and seq % FFN_TM == 0
    n = batch * seq

    mod_all = _ada_mod(c, ada_w, ada_b).reshape(DEPTH, batch, N_ADA, D_MODEL)
    lb_all = _lower_bounds(hgrn_lb_logits)
    perm = jnp.array([_block_token(p) for p in range(BLOCK)], dtype=jnp.int32)
    ws_p = gmlp_ws[:, :, perm, :][:, :, :, perm]
    bs_p = jnp.swapaxes(gmlp_b[:, :, perm], 1, 2)

    h16 = x.reshape(n // POSITIONS, CHUNK_WIDTH)
    for l in range(DEPTH):
        mod = mod_all[l]
        nw = norm_w[l].reshape(3, 1, D_MODEL)
        h16 = _ffn(h16, mod, nw[0], ffn_w13[l, 0].astype(BF16), ffn_w2[l, 0].astype(BF16),
                   mod_off=0, seq=seq)
        yap, q, k, f, v, sog, sga, sgb = _mix_in(
            h16, mod, nw[1], w_in[l].astype(BF16), ws_p[l], bs_p[l],
            lb_all[l].reshape(1, D_MODEL), seq=seq)
        o = _hgrn(q, k, f, v, batch=batch, seq=seq)
        h16 = _mix_out(h16, mod, yap, o, sog, sga, sgb, w_branch_a[l].astype(BF16),
                       w_branch_b[l].astype(BF16), w_out[l].astype(BF16), seq=seq)
        final_w = final_norm_w.reshape(1, D_MODEL) if l == DEPTH - 1 else None
        h16 = _ffn(h16, mod, nw[2], ffn_w13[l, 1].astype(BF16), ffn_w2[l, 1].astype(BF16),
                   mod_off=6, seq=seq, final_w=final_w)
    return h16.reshape(batch, seq, D_MODEL)
```

```python
import functools

import jax
import jax.numpy as jnp
from jax import lax
from jax.experimental import pallas as pl
from jax.experimental.pallas import tpu as pltpu

F32 = jnp.float32
BF16 = jnp.bfloat16

D_MODEL = 1024
DEPTH = 4
FFN_HIDDEN = 2816
N_ADA = 9
EPS = 1e-6
GMLP_GROUPS = 8
HEAD_DIM = 128
N_HEADS = D_MODEL // HEAD_DIM
N_PIECES = 8

BLOCK = 128
POSITIONS = 16
CHUNKS = BLOCK // POSITIONS
CHUNK_WIDTH = POSITIONS * D_MODEL

VMEM_LIMIT = 56 * 1024 * 1024


def _sigmoid(x):
    return 1.0 / (1.0 + jnp.exp(-x))


def _sigmoid_t(x):
    return 0.5 * jnp.tanh(0.5 * x) + 0.5


def _gelu_tanh(x):
    c = 0.7978845608028654
    return 0.5 * x * (1.0 + jnp.tanh(c * (x + 0.044715 * (x * x * x))))


def _rms_modulate(x, nw, scale, shift):
    ms = jnp.mean(x * x, axis=-1, keepdims=True)
    return (x * lax.rsqrt(ms + EPS) * nw) * (1.0 + scale) + shift


def _block_token(p):
    return (p % CHUNKS) * POSITIONS + p // CHUNKS


def _gather_block_rows(x_ref, b0, n_blocks):
    pieces = [x_ref[n * CHUNKS:(n + 1) * CHUNKS, j * D_MODEL:(j + 1) * D_MODEL]
              for n in range(b0, b0 + n_blocks) for j in range(POSITIONS)]
    return jnp.concatenate(pieces, axis=0)


def _scatter_block_rows(out_ref, y, b0, n_blocks):
    for n in range(n_blocks):
        for j in range(POSITIONS):
            r = (n * POSITIONS + j) * CHUNKS
            out_ref[(b0 + n) * CHUNKS:(b0 + n + 1) * CHUNKS,
                    j * D_MODEL:(j + 1) * D_MODEL] = y[r:r + CHUNKS, :]


def _ada_kernel(c_ref, w_ref, b_ref, o_ref):
    c = c_ref[...]
    ca = c * _sigmoid(c)
    o_ref[...] = jnp.dot(ca, w_ref[...], preferred_element_type=F32,
                         precision=lax.Precision.HIGHEST) + b_ref[...]


def _ada_mod(c, ada_w, ada_b):
    b = c.shape[0]
    return pl.pallas_call(
        _ada_kernel,
        grid=(DEPTH, N_ADA),
        in_specs=[
            pl.BlockSpec((b, D_MODEL), lambda l, j: (0, 0)),
            pl.BlockSpec((None, D_MODEL, D_MODEL), lambda l, j: (l, 0, j)),
            pl.BlockSpec((None, 1, D_MODEL), lambda l, j: (l, 0, j)),
        ],
        out_specs=pl.BlockSpec((None, b, D_MODEL), lambda l, j: (l, 0, j)),
        out_shape=jax.ShapeDtypeStruct((DEPTH, b, N_ADA * D_MODEL), F32),
        name="ada_mod",
    )(c, ada_w, ada_b.reshape(DEPTH, 1, N_ADA * D_MODEL))


def _lb_kernel(x_ref, o_ref):
    x = x_ref[...]
    rows = [x[i:i + 1, :] for i in range(DEPTH)]
    m = rows[0]
    for r in rows[1:]:
        m = jnp.maximum(m, r)
    e = [jnp.exp(r - m) for r in rows]
    tot = e[0]
    for t in e[1:]:
        tot = tot + t
    p = [t / tot for t in e]
    cum = p[0]
    first = cum
    for i in range(DEPTH):
        if i > 0:
            cum = cum + p[i]
        o_ref[i:i + 1, :] = cum - first


def _lower_bounds(logits):
    return pl.pallas_call(
        _lb_kernel,
        out_shape=jax.ShapeDtypeStruct(logits.shape, F32),
        name="hgrn_lower_bounds",
    )(logits.astype(F32))


FFN_TM = 1024
FFN_CHAINS = 2
FFN_COL_CHUNKS = ((0, 1024), (1024, 2048), (2048, FFN_HIDDEN))


def _ffn_kernel(x_ref, mod_ref, nw_ref, w13_ref, w2_ref, *rest, mod_off, final):
    if final:
        fnw_ref, o_ref = rest
    else:
        (o_ref,) = rest
    shift = mod_ref[mod_off:mod_off + 1, :]
    scale = mod_ref[mod_off + 1:mod_off + 2, :]
    gate = mod_ref[mod_off + 2:mod_off + 3, :]
    rows = x_ref.shape[0] // FFN_CHAINS
    for r in range(FFN_CHAINS):
        rs = slice(r * rows, (r + 1) * rows)
        x = jnp.concatenate(
            [x_ref[rs, j * D_MODEL:(j + 1) * D_MODEL] for j in range(POSITIONS)], axis=0)
        hb = _rms_modulate(x, nw_ref[...], scale, shift).astype(BF16)
        acc = None
        for c0, c1 in FFN_COL_CHUNKS:
            a = jnp.dot(hb, w13_ref[:, c0:c1], preferred_element_type=F32)
            b = jnp.dot(hb, w13_ref[:, FFN_HIDDEN + c0:FFN_HIDDEN + c1],
                        preferred_element_type=F32)
            act = (a * _sigmoid_t(a) * b).astype(BF16)
            part = jnp.dot(act, w2_ref[c0:c1, :], preferred_element_type=F32)
            acc = part if acc is None else acc + part
        y = x + (0.5 * (1.0 + gate)) * acc
        if final:
            ms = jnp.mean(y * y, axis=-1, keepdims=True)
            y = y * lax.rsqrt(ms + EPS) * fnw_ref[...]
        for j in range(POSITIONS):
            o_ref[rs, j * D_MODEL:(j + 1) * D_MODEL] = y[j * rows:(j + 1) * rows, :]


def _ffn(h16, mod, nw, w13, w2, *, mod_off, seq, final_w=None):
    n = h16.shape[0] * POSITIONS
    tm = FFN_TM
    tiles_per_seq = seq // tm
    final = final_w is not None
    tile16 = pl.BlockSpec((tm // POSITIONS, CHUNK_WIDTH), lambda i: (i, 0))
    in_specs = [
        tile16,
        pl.BlockSpec((None, N_ADA, D_MODEL), lambda i: (i // tiles_per_seq, 0, 0)),
        pl.BlockSpec((1, D_MODEL), lambda i: (0, 0)),
        pl.BlockSpec((D_MODEL, 2 * FFN_HIDDEN), lambda i: (0, 0),
                     pipeline_mode=pl.Buffered(1)),
        pl.BlockSpec((FFN_HIDDEN, D_MODEL), lambda i: (0, 0),
                     pipeline_mode=pl.Buffered(1)),
    ]
    args = [h16, mod, nw, w13, w2]
    if final:
        in_specs.append(pl.BlockSpec((1, D_MODEL), lambda i: (0, 0)))
        args.append(final_w)
    return pl.pallas_call(
        functools.partial(_ffn_kernel, mod_off=mod_off, final=final),
        grid=(n // tm,),
        in_specs=in_specs,
        out_specs=tile16,
        out_shape=jax.ShapeDtypeStruct(h16.shape, F32),
        compiler_params=pltpu.CompilerParams(
            dimension_semantics=("parallel",), vmem_limit_bytes=VMEM_LIMIT),
        name="ffn_half_step",
    )(*args)


M1_TM = 512
M1_PIECES = 2
M1_CHAINS = 2


def _mix_in_kernel(x_ref, mod_ref, nw_ref, win_ref, ws_ref, bs_ref, lb_ref,
                   yap_ref, q_ref, k_ref, f_ref, v_ref, sog_ref, sga_ref, sgb_ref,
                   gu_scr):
    tm = yap_ref.shape[0]
    nblk = tm // BLOCK // M1_CHAINS
    shift = mod_ref[3:4, :]
    scale = mod_ref[4:5, :]
    row = _block_token(lax.broadcasted_iota(jnp.int32, (BLOCK, BLOCK), 0))
    col = _block_token(lax.broadcasted_iota(jnp.int32, (BLOCK, BLOCK), 1))
    causal = row >= col
    lb = lb_ref[...]

    for chain in range(M1_CHAINS):
        b0 = chain * nblk
        rows = slice(b0 * BLOCK, (b0 + nblk) * BLOCK)
        x = _gather_block_rows(x_ref, b0, nblk)
        hb = _rms_modulate(x, nw_ref[...], scale, shift).astype(BF16)

        def project(p0, hb=hb):
            z = jnp.dot(hb, win_ref[:, p0 * D_MODEL:(p0 + M1_PIECES) * D_MODEL],
                        preferred_element_type=F32)
            return [z[:, i * D_MODEL:(i + 1) * D_MODEL] for i in range(M1_PIECES)]

        zu, zv = project(0)
        gu_scr[rows, :] = _gelu_tanh(zu)
        gv = _gelu_tanh(zv)
        for g in range(GMLP_GROUPS):
            ls = slice(g * HEAD_DIM, (g + 1) * HEAD_DIM)
            xg = gv[:, ls]
            mu = jnp.mean(xg, axis=-1, keepdims=True)
            xc = xg - mu
            var = jnp.mean(xc * xc, axis=-1, keepdims=True)
            vn = (xc * lax.rsqrt(var + EPS)).astype(BF16)
            rhs = jnp.concatenate(
                [vn[n * BLOCK:(n + 1) * BLOCK, :] for n in range(nblk)], axis=1)
            wc = jnp.where(causal, ws_ref[g], 0.0).astype(BF16)
            mixed = jnp.dot(wc, rhs, preferred_element_type=F32)
            bias = jnp.broadcast_to(bs_ref[:, g:g + 1], (BLOCK, BLOCK))
            for n in range(nblk):
                rs = slice((b0 + n) * BLOCK, (b0 + n + 1) * BLOCK)
                mx = mixed[:, n * BLOCK:(n + 1) * BLOCK] + bias
                yap_ref[rs, ls] = (gu_scr[rs, ls] * mx).astype(BF16)

        zq, zf = project(2)
        q_ref[rows, :] = (zq * _sigmoid_t(zq)).astype(BF16)
        th = 0.5 * jnp.tanh(0.5 * zf)
        f_ref[rows, :] = lb + (1.0 - lb) * (0.5 + th)
        k_ref[rows, :] = ((1.0 - lb) * (0.5 - th)).astype(BF16)

        zi, zog = project(4)
        v_ref[rows, :] = zi.astype(BF16)
        sog_ref[rows, :] = _sigmoid_t(zog).astype(BF16)
        zga, zgb = project(6)
        sga_ref[rows, :] = _sigmoid_t(zga).astype(BF16)
        sgb_ref[rows, :] = _sigmoid_t(zgb).astype(BF16)


def _mix_in(h16, mod, nw, w_in, ws_p, bs_p, lb, *, seq):
    n = h16.shape[0] * POSITIONS
    tm = M1_TM
    tiles_per_seq = seq // tm
    tile = pl.BlockSpec((tm, D_MODEL), lambda i: (i, 0))
    const2 = lambda i: (0, 0)
    f32_out = jax.ShapeDtypeStruct((n, D_MODEL), F32)
    bf_out = jax.ShapeDtypeStruct((n, D_MODEL), BF16)
    return pl.pallas_call(
        _mix_in_kernel,
        grid=(n // tm,),
        in_specs=[
            pl.BlockSpec((tm // POSITIONS, CHUNK_WIDTH), lambda i: (i, 0)),
            pl.BlockSpec((None, N_ADA, D_MODEL), lambda i: (i // tiles_per_seq, 0, 0)),
            pl.BlockSpec((1, D_MODEL), const2),
            pl.BlockSpec((D_MODEL, N_PIECES * D_MODEL), const2,
                         pipeline_mode=pl.Buffered(1)),
            pl.BlockSpec((GMLP_GROUPS, BLOCK, BLOCK), lambda i: (0, 0, 0)),
            pl.BlockSpec((BLOCK, GMLP_GROUPS), const2),
            pl.BlockSpec((1, D_MODEL), const2),
        ],
        out_specs=[tile] * 8,
        out_shape=[bf_out, bf_out, bf_out, f32_out, bf_out, bf_out, bf_out, bf_out],
        scratch_shapes=[pltpu.VMEM((tm, D_MODEL), F32)],
        compiler_params=pltpu.CompilerParams(
            dimension_semantics=("parallel",), vmem_limit_bytes=VMEM_LIMIT),
        name="mixer_in_proj",
    )(h16, mod, nw, w_in, ws_p, bs_p, lb)


HG_TM = 512
N_MASKS = 7


def _seg_scan_rows(x, rowid, width, reverse):
    d = 1
    y = x
    while d < width:
        if reverse:
            shifted = pltpu.roll(y, CHUNKS - d, axis=0)
            ok = (rowid & (width - 1)) < width - d
        else:
            shifted = pltpu.roll(y, d, axis=0)
            ok = (rowid & (width - 1)) >= d
        y = y * jnp.where(ok, shifted, 1.0)
        d *= 2
    return y


def _hgrn_kernel(q_ref, k_ref, f_ref, v_ref, o_ref, st_ref, mask_ref):
    n_blocks = q_ref.shape[0] // BLOCK

    @pl.when(pl.program_id(1) == 0)
    def _():
        st_ref[...] = jnp.zeros_like(st_ref)

    pr = lax.broadcasted_iota(jnp.int32, (BLOCK, BLOCK), 0)
    pc = lax.broadcasted_iota(jnp.int32, (BLOCK, BLOCK), 1)
    same_chunk = ((pr ^ pc) & (CHUNKS - 1)) == 0
    mask_ref[0] = jnp.where(pr == pc, 1.0, 0.0)
    for beta in range(4):
        sh = 3 + beta + 1
        mask_ref[1 + beta] = jnp.where(same_chunk & ((pr >> sh) == (pc >> sh)), 1.0, 0.0)
    for gamma in range(2):
        sh = gamma + 1
        mask_ref[5 + gamma] = jnp.where(
            ((pr & (CHUNKS - 1)) >> sh) == ((pc & (CHUNKS - 1)) >> sh), 1.0, 0.0)

    rowid = lax.broadcasted_iota(jnp.int32, (CHUNKS, HEAD_DIM), 0)
    zero = jnp.zeros((CHUNKS, HEAD_DIM), F32)
    lane_contract = (((1,), (1,)), ((), ()))
    row_contract = (((0,), (0,)), ((), ()))

    def stack(slabs):
        return jnp.concatenate(slabs, axis=0).astype(BF16)

    def score(q_slabs, k_slabs):
        return lax.dot_general(stack(q_slabs), stack(k_slabs), lane_contract,
                               preferred_element_type=F32)

    def block_step(n, carry):
        r0 = pl.multiple_of(n * BLOCK, BLOCK)
        for h in range(N_HEADS):
            ls = slice(h * HEAD_DIM, (h + 1) * HEAD_DIM)

            def slabs(ref):
                val = ref[pl.ds(r0, BLOCK), ls].astype(F32)
                return [val[j * CHUNKS:(j + 1) * CHUNKS, :] for j in range(POSITIONS)]

            fs, qs, ks = slabs(f_ref), slabs(q_ref), slabs(k_ref)
            vb = v_ref[pl.ds(r0, BLOCK), ls]

            p = score(qs, ks) * mask_ref[0]

            for beta in range(4):
                w = 1 << beta
                a = [None] * POSITIONS
                for j in range(POSITIONS):
                    if j & w:
                        a[j] = fs[j] if (j & (w - 1)) == 0 else a[j - 1] * fs[j]
                b = [None] * POSITIONS
                for j in range(POSITIONS - 1, -1, -1):
                    if not (j & w) and (j & (w - 1)) != w - 1:
                        b[j] = fs[j + 1] if b[j + 1] is None else b[j + 1] * fs[j + 1]
                ql = [qs[j] * a[j] if (j & w) else zero for j in range(POSITIONS)]
                kl = [zero if (j & w) else (ks[j] if b[j] is None else ks[j] * b[j])
                      for j in range(POSITIONS)]
                p = p + score(ql, kl) * mask_ref[1 + beta]

            gcum = [fs[0]]
            for j in range(1, POSITIONS):
                gcum.append(gcum[-1] * fs[j])
            hsuf = [None] * POSITIONS
            for j in range(POSITIONS - 2, -1, -1):
                hsuf[j] = fs[j + 1] if hsuf[j + 1] is None else hsuf[j + 1] * fs[j + 1]
            g = gcum[-1]
            qg = [qs[j] * gcum[j] for j in range(POSITIONS)]
            kh = [ks[j] if hsuf[j] is None else ks[j] * hsuf[j] for j in range(POSITIONS)]
            g_prev = pltpu.roll(g, 1, axis=0)
            g_next = pltpu.roll(g, CHUNKS - 1, axis=0)

            for gamma in range(3):
                w = 1 << gamma
                low = rowid & (w - 1)
                ca = _seg_scan_rows(jnp.where(low == 0, 1.0, g_prev), rowid, w, False)
                ca = jnp.where((rowid & w) != 0, ca, 0.0)
                cb = _seg_scan_rows(jnp.where(low == w - 1, 1.0, g_next), rowid, w, True)
                cb = jnp.where((rowid & w) == 0, cb, 0.0)
                s = score([x * ca for x in qg], [x * cb for x in kh])
                p = p + (s * mask_ref[5 + gamma] if gamma < 2 else s)

            cp = _seg_scan_rows(jnp.where(rowid == 0, 1.0, g_prev), rowid, CHUNKS, False)
            cs = _seg_scan_rows(jnp.where(rowid == CHUNKS - 1, 1.0, g_next), rowid, CHUNKS, True)
            total = (cp * g)[CHUNKS - 1:CHUNKS, :]
            st = st_ref[h]
            o_state = lax.dot_general(stack([x * cp for x in qg]), st.astype(BF16),
                                      lane_contract, preferred_element_type=F32)
            upd = lax.dot_general(vb, stack([x * cs for x in kh]), row_contract,
                                  preferred_element_type=F32)
            st_ref[h] = st * total + upd
            o_ref[pl.ds(r0, BLOCK), ls] = o_state + jnp.dot(
                p.astype(BF16), vb, preferred_element_type=F32)
        return carry

    lax.fori_loop(0, n_blocks, block_step, 0)


def _hgrn(q, k, f, v, *, batch, seq):
    n = q.shape[0]
    tiles_per_seq = seq // HG_TM
    tile = pl.BlockSpec((HG_TM, D_MODEL), lambda b, t: (b * tiles_per_seq + t, 0))
    return pl.pallas_call(
        _hgrn_kernel,
        grid=(batch, tiles_per_seq),
        in_specs=[tile] * 4,
        out_specs=tile,
        out_shape=jax.ShapeDtypeStruct((n, D_MODEL), F32),
        scratch_shapes=[
            pltpu.VMEM((N_HEADS, HEAD_DIM, HEAD_DIM), F32),
            pltpu.VMEM((N_MASKS, BLOCK, BLOCK), F32),
        ],
        compiler_params=pltpu.CompilerParams(
            dimension_semantics=("parallel", "arbitrary"),
            vmem_limit_bytes=VMEM_LIMIT),
        name="hgrn2_recurrence",
    )(q, k, f, v)


M3_TM = 1024
M3_CHAINS = 2


def _mix_out_kernel(x_ref, mod_ref, yap_ref, o_ref, sog_ref, sga_ref, sgb_ref,
                    wa_ref, wb_ref, wo_ref, out_ref):
    nblk = yap_ref.shape[0] // BLOCK // M3_CHAINS
    gate = mod_ref[5:6, :]
    for chain in range(M3_CHAINS):
        b0 = chain * nblk
        rows = slice(b0 * BLOCK, (b0 + nblk) * BLOCK)
        o = o_ref[rows, :]
        parts = []
        for h in range(N_HEADS):
            oh = o[:, h * HEAD_DIM:(h + 1) * HEAD_DIM]
            ms = jnp.mean(oh * oh, axis=-1, keepdims=True)
            parts.append(oh * lax.rsqrt(ms + EPS))
        on = (jnp.concatenate(parts, axis=1) * sog_ref[rows, :].astype(F32)).astype(BF16)
        ya = jnp.dot(yap_ref[rows, :], wa_ref[...], preferred_element_type=F32)
        yb = jnp.dot(on, wb_ref[...], preferred_element_type=F32)
        m = (sga_ref[rows, :].astype(F32) * ya + sgb_ref[rows, :].astype(F32) * yb).astype(BF16)
        out = jnp.dot(m, wo_ref[...], preferred_element_type=F32)
        y = _gather_block_rows(x_ref, b0, nblk) + (1.0 + gate) * out
        _scatter_block_rows(out_ref, y, b0, nblk)


def _mix_out(h16, mod, yap, o, sog, sga, sgb, w_a, w_b, w_o, *, seq):
    n = h16.shape[0] * POSITIONS
    tm = M3_TM
    tiles_per_seq = seq // tm
    tile = pl.BlockSpec((tm, D_MODEL), lambda i: (i, 0))
    tile16 = pl.BlockSpec((tm // POSITIONS, CHUNK_WIDTH), lambda i: (i, 0))
    wspec = pl.BlockSpec((D_MODEL, D_MODEL), lambda i: (0, 0),
                         pipeline_mode=pl.Buffered(1))
    return pl.pallas_call(
        _mix_out_kernel,
        grid=(n // tm,),
        in_specs=[
            tile16,
            pl.BlockSpec((None, N_ADA, D_MODEL), lambda i: (i // tiles_per_seq, 0, 0)),
            tile, tile, tile, tile, tile,
            wspec, wspec, wspec,
        ],
        out_specs=tile16,
        out_shape=jax.ShapeDtypeStruct(h16.shape, F32),
        compiler_params=pltpu.CompilerParams(
            dimension_semantics=("parallel",), vmem_limit_bytes=VMEM_LIMIT),
        name="mixer_out_proj",
    )(h16, mod, yap, o, sog, sga, sgb, w_a, w_b, w_o)


def kernel(x, c, w_in, gmlp_ws, gmlp_b, hgrn_lb_logits, w_branch_a, w_branch_b, w_out,
           ffn_w13, ffn_w2, norm_w, ada_w, ada_b, final_norm_w):
    batch, seq, d = x.shape
    assert d == D_MODEL and seq % HG_TM == 0 and seq % FFN_TM == 0
    n = batch * seq

    mod_all = _ada_mod(c, ada_w, ada_b).reshape(DEPTH, batch, N_ADA, D_MODEL)
    lb_all = _lower_bounds(hgrn_lb_logits)
    perm = jnp.array([_block_token(p) for p in range(BLOCK)], dtype=jnp.int32)
    ws_p = gmlp_ws[:, :, perm, :][:, :, :, perm]
    bs_p = jnp.swapaxes(gmlp_b[:, :, perm], 1, 2)

    h16 = x.reshape(n // POSITIONS, CHUNK_WIDTH)
    for l in range(DEPTH):
        mod = mod_all[l]
        nw = norm_w[l].reshape(3, 1, D_MODEL)
        h16 = _ffn(h16, mod, nw[0], ffn_w13[l, 0].astype(BF16), ffn_w2[l, 0].astype(BF16),
                   mod_off=0, seq=seq)
        yap, q, k, f, v, sog, sga, sgb = _mix_in(
            h16, mod, nw[1], w_in[l].astype(BF16), ws_p[l], bs_p[l],
            lb_all[l].reshape(1, D_MODEL), seq=seq)
        o = _hgrn(q, k, f, v, batch=batch, seq=seq)
        h16 = _mix_out(h16, mod, yap, o, sog, sga, sgb, w_branch_a[l].astype(BF16),
                       w_branch_b[l].astype(BF16), w_out[l].astype(BF16), seq=seq)
        final_w = final_norm_w.reshape(1, D_MODEL) if l == DEPTH - 1 else None
        h16 = _ffn(h16, mod, nw[2], ffn_w13[l, 1].astype(BF16), ffn_w2[l, 1].astype(BF16),
                   mod_off=6, seq=seq, final_w=final_w)
    return h16.reshape(batch, seq, D_MODEL)
```

```python
import functools

import jax
import jax.numpy as jnp
from jax import lax
from jax.experimental import pallas as pl
from jax.experimental.pallas import tpu as pltpu

F32 = jnp.float32
BF16 = jnp.bfloat16

D_MODEL = 1024
DEPTH = 4
FFN_HIDDEN = 2816
N_ADA = 9
EPS = 1e-6
GMLP_GROUPS = 8
HEAD_DIM = 128
N_HEADS = D_MODEL // HEAD_DIM
N_PIECES = 8

BLOCK = 128
POSITIONS = 16
CHUNKS = BLOCK // POSITIONS
CHUNK_WIDTH = POSITIONS * D_MODEL

VMEM_LIMIT = 56 * 1024 * 1024


def _sigmoid(x):
    return 1.0 / (1.0 + jnp.exp(-x))


def _sigmoid_t(x):
    return 0.5 * jnp.tanh(0.5 * x) + 0.5


def _gelu_tanh(x):
    c = 0.7978845608028654
    return 0.5 * x * (1.0 + jnp.tanh(c * (x + 0.044715 * (x * x * x))))


def _rms_modulate(x, nw, scale, shift):
    ms = jnp.mean(x * x, axis=-1, keepdims=True)
    return (x * lax.rsqrt(ms + EPS) * nw) * (1.0 + scale) + shift


def _block_token(p):
    return (p % CHUNKS) * POSITIONS + p // CHUNKS


def _gather_block_rows(x_ref, b0, n_blocks):
    pieces = [x_ref[n * CHUNKS:(n + 1) * CHUNKS, j * D_MODEL:(j + 1) * D_MODEL]
              for n in range(b0, b0 + n_blocks) for j in range(POSITIONS)]
    return jnp.concatenate(pieces, axis=0)


def _scatter_block_rows(out_ref, y, b0, n_blocks):
    for n in range(n_blocks):
        for j in range(POSITIONS):
            r = (n * POSITIONS + j) * CHUNKS
            out_ref[(b0 + n) * CHUNKS:(b0 + n + 1) * CHUNKS,
                    j * D_MODEL:(j + 1) * D_MODEL] = y[r:r + CHUNKS, :]


def _ada_kernel(c_ref, w_ref, b_ref, o_ref):
    c = c_ref[...]
    ca = c * _sigmoid(c)
    o_ref[...] = jnp.dot(ca, w_ref[...], preferred_element_type=F32,
                         precision=lax.Precision.HIGHEST) + b_ref[...]


def _ada_mod(c, ada_w, ada_b):
    b = c.shape[0]
    return pl.pallas_call(
        _ada_kernel,
        grid=(DEPTH, N_ADA),
        in_specs=[
            pl.BlockSpec((b, D_MODEL), lambda l, j: (0, 0)),
            pl.BlockSpec((None, D_MODEL, D_MODEL), lambda l, j: (l, 0, j)),
            pl.BlockSpec((None, 1, D_MODEL), lambda l, j: (l, 0, j)),
        ],
        out_specs=pl.BlockSpec((None, b, D_MODEL), lambda l, j: (l, 0, j)),
        out_shape=jax.ShapeDtypeStruct((DEPTH, b, N_ADA * D_MODEL), F32),
        name="ada_mod",
    )(c, ada_w, ada_b.reshape(DEPTH, 1, N_ADA * D_MODEL))


def _lb_kernel(x_ref, o_ref):
    x = x_ref[...]
    rows = [x[i:i + 1, :] for i in range(DEPTH)]
    m = rows[0]
    for r in rows[1:]:
        m = jnp.maximum(m, r)
    e = [jnp.exp(r - m) for r in rows]
    tot = e[0]
    for t in e[1:]:
        tot = tot + t
    p = [t / tot for t in e]
    cum = p[0]
    first = cum
    for i in range(DEPTH):
        if i > 0:
            cum = cum + p[i]
        o_ref[i:i + 1, :] = cum - first


def _lower_bounds(logits):
    return pl.pallas_call(
        _lb_kernel,
        out_shape=jax.ShapeDtypeStruct(logits.shape, F32),
        name="hgrn_lower_bounds",
    )(logits.astype(F32))


FFN_TM = 1024
FFN_CHAINS = 2
FFN_COL_CHUNKS = ((0, 1024), (1024, 2048), (2048, FFN_HIDDEN))


def _ffn_kernel(x_ref, mod_ref, nw_ref, w13_ref, w2_ref, *rest, mod_off, final):
    if final:
        fnw_ref, o_ref = rest
    else:
        (o_ref,) = rest
    shift = mod_ref[mod_off:mod_off + 1, :]
    scale = mod_ref[mod_off + 1:mod_off + 2, :]
    gate = mod_ref[mod_off + 2:mod_off + 3, :]
    rows = x_ref.shape[0] // FFN_CHAINS
    for r in range(FFN_CHAINS):
        rs = slice(r * rows, (r + 1) * rows)
        x = jnp.concatenate(
            [x_ref[rs, j * D_MODEL:(j + 1) * D_MODEL] for j in range(POSITIONS)], axis=0)
        hb = _rms_modulate(x, nw_ref[...], scale, shift).astype(BF16)
        acc = None
        for c0, c1 in FFN_COL_CHUNKS:
            a = jnp.dot(hb, w13_ref[:, c0:c1], preferred_element_type=F32)
            b = jnp.dot(hb, w13_ref[:, FFN_HIDDEN + c0:FFN_HIDDEN + c1],
                        preferred_element_type=F32)
            act = (a * _sigmoid_t(a) * b).astype(BF16)
            part = jnp.dot(act, w2_ref[c0:c1, :], preferred_element_type=F32)
            acc = part if acc is None else acc + part
        y = x + (0.5 * (1.0 + gate)) * acc
        if final:
            ms = jnp.mean(y * y, axis=-1, keepdims=True)
            y = y * lax.rsqrt(ms + EPS) * fnw_ref[...]
        for j in range(POSITIONS):
            o_ref[rs, j * D_MODEL:(j + 1) * D_MODEL] = y[j * rows:(j + 1) * rows, :]


def _ffn(h16, mod, nw, w13, w2, *, mod_off, seq, final_w=None):
    n = h16.shape[0] * POSITIONS
    tm = FFN_TM
    tiles_per_seq = seq // tm
    final = final_w is not None
    tile16 = pl.BlockSpec((tm // POSITIONS, CHUNK_WIDTH), lambda i: (i, 0))
    in_specs = [
        tile16,
        pl.BlockSpec((None, N_ADA, D_MODEL), lambda i: (i // tiles_per_seq, 0, 0)),
        pl.BlockSpec((1, D_MODEL), lambda i: (0, 0)),
        pl.BlockSpec((D_MODEL, 2 * FFN_HIDDEN), lambda i: (0, 0),
                     pipeline_mode=pl.Buffered(1)),
        pl.BlockSpec((FFN_HIDDEN, D_MODEL), lambda i: (0, 0),
                     pipeline_mode=pl.Buffered(1)),
    ]
    args = [h16, mod, nw, w13, w2]
    if final:
        in_specs.append(pl.BlockSpec((1, D_MODEL), lambda i: (0, 0)))
        args.append(final_w)
    return pl.pallas_call(
        functools.partial(_ffn_kernel, mod_off=mod_off, final=final),
        grid=(n // tm,),
        in_specs=in_specs,
        out_specs=tile16,
        out_shape=jax.ShapeDtypeStruct(h16.shape, F32),
        compiler_params=pltpu.CompilerParams(
            dimension_semantics=("parallel",), vmem_limit_bytes=VMEM_LIMIT),
        name="ffn_half_step",
    )(*args)


M1_TM = 512
M1_PIECES = 2
M1_CHAINS = 2


def _mix_in_kernel(x_ref, mod_ref, nw_ref, win_ref, ws_ref, bs_ref, lb_ref,
                   yap_ref, q_ref, k_ref, f_ref, v_ref, sog_ref, sga_ref, sgb_ref,
                   gu_scr):
    tm = yap_ref.shape[0]
    nblk = tm // BLOCK // M1_CHAINS
    shift = mod_ref[3:4, :]
    scale = mod_ref[4:5, :]
    row = _block_token(lax.broadcasted_iota(jnp.int32, (BLOCK, BLOCK), 0))
    col = _block_token(lax.broadcasted_iota(jnp.int32, (BLOCK, BLOCK), 1))
    causal = row >= col
    lb = lb_ref[...]

    for chain in range(M1_CHAINS):
        b0 = chain * nblk
        rows = slice(b0 * BLOCK, (b0 + nblk) * BLOCK)
        x = _gather_block_rows(x_ref, b0, nblk)
        hb = _rms_modulate(x, nw_ref[...], scale, shift).astype(BF16)

        def project(p0, hb=hb):
            z = jnp.dot(hb, win_ref[:, p0 * D_MODEL:(p0 + M1_PIECES) * D_MODEL],
                        preferred_element_type=F32)
            return [z[:, i * D_MODEL:(i + 1) * D_MODEL] for i in range(M1_PIECES)]

        zu, zv = project(0)
        gu_scr[rows, :] = _gelu_tanh(zu)
        gv = _gelu_tanh(zv)
        for g in range(GMLP_GROUPS):
            ls = slice(g * HEAD_DIM, (g + 1) * HEAD_DIM)
            xg = gv[:, ls]
            mu = jnp.mean(xg, axis=-1, keepdims=True)
            xc = xg - mu
            var = jnp.mean(xc * xc, axis=-1, keepdims=True)
            vn = (xc * lax.rsqrt(var + EPS)).astype(BF16)
            rhs = jnp.concatenate(
                [vn[n * BLOCK:(n + 1) * BLOCK, :] for n in range(nblk)], axis=1)
            wc = jnp.where(causal, ws_ref[g], 0.0).astype(BF16)
            mixed = jnp.dot(wc, rhs, preferred_element_type=F32)
            bias = jnp.broadcast_to(bs_ref[:, g:g + 1], (BLOCK, BLOCK))
            for n in range(nblk):
                rs = slice((b0 + n) * BLOCK, (b0 + n + 1) * BLOCK)
                mx = mixed[:, n * BLOCK:(n + 1) * BLOCK] + bias
                yap_ref[rs, ls] = (gu_scr[rs, ls] * mx).astype(BF16)

        zq, zf = project(2)
        q_ref[rows, :] = (zq * _sigmoid_t(zq)).astype(BF16)
        th = 0.5 * jnp.tanh(0.5 * zf)
        f_ref[rows, :] = lb + (1.0 - lb) * (0.5 + th)
        k_ref[rows, :] = ((1.0 - lb) * (0.5 - th)).astype(BF16)

        zi, zog = project(4)
        v_ref[rows, :] = zi.astype(BF16)
        sog_ref[rows, :] = _sigmoid_t(zog).astype(BF16)
        zga, zgb = project(6)
        sga_ref[rows, :] = _sigmoid_t(zga).astype(BF16)
        sgb_ref[rows, :] = _sigmoid_t(zgb).astype(BF16)


def _mix_in(h16, mod, nw, w_in, ws_p, bs_p, lb, *, seq):
    n = h16.shape[0] * POSITIONS
    tm = M1_TM
    tiles_per_seq = seq // tm
    tile = pl.BlockSpec((tm, D_MODEL), lambda i: (i, 0))
    const2 = lambda i: (0, 0)
    f32_out = jax.ShapeDtypeStruct((n, D_MODEL), F32)
    bf_out = jax.ShapeDtypeStruct((n, D_MODEL), BF16)
    return pl.pallas_call(
        _mix_in_kernel,
        grid=(n // tm,),
        in_specs=[
            pl.BlockSpec((tm // POSITIONS, CHUNK_WIDTH), lambda i: (i, 0)),
            pl.BlockSpec((None, N_ADA, D_MODEL), lambda i: (i // tiles_per_seq, 0, 0)),
            pl.BlockSpec((1, D_MODEL), const2),
            pl.BlockSpec((D_MODEL, N_PIECES * D_MODEL), const2,
                         pipeline_mode=pl.Buffered(1)),
            pl.BlockSpec((GMLP_GROUPS, BLOCK, BLOCK), lambda i: (0, 0, 0)),
            pl.BlockSpec((BLOCK, GMLP_GROUPS), const2),
            pl.BlockSpec((1, D_MODEL), const2),
        ],
        out_specs=[tile] * 8,
        out_shape=[bf_out, bf_out, bf_out, f32_out, bf_out, bf_out, bf_out, bf_out],
        scratch_shapes=[pltpu.VMEM((tm, D_MODEL), F32)],
        compiler_params=pltpu.CompilerParams(
            dimension_semantics=("parallel",), vmem_limit_bytes=VMEM_LIMIT),
        name="mixer_in_proj",
    )(h16, mod, nw, w_in, ws_p, bs_p, lb)


HG_TM = 512
N_MASKS = 7


def _seg_scan_rows(x, rowid, width, reverse):
    d = 1
    y = x
    while d < width:
        if reverse:
            shifted = pltpu.roll(y, CHUNKS - d, axis=0)
            ok = (rowid & (width - 1)) < width - d
        else:
            shifted = pltpu.roll(y, d, axis=0)
            ok = (rowid & (width - 1)) >= d
        y = y * jnp.where(ok, shifted, 1.0)
        d *= 2
    return y


def _hgrn_kernel(q_ref, k_ref, f_ref, v_ref, o_ref, st_ref, mask_ref):
    n_blocks = q_ref.shape[0] // BLOCK

    @pl.when(pl.program_id(1) == 0)
    def _():
        st_ref[...] = jnp.zeros_like(st_ref)

    pr = lax.broadcasted_iota(jnp.int32, (BLOCK, BLOCK), 0)
    pc = lax.broadcasted_iota(jnp.int32, (BLOCK, BLOCK), 1)
    same_chunk = ((pr ^ pc) & (CHUNKS - 1)) == 0
    mask_ref[0] = jnp.where(pr == pc, 1.0, 0.0)
    for beta in range(4):
        sh = 3 + beta + 1
        mask_ref[1 + beta] = jnp.where(same_chunk & ((pr >> sh) == (pc >> sh)), 1.0, 0.0)
    for gamma in range(2):
        sh = gamma + 1
        mask_ref[5 + gamma] = jnp.where(
            ((pr & (CHUNKS - 1)) >> sh) == ((pc & (CHUNKS - 1)) >> sh), 1.0, 0.0)

    rowid = lax.broadcasted_iota(jnp.int32, (CHUNKS, HEAD_DIM), 0)
    zero = jnp.zeros((CHUNKS, HEAD_DIM), F32)
    row_contract = (((0,), (0,)), ((), ()))

    def stack(slabs):
        return jnp.concatenate(slabs, axis=0).astype(BF16)

    def score(q_slabs, k_slabs):
        kt = jnp.transpose(jnp.concatenate(k_slabs, axis=0)).astype(BF16)
        return jnp.dot(stack(q_slabs), kt, preferred_element_type=F32)

    def block_step(n, carry):
        r0 = pl.multiple_of(n * BLOCK, BLOCK)
        for h in range(N_HEADS):
            ls = slice(h * HEAD_DIM, (h + 1) * HEAD_DIM)

            def slabs(ref):
                val = ref[pl.ds(r0, BLOCK), ls].astype(F32)
                return [val[j * CHUNKS:(j + 1) * CHUNKS, :] for j in range(POSITIONS)]

            fs, qs, ks = slabs(f_ref), slabs(q_ref), slabs(k_ref)
            vb = v_ref[pl.ds(r0, BLOCK), ls]

            def mask_rows(level, j):
                return mask_ref[level, j * CHUNKS:(j + 1) * CHUNKS, :]

            p = [jnp.sum(qs[j] * ks[j], axis=-1, keepdims=True) * mask_rows(0, j)
                 for j in range(POSITIONS)]

            for beta in range(4):
                w = 1 << beta
                a = [None] * POSITIONS
                for j in range(POSITIONS):
                    if j & w:
                        a[j] = fs[j] if (j & (w - 1)) == 0 else a[j - 1] * fs[j]
                b = [None] * POSITIONS
                for j in range(POSITIONS - 1, -1, -1):
                    if not (j & w) and (j & (w - 1)) != w - 1:
                        b[j] = fs[j + 1] if b[j + 1] is None else b[j + 1] * fs[j + 1]
                upper = [j for j in range(POSITIONS) if j & w]
                kl = [zero if (j & w) else (ks[j] if b[j] is None else ks[j] * b[j])
                      for j in range(POSITIONS)]
                s = score([qs[j] * a[j] for j in upper], kl)
                for i, j in enumerate(upper):
                    p[j] = p[j] + s[i * CHUNKS:(i + 1) * CHUNKS, :] * mask_rows(1 + beta, j)

            gcum = [fs[0]]
            for j in range(1, POSITIONS):
                gcum.append(gcum[-1] * fs[j])
            hsuf = [None] * POSITIONS
            for j in range(POSITIONS - 2, -1, -1):
                hsuf[j] = fs[j + 1] if hsuf[j + 1] is None else hsuf[j + 1] * fs[j + 1]
            g = gcum[-1]
            qg = [qs[j] * gcum[j] for j in range(POSITIONS)]
            kh = [ks[j] if hsuf[j] is None else ks[j] * hsuf[j] for j in range(POSITIONS)]
            g_prev = pltpu.roll(g, 1, axis=0)
            g_next = pltpu.roll(g, CHUNKS - 1, axis=0)

            for gamma in range(3):
                w = 1 << gamma
                low = rowid & (w - 1)
                ca = _seg_scan_rows(jnp.where(low == 0, 1.0, g_prev), rowid, w, False)
                ca = jnp.where((rowid & w) != 0, ca, 0.0)
                cb = _seg_scan_rows(jnp.where(low == w - 1, 1.0, g_next), rowid, w, True)
                cb = jnp.where((rowid & w) == 0, cb, 0.0)
                s = score([x * ca for x in qg], [x * cb for x in kh])
                for j in range(POSITIONS):
                    sj = s[j * CHUNKS:(j + 1) * CHUNKS, :]
                    p[j] = p[j] + (sj * mask_rows(5 + gamma, j) if gamma < 2 else sj)

            cp = _seg_scan_rows(jnp.where(rowid == 0, 1.0, g_prev), rowid, CHUNKS, False)
            cs = _seg_scan_rows(jnp.where(rowid == CHUNKS - 1, 1.0, g_next), rowid, CHUNKS, True)
            total = (cp * g)[CHUNKS - 1:CHUNKS, :]
            st = st_ref[h]
            lhs = jnp.concatenate([stack(p), stack([x * cp for x in qg])], axis=1)
            rhs = jnp.concatenate([vb, st.astype(BF16)], axis=0)
            o_ref[pl.ds(r0, BLOCK), ls] = jnp.dot(lhs, rhs, preferred_element_type=F32)
            upd = lax.dot_general(stack([x * cs for x in kh]), vb, row_contract,
                                  preferred_element_type=F32)
            decay_col = jnp.transpose(jnp.broadcast_to(total, (HEAD_DIM, HEAD_DIM)))
            st_ref[h] = st * decay_col + upd
        return carry

    lax.fori_loop(0, n_blocks, block_step, 0)


def _hgrn(q, k, f, v, *, batch, seq):
    n = q.shape[0]
    tiles_per_seq = seq // HG_TM
    tile = pl.BlockSpec((HG_TM, D_MODEL), lambda b, t: (b * tiles_per_seq + t, 0))
    return pl.pallas_call(
        _hgrn_kernel,
        grid=(batch, tiles_per_seq),
        in_specs=[tile] * 4,
        out_specs=tile,
        out_shape=jax.ShapeDtypeStruct((n, D_MODEL), F32),
        scratch_shapes=[
            pltpu.VMEM((N_HEADS, HEAD_DIM, HEAD_DIM), F32),
            pltpu.VMEM((N_MASKS, BLOCK, BLOCK), F32),
        ],
        compiler_params=pltpu.CompilerParams(
            dimension_semantics=("parallel", "arbitrary"),
            vmem_limit_bytes=VMEM_LIMIT),
        name="hgrn2_recurrence",
    )(q, k, f, v)


M3_TM = 1024
M3_CHAINS = 2


def _mix_out_kernel(x_ref, mod_ref, yap_ref, o_ref, sog_ref, sga_ref, sgb_ref,
                    wa_ref, wb_ref, wo_ref, out_ref):
    nblk = yap_ref.shape[0] // BLOCK // M3_CHAINS
    gate = mod_ref[5:6, :]
    for chain in range(M3_CHAINS):
        b0 = chain * nblk
        rows = slice(b0 * BLOCK, (b0 + nblk) * BLOCK)
        o = o_ref[rows, :]
        parts = []
        for h in range(N_HEADS):
            oh = o[:, h * HEAD_DIM:(h + 1) * HEAD_DIM]
            ms = jnp.mean(oh * oh, axis=-1, keepdims=True)
            parts.append(oh * lax.rsqrt(ms + EPS))
        on = (jnp.concatenate(parts, axis=1) * sog_ref[rows, :].astype(F32)).astype(BF16)
        ya = jnp.dot(yap_ref[rows, :], wa_ref[...], preferred_element_type=F32)
        yb = jnp.dot(on, wb_ref[...], preferred_element_type=F32)
        m = (sga_ref[rows, :].astype(F32) * ya + sgb_ref[rows, :].astype(F32) * yb).astype(BF16)
        out = jnp.dot(m, wo_ref[...], preferred_element_type=F32)
        y = _gather_block_rows(x_ref, b0, nblk) + (1.0 + gate) * out
        _scatter_block_rows(out_ref, y, b0, nblk)


def _mix_out(h16, mod, yap, o, sog, sga, sgb, w_a, w_b, w_o, *, seq):
    n = h16.shape[0] * POSITIONS
    tm = M3_TM
    tiles_per_seq = seq // tm
    tile = pl.BlockSpec((tm, D_MODEL), lambda i: (i, 0))
    tile16 = pl.BlockSpec((tm // POSITIONS, CHUNK_WIDTH), lambda i: (i, 0))
    wspec = pl.BlockSpec((D_MODEL, D_MODEL), lambda i: (0, 0),
                         pipeline_mode=pl.Buffered(1))
    return pl.pallas_call(
        _mix_out_kernel,
        grid=(n // tm,),
        in_specs=[
            tile16,
            pl.BlockSpec((None, N_ADA, D_MODEL), lambda i: (i // tiles_per_seq, 0, 0)),
            tile, tile, tile, tile, tile,
            wspec, wspec, wspec,
        ],
        out_specs=tile16,
        out_shape=jax.ShapeDtypeStruct(h16.shape, F32),
        compiler_params=pltpu.CompilerParams(
            dimension_semantics=("parallel",), vmem_limit_bytes=VMEM_LIMIT),
        name="mixer_out_proj",
    )(h16, mod, yap, o, sog, sga, sgb, w_a, w_b, w_o)


def kernel(x, c, w_in, gmlp_ws, gmlp_b, hgrn_lb_logits, w_branch_a, w_branch_b, w_out,
           ffn_w13, ffn_w2, norm_w, ada_w, ada_b, final_norm_w):
    batch, seq, d = x.shape
    assert d == D_MODEL and seq % HG_TM == 0 and seq % FFN_TM == 0
    n = batch * seq

    mod_all = _ada_mod(c, ada_w, ada_b).reshape(DEPTH, batch, N_ADA, D_MODEL)
    lb_all = _lower_bounds(hgrn_lb_logits)
    perm = jnp.array([_block_token(p) for p in range(BLOCK)], dtype=jnp.int32)
    ws_p = gmlp_ws[:, :, perm, :][:, :, :, perm]
    bs_p = jnp.swapaxes(gmlp_b[:, :, perm], 1, 2)

    h16 = x.reshape(n // POSITIONS, CHUNK_WIDTH)
    for l in range(DEPTH):
        mod = mod_all[l]
        nw = norm_w[l].reshape(3, 1, D_MODEL)
        h16 = _ffn(h16, mod, nw[0], ffn_w13[l, 0].astype(BF16), ffn_w2[l, 0].astype(BF16),
                   mod_off=0, seq=seq)
        yap, q, k, f, v, sog, sga, sgb = _mix_in(
            h16, mod, nw[1], w_in[l].astype(BF16), ws_p[l], bs_p[l],
            lb_all[l].reshape(1, D_MODEL), seq=seq)
        o = _hgrn(q, k, f, v, batch=batch, seq=seq)
        h16 = _mix_out(h16, mod, yap, o, sog, sga, sgb, w_branch_a[l].astype(BF16),
                       w_branch_b[l].astype(BF16), w_out[l].astype(BF16), seq=seq)
        final_w = final_norm_w.reshape(1, D_MODEL) if l == DEPTH - 1 else None
        h16 = _ffn(h16, mod, nw[2], ffn_w13[l, 1].astype(BF16), ffn_w2[l, 1].astype(BF16),
                   mod_off=6, seq=seq, final_w=final_w)
    return h16.reshape(batch, seq, D_MODEL)
```

```python
import functools

import jax
import jax.numpy as jnp
from jax import lax
from jax.experimental import pallas as pl
from jax.experimental.pallas import tpu as pltpu

F32 = jnp.float32
BF16 = jnp.bfloat16

D_MODEL = 1024
DEPTH = 4
FFN_HIDDEN = 2816
N_ADA = 9
EPS = 1e-6
GMLP_GROUPS = 8
HEAD_DIM = 128
N_HEADS = D_MODEL // HEAD_DIM
N_PIECES = 8

BLOCK = 128
POSITIONS = 16
CHUNKS = BLOCK // POSITIONS
CHUNK_WIDTH = POSITIONS * D_MODEL

VMEM_LIMIT = 56 * 1024 * 1024


def _sigmoid(x):
    return 1.0 / (1.0 + jnp.exp(-x))


def _sigmoid_t(x):
    return 0.5 * jnp.tanh(0.5 * x) + 0.5


def _gelu_tanh(x):
    c = 0.7978845608028654
    return 0.5 * x * (1.0 + jnp.tanh(c * (x + 0.044715 * (x * x * x))))


def _rms_modulate(x, nw, scale, shift):
    ms = jnp.mean(x * x, axis=-1, keepdims=True)
    return (x * lax.rsqrt(ms + EPS) * nw) * (1.0 + scale) + shift


def _block_token(p):
    return (p % CHUNKS) * POSITIONS + p // CHUNKS


def _gather_block_rows(x_ref, b0, n_blocks):
    pieces = [x_ref[n * CHUNKS:(n + 1) * CHUNKS, j * D_MODEL:(j + 1) * D_MODEL]
              for n in range(b0, b0 + n_blocks) for j in range(POSITIONS)]
    return jnp.concatenate(pieces, axis=0)


def _scatter_block_rows(out_ref, y, b0, n_blocks):
    for n in range(n_blocks):
        for j in range(POSITIONS):
            r = (n * POSITIONS + j) * CHUNKS
            out_ref[(b0 + n) * CHUNKS:(b0 + n + 1) * CHUNKS,
                    j * D_MODEL:(j + 1) * D_MODEL] = y[r:r + CHUNKS, :]


def _ada_kernel(c_ref, w_ref, b_ref, o_ref):
    c = c_ref[...]
    ca = c * _sigmoid(c)
    o_ref[...] = jnp.dot(ca, w_ref[...], preferred_element_type=F32,
                         precision=lax.Precision.HIGHEST) + b_ref[...]


def _ada_mod(c, ada_w, ada_b):
    b = c.shape[0]
    return pl.pallas_call(
        _ada_kernel,
        grid=(DEPTH, N_ADA),
        in_specs=[
            pl.BlockSpec((b, D_MODEL), lambda l, j: (0, 0)),
            pl.BlockSpec((None, D_MODEL, D_MODEL), lambda l, j: (l, 0, j)),
            pl.BlockSpec((None, 1, D_MODEL), lambda l, j: (l, 0, j)),
        ],
        out_specs=pl.BlockSpec((None, b, D_MODEL), lambda l, j: (l, 0, j)),
        out_shape=jax.ShapeDtypeStruct((DEPTH, b, N_ADA * D_MODEL), F32),
        name="ada_mod",
    )(c, ada_w, ada_b.reshape(DEPTH, 1, N_ADA * D_MODEL))


def _lb_kernel(x_ref, o_ref):
    x = x_ref[...]
    rows = [x[i:i + 1, :] for i in range(DEPTH)]
    m = rows[0]
    for r in rows[1:]:
        m = jnp.maximum(m, r)
    e = [jnp.exp(r - m) for r in rows]
    tot = e[0]
    for t in e[1:]:
        tot = tot + t
    p = [t / tot for t in e]
    cum = p[0]
    first = cum
    for i in range(DEPTH):
        if i > 0:
            cum = cum + p[i]
        o_ref[i:i + 1, :] = cum - first


def _lower_bounds(logits):
    return pl.pallas_call(
        _lb_kernel,
        out_shape=jax.ShapeDtypeStruct(logits.shape, F32),
        name="hgrn_lower_bounds",
    )(logits.astype(F32))


FFN_TM = 1024
FFN_CHAINS = 2
FFN_COL_CHUNKS = ((0, 1024), (1024, 2048), (2048, FFN_HIDDEN))


def _ffn_kernel(x_ref, mod_ref, nw_ref, w13_ref, w2_ref, *rest, mod_off, final):
    if final:
        fnw_ref, o_ref = rest
    else:
        (o_ref,) = rest
    shift = mod_ref[mod_off:mod_off + 1, :]
    scale = mod_ref[mod_off + 1:mod_off + 2, :]
    gate = mod_ref[mod_off + 2:mod_off + 3, :]
    rows = x_ref.shape[0] // FFN_CHAINS
    for r in range(FFN_CHAINS):
        rs = slice(r * rows, (r + 1) * rows)
        x = jnp.concatenate(
            [x_ref[rs, j * D_MODEL:(j + 1) * D_MODEL] for j in range(POSITIONS)], axis=0)
        hb = _rms_modulate(x, nw_ref[...], scale, shift).astype(BF16)
        acc = None
        for c0, c1 in FFN_COL_CHUNKS:
            a = jnp.dot(hb, w13_ref[:, c0:c1], preferred_element_type=F32)
            b = jnp.dot(hb, w13_ref[:, FFN_HIDDEN + c0:FFN_HIDDEN + c1],
                        preferred_element_type=F32)
            act = (a * _sigmoid_t(a) * b).astype(BF16)
            part = jnp.dot(act, w2_ref[c0:c1, :], preferred_element_type=F32)
            acc = part if acc is None else acc + part
        y = x + (0.5 * (1.0 + gate)) * acc
        if final:
            ms = jnp.mean(y * y, axis=-1, keepdims=True)
            y = y * lax.rsqrt(ms + EPS) * fnw_ref[...]
        for j in range(POSITIONS):
            o_ref[rs, j * D_MODEL:(j + 1) * D_MODEL] = y[j * rows:(j + 1) * rows, :]


def _ffn(h16, mod, nw, w13, w2, *, mod_off, seq, final_w=None):
    n = h16.shape[0] * POSITIONS
    tm = FFN_TM
    tiles_per_seq = seq // tm
    final = final_w is not None
    tile16 = pl.BlockSpec((tm // POSITIONS, CHUNK_WIDTH), lambda i: (i, 0))
    in_specs = [
        tile16,
        pl.BlockSpec((None, N_ADA, D_MODEL), lambda i: (i // tiles_per_seq, 0, 0)),
        pl.BlockSpec((1, D_MODEL), lambda i: (0, 0)),
        pl.BlockSpec((D_MODEL, 2 * FFN_HIDDEN), lambda i: (0, 0),
                     pipeline_mode=pl.Buffered(1)),
        pl.BlockSpec((FFN_HIDDEN, D_MODEL), lambda i: (0, 0),
                     pipeline_mode=pl.Buffered(1)),
    ]
    args = [h16, mod, nw, w13, w2]
    if final:
        in_specs.append(pl.BlockSpec((1, D_MODEL), lambda i: (0, 0)))
        args.append(final_w)
    return pl.pallas_call(
        functools.partial(_ffn_kernel, mod_off=mod_off, final=final),
        grid=(n // tm,),
        in_specs=in_specs,
        out_specs=tile16,
        out_shape=jax.ShapeDtypeStruct(h16.shape, F32),
        compiler_params=pltpu.CompilerParams(
            dimension_semantics=("parallel",), vmem_limit_bytes=VMEM_LIMIT),
        name="ffn_half_step",
    )(*args)


M1_TM = 512
M1_PIECES = 2
M1_CHAINS = 2


def _mix_in_kernel(x_ref, mod_ref, nw_ref, win_ref, ws_ref, bs_ref, lb_ref,
                   yap_ref, q_ref, k_ref, f_ref, v_ref, sog_ref, sga_ref, sgb_ref,
                   gu_scr):
    tm = yap_ref.shape[0]
    nblk = tm // BLOCK // M1_CHAINS
    shift = mod_ref[3:4, :]
    scale = mod_ref[4:5, :]
    row = _block_token(lax.broadcasted_iota(jnp.int32, (BLOCK, BLOCK), 0))
    col = _block_token(lax.broadcasted_iota(jnp.int32, (BLOCK, BLOCK), 1))
    causal = row >= col
    lb = lb_ref[...]

    for chain in range(M1_CHAINS):
        b0 = chain * nblk
        rows = slice(b0 * BLOCK, (b0 + nblk) * BLOCK)
        x = _gather_block_rows(x_ref, b0, nblk)
        hb = _rms_modulate(x, nw_ref[...], scale, shift).astype(BF16)

        def project(p0, hb=hb):
            z = jnp.dot(hb, win_ref[:, p0 * D_MODEL:(p0 + M1_PIECES) * D_MODEL],
                        preferred_element_type=F32)
            return [z[:, i * D_MODEL:(i + 1) * D_MODEL] for i in range(M1_PIECES)]

        zu, zv = project(0)
        gu_scr[rows, :] = _gelu_tanh(zu)
        gv = _gelu_tanh(zv)
        for g in range(GMLP_GROUPS):
            ls = slice(g * HEAD_DIM, (g + 1) * HEAD_DIM)
            xg = gv[:, ls]
            mu = jnp.mean(xg, axis=-1, keepdims=True)
            xc = xg - mu
            var = jnp.mean(xc * xc, axis=-1, keepdims=True)
            vn = (xc * lax.rsqrt(var + EPS)).astype(BF16)
            rhs = jnp.concatenate(
                [vn[n * BLOCK:(n + 1) * BLOCK, :] for n in range(nblk)], axis=1)
            wc = jnp.where(causal, ws_ref[g], 0.0).astype(BF16)
            mixed = jnp.dot(wc, rhs, preferred_element_type=F32)
            bias = jnp.broadcast_to(bs_ref[:, g:g + 1], (BLOCK, BLOCK))
            for n in range(nblk):
                rs = slice((b0 + n) * BLOCK, (b0 + n + 1) * BLOCK)
                mx = mixed[:, n * BLOCK:(n + 1) * BLOCK] + bias
                yap_ref[rs, ls] = (gu_scr[rs, ls] * mx).astype(BF16)

        zq, zf = project(2)
        q_ref[rows, :] = (zq * _sigmoid_t(zq)).astype(BF16)
        th = 0.5 * jnp.tanh(0.5 * zf)
        f_ref[rows, :] = lb + (1.0 - lb) * (0.5 + th)
        k_ref[rows, :] = ((1.0 - lb) * (0.5 - th)).astype(BF16)

        zi, zog = project(4)
        v_ref[rows, :] = zi.astype(BF16)
        sog_ref[rows, :] = _sigmoid_t(zog).astype(BF16)
        zga, zgb = project(6)
        sga_ref[rows, :] = _sigmoid_t(zga).astype(BF16)
        sgb_ref[rows, :] = _sigmoid_t(zgb).astype(BF16)


def _mix_in(h16, mod, nw, w_in, ws_p, bs_p, lb, *, seq):
    n = h16.shape[0] * POSITIONS
    tm = M1_TM
    tiles_per_seq = seq // tm
    tile = pl.BlockSpec((tm, D_MODEL), lambda i: (i, 0))
    const2 = lambda i: (0, 0)
    f32_out = jax.ShapeDtypeStruct((n, D_MODEL), F32)
    bf_out = jax.ShapeDtypeStruct((n, D_MODEL), BF16)
    return pl.pallas_call(
        _mix_in_kernel,
        grid=(n // tm,),
        in_specs=[
            pl.BlockSpec((tm // POSITIONS, CHUNK_WIDTH), lambda i: (i, 0)),
            pl.BlockSpec((None, N_ADA, D_MODEL), lambda i: (i // tiles_per_seq, 0, 0)),
            pl.BlockSpec((1, D_MODEL), const2),
            pl.BlockSpec((D_MODEL, N_PIECES * D_MODEL), const2,
                         pipeline_mode=pl.Buffered(1)),
            pl.BlockSpec((GMLP_GROUPS, BLOCK, BLOCK), lambda i: (0, 0, 0)),
            pl.BlockSpec((BLOCK, GMLP_GROUPS), const2),
            pl.BlockSpec((1, D_MODEL), const2),
        ],
        out_specs=[tile] * 8,
        out_shape=[bf_out, bf_out, bf_out, f32_out, bf_out, bf_out, bf_out, bf_out],
        scratch_shapes=[pltpu.VMEM((tm, D_MODEL), F32)],
        compiler_params=pltpu.CompilerParams(
            dimension_semantics=("parallel",), vmem_limit_bytes=VMEM_LIMIT),
        name="mixer_in_proj",
    )(h16, mod, nw, w_in, ws_p, bs_p, lb)


HG_TM = 1024
N_MASKS = 7


def _seg_scan_rows(x, rowid, width, reverse):
    d = 1
    y = x
    while d < width:
        if reverse:
            shifted = pltpu.roll(y, CHUNKS - d, axis=0)
            ok = (rowid & (width - 1)) < width - d
        else:
            shifted = pltpu.roll(y, d, axis=0)
            ok = (rowid & (width - 1)) >= d
        y = y * jnp.where(ok, shifted, 1.0)
        d *= 2
    return y


def _hgrn_kernel(q_ref, k_ref, f_ref, v_ref, o_ref, st_ref, mask_ref):
    n_blocks = q_ref.shape[0] // BLOCK

    @pl.when(pl.program_id(1) == 0)
    def _():
        st_ref[...] = jnp.zeros_like(st_ref)

    pr = lax.broadcasted_iota(jnp.int32, (BLOCK, BLOCK), 0)
    pc = lax.broadcasted_iota(jnp.int32, (BLOCK, BLOCK), 1)
    same_chunk = ((pr ^ pc) & (CHUNKS - 1)) == 0
    mask_ref[0] = jnp.where(pr == pc, 1.0, 0.0)
    for beta in range(4):
        sh = 3 + beta + 1
        mask_ref[1 + beta] = jnp.where(same_chunk & ((pr >> sh) == (pc >> sh)), 1.0, 0.0)
    for gamma in range(2):
        sh = gamma + 1
        mask_ref[5 + gamma] = jnp.where(
            ((pr & (CHUNKS - 1)) >> sh) == ((pc & (CHUNKS - 1)) >> sh), 1.0, 0.0)

    rowid = lax.broadcasted_iota(jnp.int32, (CHUNKS, HEAD_DIM), 0)
    zero = jnp.zeros((CHUNKS, HEAD_DIM), F32)
    row_contract = (((0,), (0,)), ((), ()))

    def stack(slabs):
        return jnp.concatenate(slabs, axis=0).astype(BF16)

    def score(q_slabs, k_slabs):
        kt = jnp.transpose(jnp.concatenate(k_slabs, axis=0)).astype(BF16)
        return jnp.dot(stack(q_slabs), kt, preferred_element_type=F32)

    def block_step(n, carry):
        r0 = pl.multiple_of(n * BLOCK, BLOCK)
        for h in range(N_HEADS):
            ls = slice(h * HEAD_DIM, (h + 1) * HEAD_DIM)

            def slabs(ref):
                val = ref[pl.ds(r0, BLOCK), ls].astype(F32)
                return [val[j * CHUNKS:(j + 1) * CHUNKS, :] for j in range(POSITIONS)]

            fs, qs, ks = slabs(f_ref), slabs(q_ref), slabs(k_ref)
            vb = v_ref[pl.ds(r0, BLOCK), ls]

            def mask_rows(level, j):
                return mask_ref[level, j * CHUNKS:(j + 1) * CHUNKS, :]

            p = [jnp.sum(qs[j] * ks[j], axis=-1, keepdims=True) * mask_rows(0, j)
                 for j in range(POSITIONS)]

            for beta in range(4):
                w = 1 << beta
                a = [None] * POSITIONS
                for j in range(POSITIONS):
                    if j & w:
                        a[j] = fs[j] if (j & (w - 1)) == 0 else a[j - 1] * fs[j]
                b = [None] * POSITIONS
                for j in range(POSITIONS - 1, -1, -1):
                    if not (j & w) and (j & (w - 1)) != w - 1:
                        b[j] = fs[j + 1] if b[j + 1] is None else b[j + 1] * fs[j + 1]
                upper = [j for j in range(POSITIONS) if j & w]
                kl = [zero if (j & w) else (ks[j] if b[j] is None else ks[j] * b[j])
                      for j in range(POSITIONS)]
                s = score([qs[j] * a[j] for j in upper], kl)
                for i, j in enumerate(upper):
                    p[j] = p[j] + s[i * CHUNKS:(i + 1) * CHUNKS, :] * mask_rows(1 + beta, j)

            gcum = [fs[0]]
            for j in range(1, POSITIONS):
                gcum.append(gcum[-1] * fs[j])
            hsuf = [None] * POSITIONS
            for j in range(POSITIONS - 2, -1, -1):
                hsuf[j] = fs[j + 1] if hsuf[j + 1] is None else hsuf[j + 1] * fs[j + 1]
            g = gcum[-1]
            qg = [qs[j] * gcum[j] for j in range(POSITIONS)]
            kh = [ks[j] if hsuf[j] is None else ks[j] * hsuf[j] for j in range(POSITIONS)]
            g_prev = pltpu.roll(g, 1, axis=0)
            g_next = pltpu.roll(g, CHUNKS - 1, axis=0)

            for gamma in range(3):
                w = 1 << gamma
                low = rowid & (w - 1)
                ca = _seg_scan_rows(jnp.where(low == 0, 1.0, g_prev), rowid, w, False)
                ca = jnp.where((rowid & w) != 0, ca, 0.0)
                cb = _seg_scan_rows(jnp.where(low == w - 1, 1.0, g_next), rowid, w, True)
                cb = jnp.where((rowid & w) == 0, cb, 0.0)
                s = score([x * ca for x in qg], [x * cb for x in kh])
                for j in range(POSITIONS):
                    sj = s[j * CHUNKS:(j + 1) * CHUNKS, :]
                    p[j] = p[j] + (sj * mask_rows(5 + gamma, j) if gamma < 2 else sj)

            cp = _seg_scan_rows(jnp.where(rowid == 0, 1.0, g_prev), rowid, CHUNKS, False)
            cs = _seg_scan_rows(jnp.where(rowid == CHUNKS - 1, 1.0, g_next), rowid, CHUNKS, True)
            total = (cp * g)[CHUNKS - 1:CHUNKS, :]
            st = st_ref[h]
            lhs = jnp.concatenate([stack(p), stack([x * cp for x in qg])], axis=1)
            rhs = jnp.concatenate([vb, st.astype(BF16)], axis=0)
            o_ref[pl.ds(r0, BLOCK), ls] = jnp.dot(lhs, rhs, preferred_element_type=F32)
            upd = lax.dot_general(stack([x * cs for x in kh]), vb, row_contract,
                                  preferred_element_type=F32)
            decay_col = jnp.transpose(jnp.broadcast_to(total, (HEAD_DIM, HEAD_DIM)))
            st_ref[h] = st * decay_col + upd
        return carry

    lax.fori_loop(0, n_blocks, block_step, 0)


def _hgrn(q, k, f, v, *, batch, seq):
    n = q.shape[0]
    tiles_per_seq = seq // HG_TM
    tile = pl.BlockSpec((HG_TM, D_MODEL), lambda b, t: (b * tiles_per_seq + t, 0))
    return pl.pallas_call(
        _hgrn_kernel,
        grid=(batch, tiles_per_seq),
        in_specs=[tile] * 4,
        out_specs=tile,
        out_shape=jax.ShapeDtypeStruct((n, D_MODEL), F32),
        scratch_shapes=[
            pltpu.VMEM((N_HEADS, HEAD_DIM, HEAD_DIM), F32),
            pltpu.VMEM((N_MASKS, BLOCK, BLOCK), F32),
        ],
        compiler_params=pltpu.CompilerParams(
            dimension_semantics=("parallel", "arbitrary"),
            vmem_limit_bytes=VMEM_LIMIT),
        name="hgrn2_recurrence",
    )(q, k, f, v)


M3_TM = 1024
M3_CHAINS = 2


def _mix_out_kernel(x_ref, mod_ref, yap_ref, o_ref, sog_ref, sga_ref, sgb_ref,
                    wa_ref, wb_ref, wo_ref, out_ref):
    nblk = yap_ref.shape[0] // BLOCK // M3_CHAINS
    gate = mod_ref[5:6, :]
    for chain in range(M3_CHAINS):
        b0 = chain * nblk
        rows = slice(b0 * BLOCK, (b0 + nblk) * BLOCK)
        o = o_ref[rows, :]
        parts = []
        for h in range(N_HEADS):
            oh = o[:, h * HEAD_DIM:(h + 1) * HEAD_DIM]
            ms = jnp.mean(oh * oh, axis=-1, keepdims=True)
            parts.append(oh * lax.rsqrt(ms + EPS))
        on = (jnp.concatenate(parts, axis=1) * sog_ref[rows, :].astype(F32)).astype(BF16)
        ya = jnp.dot(yap_ref[rows, :], wa_ref[...], preferred_element_type=F32)
        yb = jnp.dot(on, wb_ref[...], preferred_element_type=F32)
        m = (sga_ref[rows, :].astype(F32) * ya + sgb_ref[rows, :].astype(F32) * yb).astype(BF16)
        out = jnp.dot(m, wo_ref[...], preferred_element_type=F32)
        y = _gather_block_rows(x_ref, b0, nblk) + (1.0 + gate) * out
        _scatter_block_rows(out_ref, y, b0, nblk)


def _mix_out(h16, mod, yap, o, sog, sga, sgb, w_a, w_b, w_o, *, seq):
    n = h16.shape[0] * POSITIONS
    tm = M3_TM
    tiles_per_seq = seq // tm
    tile = pl.BlockSpec((tm, D_MODEL), lambda i: (i, 0))
    tile16 = pl.BlockSpec((tm // POSITIONS, CHUNK_WIDTH), lambda i: (i, 0))
    wspec = pl.BlockSpec((D_MODEL, D_MODEL), lambda i: (0, 0),
                         pipeline_mode=pl.Buffered(1))
    return pl.pallas_call(
        _mix_out_kernel,
        grid=(n // tm,),
        in_specs=[
            tile16,
            pl.BlockSpec((None, N_ADA, D_MODEL), lambda i: (i // tiles_per_seq, 0, 0)),
            tile, tile, tile, tile, tile,
            wspec, wspec, wspec,
        ],
        out_specs=tile16,
        out_shape=jax.ShapeDtypeStruct(h16.shape, F32),
        compiler_params=pltpu.CompilerParams(
            dimension_semantics=("parallel",), vmem_limit_bytes=VMEM_LIMIT),
        name="mixer_out_proj",
    )(h16, mod, yap, o, sog, sga, sgb, w_a, w_b, w_o)


def kernel(x, c, w_in, gmlp_ws, gmlp_b, hgrn_lb_logits, w_branch_a, w_branch_b, w_out,
           ffn_w13, ffn_w2, norm_w, ada_w, ada_b, final_norm_w):
    batch, seq, d = x.shape
    assert d == D_MODEL and seq % HG_TM == 0 and seq % FFN_TM == 0
    n = batch * seq

    mod_all = _ada_mod(c, ada_w, ada_b).reshape(DEPTH, batch, N_ADA, D_MODEL)
    lb_all = _lower_bounds(hgrn_lb_logits)
    perm = jnp.array([_block_token(p) for p in range(BLOCK)], dtype=jnp.int32)
    ws_p = gmlp_ws[:, :, perm, :][:, :, :, perm]
    bs_p = jnp.swapaxes(gmlp_b[:, :, perm], 1, 2)

    h16 = x.reshape(n // POSITIONS, CHUNK_WIDTH)
    for l in range(DEPTH):
        mod = mod_all[l]
        nw = norm_w[l].reshape(3, 1, D_MODEL)
        h16 = _ffn(h16, mod, nw[0], ffn_w13[l, 0].astype(BF16), ffn_w2[l, 0].astype(BF16),
                   mod_off=0, seq=seq)
        yap, q, k, f, v, sog, sga, sgb = _mix_in(
            h16, mod, nw[1], w_in[l].astype(BF16), ws_p[l], bs_p[l],
            lb_all[l].reshape(1, D_MODEL), seq=seq)
        o = _hgrn(q, k, f, v, batch=batch, seq=seq)
        h16 = _mix_out(h16, mod, yap, o, sog, sga, sgb, w_branch_a[l].astype(BF16),
                       w_branch_b[l].astype(BF16), w_out[l].astype(BF16), seq=seq)
        final_w = final_norm_w.reshape(1, D_MODEL) if l == DEPTH - 1 else None
        h16 = _ffn(h16, mod, nw[2], ffn_w13[l, 1].astype(BF16), ffn_w2[l, 1].astype(BF16),
                   mod_off=6, seq=seq, final_w=final_w)
    return h16.reshape(batch, seq, D_MODEL)
```

```python
import functools

import jax
import jax.numpy as jnp
from jax import lax
from jax.experimental import pallas as pl
from jax.experimental.pallas import tpu as pltpu

F32 = jnp.float32
BF16 = jnp.bfloat16

D_MODEL = 1024
DEPTH = 4
FFN_HIDDEN = 2816
N_ADA = 9
EPS = 1e-6
GMLP_GROUPS = 8
HEAD_DIM = 128
N_HEADS = D_MODEL // HEAD_DIM
N_PIECES = 8

BLOCK = 128
POSITIONS = 16
CHUNKS = BLOCK // POSITIONS
CHUNK_WIDTH = POSITIONS * D_MODEL

VMEM_LIMIT = 56 * 1024 * 1024


def _sigmoid(x):
    return 1.0 / (1.0 + jnp.exp(-x))


def _sigmoid_t(x):
    return 0.5 * jnp.tanh(0.5 * x) + 0.5


def _gelu_tanh(x):
    c = 0.7978845608028654
    return 0.5 * x * (1.0 + jnp.tanh(c * (x + 0.044715 * (x * x * x))))


def _rms_modulate(x, nw, scale, shift):
    ms = jnp.mean(x * x, axis=-1, keepdims=True)
    return (x * lax.rsqrt(ms + EPS) * nw) * (1.0 + scale) + shift


def _block_token(p):
    return (p % CHUNKS) * POSITIONS + p // CHUNKS


def _gather_block_rows(x_ref, b0, n_blocks):
    pieces = [x_ref[n * CHUNKS:(n + 1) * CHUNKS, j * D_MODEL:(j + 1) * D_MODEL]
              for n in range(b0, b0 + n_blocks) for j in range(POSITIONS)]
    return jnp.concatenate(pieces, axis=0)


def _scatter_block_rows(out_ref, y, b0, n_blocks):
    for n in range(n_blocks):
        for j in range(POSITIONS):
            r = (n * POSITIONS + j) * CHUNKS
            out_ref[(b0 + n) * CHUNKS:(b0 + n + 1) * CHUNKS,
                    j * D_MODEL:(j + 1) * D_MODEL] = y[r:r + CHUNKS, :]


def _ada_kernel(c_ref, w_ref, b_ref, o_ref):
    c = c_ref[...]
    ca = c * _sigmoid(c)
    o_ref[...] = jnp.dot(ca, w_ref[...], preferred_element_type=F32,
                         precision=lax.Precision.HIGHEST) + b_ref[...]


def _ada_mod(c, ada_w, ada_b):
    b = c.shape[0]
    return pl.pallas_call(
        _ada_kernel,
        grid=(DEPTH, N_ADA),
        in_specs=[
            pl.BlockSpec((b, D_MODEL), lambda l, j: (0, 0)),
            pl.BlockSpec((None, D_MODEL, D_MODEL), lambda l, j: (l, 0, j)),
            pl.BlockSpec((None, 1, D_MODEL), lambda l, j: (l, 0, j)),
        ],
        out_specs=pl.BlockSpec((None, b, D_MODEL), lambda l, j: (l, 0, j)),
        out_shape=jax.ShapeDtypeStruct((DEPTH, b, N_ADA * D_MODEL), F32),
        name="ada_mod",
    )(c, ada_w, ada_b.reshape(DEPTH, 1, N_ADA * D_MODEL))


def _lb_kernel(x_ref, o_ref):
    x = x_ref[...]
    rows = [x[i:i + 1, :] for i in range(DEPTH)]
    m = rows[0]
    for r in rows[1:]:
        m = jnp.maximum(m, r)
    e = [jnp.exp(r - m) for r in rows]
    tot = e[0]
    for t in e[1:]:
        tot = tot + t
    p = [t / tot for t in e]
    cum = p[0]
    first = cum
    for i in range(DEPTH):
        if i > 0:
            cum = cum + p[i]
        o_ref[i:i + 1, :] = cum - first


def _lower_bounds(logits):
    return pl.pallas_call(
        _lb_kernel,
        out_shape=jax.ShapeDtypeStruct(logits.shape, F32),
        name="hgrn_lower_bounds",
    )(logits.astype(F32))


FFN_TM = 1024
FFN_CHAINS = 2
FFN_COL_CHUNKS = ((0, 1024), (1024, 2048), (2048, FFN_HIDDEN))


def _ffn_kernel(x_ref, mod_ref, nw_ref, w13_ref, w2_ref, *rest, mod_off, final):
    if final:
        fnw_ref, o_ref = rest
    else:
        (o_ref,) = rest
    shift = mod_ref[mod_off:mod_off + 1, :]
    scale = mod_ref[mod_off + 1:mod_off + 2, :]
    gate = mod_ref[mod_off + 2:mod_off + 3, :]
    rows = x_ref.shape[0] // FFN_CHAINS
    for r in range(FFN_CHAINS):
        rs = slice(r * rows, (r + 1) * rows)
        x = jnp.concatenate(
            [x_ref[rs, j * D_MODEL:(j + 1) * D_MODEL] for j in range(POSITIONS)], axis=0)
        hb = _rms_modulate(x, nw_ref[...], scale, shift).astype(BF16)
        acc = None
        for c0, c1 in FFN_COL_CHUNKS:
            a = jnp.dot(hb, w13_ref[:, c0:c1], preferred_element_type=F32)
            b = jnp.dot(hb, w13_ref[:, FFN_HIDDEN + c0:FFN_HIDDEN + c1],
                        preferred_element_type=F32)
            act = (a * _sigmoid_t(a) * b).astype(BF16)
            part = jnp.dot(act, w2_ref[c0:c1, :], preferred_element_type=F32)
            acc = part if acc is None else acc + part
        y = x + (0.5 * (1.0 + gate)) * acc
        if final:
            ms = jnp.mean(y * y, axis=-1, keepdims=True)
            y = y * lax.rsqrt(ms + EPS) * fnw_ref[...]
        for j in range(POSITIONS):
            o_ref[rs, j * D_MODEL:(j + 1) * D_MODEL] = y[j * rows:(j + 1) * rows, :]


def _ffn(h16, mod, nw, w13, w2, *, mod_off, seq, final_w=None):
    n = h16.shape[0] * POSITIONS
    tm = FFN_TM
    tiles_per_seq = seq // tm
    final = final_w is not None
    tile16 = pl.BlockSpec((tm // POSITIONS, CHUNK_WIDTH), lambda i: (i, 0))
    in_specs = [
        tile16,
        pl.BlockSpec((None, N_ADA, D_MODEL), lambda i: (i // tiles_per_seq, 0, 0)),
        pl.BlockSpec((1, D_MODEL), lambda i: (0, 0)),
        pl.BlockSpec((D_MODEL, 2 * FFN_HIDDEN), lambda i: (0, 0),
                     pipeline_mode=pl.Buffered(1)),
        pl.BlockSpec((FFN_HIDDEN, D_MODEL), lambda i: (0, 0),
                     pipeline_mode=pl.Buffered(1)),
    ]
    args = [h16, mod, nw, w13, w2]
    if final:
        in_specs.append(pl.BlockSpec((1, D_MODEL), lambda i: (0, 0)))
        args.append(final_w)
    return pl.pallas_call(
        functools.partial(_ffn_kernel, mod_off=mod_off, final=final),
        grid=(n // tm,),
        in_specs=in_specs,
        out_specs=tile16,
        out_shape=jax.ShapeDtypeStruct(h16.shape, F32),
        compiler_params=pltpu.CompilerParams(
            dimension_semantics=("parallel",), vmem_limit_bytes=VMEM_LIMIT),
        name="ffn_half_step",
    )(*args)


M1_TM = 512
M1_PIECES = 2
M1_CHAINS = 2


def _mix_in_kernel(x_ref, mod_ref, nw_ref, win_ref, ws_ref, bs_ref, lb_ref,
                   yap_ref, q_ref, k_ref, f_ref, v_ref, sog_ref, sga_ref, sgb_ref,
                   gu_scr):
    tm = yap_ref.shape[0]
    nblk = tm // BLOCK // M1_CHAINS
    shift = mod_ref[3:4, :]
    scale = mod_ref[4:5, :]
    row = _block_token(lax.broadcasted_iota(jnp.int32, (BLOCK, BLOCK), 0))
    col = _block_token(lax.broadcasted_iota(jnp.int32, (BLOCK, BLOCK), 1))
    causal = row >= col
    lb = lb_ref[...]

    for chain in range(M1_CHAINS):
        b0 = chain * nblk
        rows = slice(b0 * BLOCK, (b0 + nblk) * BLOCK)
        x = _gather_block_rows(x_ref, b0, nblk)
        hb = _rms_modulate(x, nw_ref[...], scale, shift).astype(BF16)

        def project(p0, hb=hb):
            z = jnp.dot(hb, win_ref[:, p0 * D_MODEL:(p0 + M1_PIECES) * D_MODEL],
                        preferred_element_type=F32)
            return [z[:, i * D_MODEL:(i + 1) * D_MODEL] for i in range(M1_PIECES)]

        zu, zv = project(0)
        gu_scr[rows, :] = _gelu_tanh(zu)
        gv = _gelu_tanh(zv)

        zq, zf = project(2)
        q_ref[rows, :] = (zq * _sigmoid_t(zq)).astype(BF16)
        th = 0.5 * jnp.tanh(0.5 * zf)
        f_ref[rows, :] = lb + (1.0 - lb) * (0.5 + th)
        k_ref[rows, :] = ((1.0 - lb) * (0.5 - th)).astype(BF16)

        zi, zog = project(4)
        v_ref[rows, :] = zi.astype(BF16)
        sog_ref[rows, :] = _sigmoid_t(zog).astype(BF16)
        zga, zgb = project(6)
        sga_ref[rows, :] = _sigmoid_t(zga).astype(BF16)
        sgb_ref[rows, :] = _sigmoid_t(zgb).astype(BF16)

        for g in range(GMLP_GROUPS):
            ls = slice(g * HEAD_DIM, (g + 1) * HEAD_DIM)
            xg = gv[:, ls]
            mu = jnp.mean(xg, axis=-1, keepdims=True)
            xc = xg - mu
            var = jnp.mean(xc * xc, axis=-1, keepdims=True)
            vn = (xc * lax.rsqrt(var + EPS)).astype(BF16)
            rhs = jnp.concatenate(
                [vn[n * BLOCK:(n + 1) * BLOCK, :] for n in range(nblk)], axis=1)
            wc = jnp.where(causal, ws_ref[g], 0.0).astype(BF16)
            mixed = jnp.dot(wc, rhs, preferred_element_type=F32)
            bias = jnp.broadcast_to(bs_ref[:, g:g + 1], (BLOCK, BLOCK))
            for n in range(nblk):
                rs = slice((b0 + n) * BLOCK, (b0 + n + 1) * BLOCK)
                mx = mixed[:, n * BLOCK:(n + 1) * BLOCK] + bias
                yap_ref[rs, ls] = (gu_scr[rs, ls] * mx).astype(BF16)


def _mix_in(h16, mod, nw, w_in, ws_p, bs_p, lb, *, seq):
    n = h16.shape[0] * POSITIONS
    tm = M1_TM
    tiles_per_seq = seq // tm
    tile = pl.BlockSpec((tm, D_MODEL), lambda i: (i, 0))
    const2 = lambda i: (0, 0)
    f32_out = jax.ShapeDtypeStruct((n, D_MODEL), F32)
    bf_out = jax.ShapeDtypeStruct((n, D_MODEL), BF16)
    return pl.pallas_call(
        _mix_in_kernel,
        grid=(n // tm,),
        in_specs=[
            pl.BlockSpec((tm // POSITIONS, CHUNK_WIDTH), lambda i: (i, 0)),
            pl.BlockSpec((None, N_ADA, D_MODEL), lambda i: (i // tiles_per_seq, 0, 0)),
            pl.BlockSpec((1, D_MODEL), const2),
            pl.BlockSpec((D_MODEL, N_PIECES * D_MODEL), const2,
                         pipeline_mode=pl.Buffered(1)),
            pl.BlockSpec((GMLP_GROUPS, BLOCK, BLOCK), lambda i: (0, 0, 0)),
            pl.BlockSpec((BLOCK, GMLP_GROUPS), const2),
            pl.BlockSpec((1, D_MODEL), const2),
        ],
        out_specs=[tile] * 8,
        out_shape=[bf_out, bf_out, bf_out, f32_out, bf_out, bf_out, bf_out, bf_out],
        scratch_shapes=[pltpu.VMEM((tm, D_MODEL), F32)],
        compiler_params=pltpu.CompilerParams(
            dimension_semantics=("parallel",), vmem_limit_bytes=VMEM_LIMIT),
        name="mixer_in_proj",
    )(h16, mod, nw, w_in, ws_p, bs_p, lb)


HG_TM = 1024
N_MASKS = 7


def _seg_scan_rows(x, rowid, width, reverse):
    d = 1
    y = x
    while d < width:
        if reverse:
            shifted = pltpu.roll(y, CHUNKS - d, axis=0)
            ok = (rowid & (width - 1)) < width - d
        else:
            shifted = pltpu.roll(y, d, axis=0)
            ok = (rowid & (width - 1)) >= d
        y = y * jnp.where(ok, shifted, 1.0)
        d *= 2
    return y


def _hgrn_kernel(q_ref, k_ref, f_ref, v_ref, o_ref, st_ref, mask_ref):
    n_blocks = q_ref.shape[0] // BLOCK

    @pl.when(pl.program_id(1) == 0)
    def _():
        st_ref[...] = jnp.zeros_like(st_ref)

    pr = lax.broadcasted_iota(jnp.int32, (BLOCK, BLOCK), 0)
    pc = lax.broadcasted_iota(jnp.int32, (BLOCK, BLOCK), 1)
    same_chunk = ((pr ^ pc) & (CHUNKS - 1)) == 0
    mask_ref[0] = jnp.where(pr == pc, 1.0, 0.0)
    for beta in range(4):
        sh = 3 + beta + 1
        mask_ref[1 + beta] = jnp.where(same_chunk & ((pr >> sh) == (pc >> sh)), 1.0, 0.0)
    for gamma in range(2):
        sh = gamma + 1
        mask_ref[5 + gamma] = jnp.where(
            ((pr & (CHUNKS - 1)) >> sh) == ((pc & (CHUNKS - 1)) >> sh), 1.0, 0.0)

    rowid = lax.broadcasted_iota(jnp.int32, (CHUNKS, HEAD_DIM), 0)
    zero = jnp.zeros((CHUNKS, HEAD_DIM), F32)

    def stack(slabs):
        return jnp.concatenate(slabs, axis=0).astype(BF16)

    def block_step(n, carry):
        r0 = pl.multiple_of(n * BLOCK, BLOCK)
        for h in range(N_HEADS):
            ls = slice(h * HEAD_DIM, (h + 1) * HEAD_DIM)

            def slabs(ref):
                val = ref[pl.ds(r0, BLOCK), ls].astype(F32)
                return [val[j * CHUNKS:(j + 1) * CHUNKS, :] for j in range(POSITIONS)]

            fs, qs, ks = slabs(f_ref), slabs(q_ref), slabs(k_ref)
            vb = v_ref[pl.ds(r0, BLOCK), ls]

            def mask_rows(level, j):
                return mask_ref[level, j * CHUNKS:(j + 1) * CHUNKS, :]

            p = [jnp.sum(qs[j] * ks[j], axis=-1, keepdims=True) * mask_rows(0, j)
                 for j in range(POSITIONS)]
            levels = []
            key_mats = []

            for beta in range(4):
                w = 1 << beta
                a = [None] * POSITIONS
                for j in range(POSITIONS):
                    if j & w:
                        a[j] = fs[j] if (j & (w - 1)) == 0 else a[j - 1] * fs[j]
                b = [None] * POSITIONS
                for j in range(POSITIONS - 1, -1, -1):
                    if not (j & w) and (j & (w - 1)) != w - 1:
                        b[j] = fs[j + 1] if b[j + 1] is None else b[j + 1] * fs[j + 1]
                upper = [j for j in range(POSITIONS) if j & w]
                levels.append(([qs[j] * a[j] for j in upper], upper, 1 + beta))
                key_mats.append([zero if (j & w) else (ks[j] if b[j] is None else ks[j] * b[j])
                                 for j in range(POSITIONS)])

            gcum = [fs[0]]
            for j in range(1, POSITIONS):
                gcum.append(gcum[-1] * fs[j])
            hsuf = [None] * POSITIONS
            for j in range(POSITIONS - 2, -1, -1):
                hsuf[j] = fs[j + 1] if hsuf[j + 1] is None else hsuf[j + 1] * fs[j + 1]
            g = gcum[-1]
            qg = [qs[j] * gcum[j] for j in range(POSITIONS)]
            kh = [ks[j] if hsuf[j] is None else ks[j] * hsuf[j] for j in range(POSITIONS)]
            g_prev = pltpu.roll(g, 1, axis=0)
            g_next = pltpu.roll(g, CHUNKS - 1, axis=0)

            for gamma in range(3):
                w = 1 << gamma
                low = rowid & (w - 1)
                ca = _seg_scan_rows(jnp.where(low == 0, 1.0, g_prev), rowid, w, False)
                ca = jnp.where((rowid & w) != 0, ca, 0.0)
                cb = _seg_scan_rows(jnp.where(low == w - 1, 1.0, g_next), rowid, w, True)
                cb = jnp.where((rowid & w) == 0, cb, 0.0)
                levels.append(([x * ca for x in qg], list(range(POSITIONS)),
                               5 + gamma if gamma < 2 else None))
                key_mats.append([x * cb for x in kh])

            cp = _seg_scan_rows(jnp.where(rowid == 0, 1.0, g_prev), rowid, CHUNKS, False)
            cs = _seg_scan_rows(jnp.where(rowid == CHUNKS - 1, 1.0, g_next), rowid, CHUNKS, True)
            total = (cp * g)[CHUNKS - 1:CHUNKS, :]
            key_mats.append([x * cs for x in kh])

            def transposed(key_slabs):
                return jnp.transpose(jnp.concatenate(key_slabs, axis=0)).astype(BF16)

            for (q_slabs, targets, mask_idx), key_slabs in zip(levels, key_mats):
                s = jnp.dot(stack(q_slabs), transposed(key_slabs), preferred_element_type=F32)
                for i, j in enumerate(targets):
                    sj = s[i * CHUNKS:(i + 1) * CHUNKS, :]
                    p[j] = p[j] + (sj if mask_idx is None else sj * mask_rows(mask_idx, j))

            st = st_ref[h]
            lhs = jnp.concatenate([stack(p), stack([x * cp for x in qg])], axis=1)
            rhs = jnp.concatenate([vb, st.astype(BF16)], axis=0)
            o_ref[pl.ds(r0, BLOCK), ls] = jnp.dot(lhs, rhs, preferred_element_type=F32)
            upd = jnp.dot(transposed(key_mats[-1]), vb, preferred_element_type=F32)
            decay_col = jnp.transpose(jnp.broadcast_to(total, (HEAD_DIM, HEAD_DIM)))
            st_ref[h] = st * decay_col + upd
        return carry

    lax.fori_loop(0, n_blocks, block_step, 0)


def _hgrn(q, k, f, v, *, batch, seq):
    n = q.shape[0]
    tiles_per_seq = seq // HG_TM
    tile = pl.BlockSpec((HG_TM, D_MODEL), lambda b, t: (b * tiles_per_seq + t, 0))
    return pl.pallas_call(
        _hgrn_kernel,
        grid=(batch, tiles_per_seq),
        in_specs=[tile] * 4,
        out_specs=tile,
        out_shape=jax.ShapeDtypeStruct((n, D_MODEL), F32),
        scratch_shapes=[
            pltpu.VMEM((N_HEADS, HEAD_DIM, HEAD_DIM), F32),
            pltpu.VMEM((N_MASKS, BLOCK, BLOCK), F32),
        ],
        compiler_params=pltpu.CompilerParams(
            dimension_semantics=("parallel", "arbitrary"),
            vmem_limit_bytes=VMEM_LIMIT),
        name="hgrn2_recurrence",
    )(q, k, f, v)


M3_TM = 1024
M3_CHAINS = 2


def _mix_out_kernel(x_ref, mod_ref, yap_ref, o_ref, sog_ref, sga_ref, sgb_ref,
                    wa_ref, wb_ref, wo_ref, out_ref):
    nblk = yap_ref.shape[0] // BLOCK // M3_CHAINS
    gate = mod_ref[5:6, :]
    for chain in range(M3_CHAINS):
        b0 = chain * nblk
        rows = slice(b0 * BLOCK, (b0 + nblk) * BLOCK)
        o = o_ref[rows, :]
        parts = []
        for h in range(N_HEADS):
            oh = o[:, h * HEAD_DIM:(h + 1) * HEAD_DIM]
            ms = jnp.mean(oh * oh, axis=-1, keepdims=True)
            parts.append(oh * lax.rsqrt(ms + EPS))
        on = (jnp.concatenate(parts, axis=1) * sog_ref[rows, :].astype(F32)).astype(BF16)
        ya = jnp.dot(yap_ref[rows, :], wa_ref[...], preferred_element_type=F32)
        yb = jnp.dot(on, wb_ref[...], preferred_element_type=F32)
        m = (sga_ref[rows, :].astype(F32) * ya + sgb_ref[rows, :].astype(F32) * yb).astype(BF16)
        out = jnp.dot(m, wo_ref[...], preferred_element_type=F32)
        y = _gather_block_rows(x_ref, b0, nblk) + (1.0 + gate) * out
        _scatter_block_rows(out_ref, y, b0, nblk)


def _mix_out(h16, mod, yap, o, sog, sga, sgb, w_a, w_b, w_o, *, seq):
    n = h16.shape[0] * POSITIONS
    tm = M3_TM
    tiles_per_seq = seq // tm
    tile = pl.BlockSpec((tm, D_MODEL), lambda i: (i, 0))
    tile16 = pl.BlockSpec((tm // POSITIONS, CHUNK_WIDTH), lambda i: (i, 0))
    wspec = pl.BlockSpec((D_MODEL, D_MODEL), lambda i: (0, 0),
                         pipeline_mode=pl.Buffered(1))
    return pl.pallas_call(
        _mix_out_kernel,
        grid=(n // tm,),
        in_specs=[
            tile16,
            pl.BlockSpec((None, N_ADA, D_MODEL), lambda i: (i // tiles_per_seq, 0, 0)),
            tile, tile, tile, tile, tile,
            wspec, wspec, wspec,
        ],
        out_specs=tile16,
        out_shape=jax.ShapeDtypeStruct(h16.shape, F32),
        compiler_params=pltpu.CompilerParams(
            dimension_semantics=("parallel",), vmem_limit_bytes=VMEM_LIMIT),
        name="mixer_out_proj",
    )(h16, mod, yap, o, sog, sga, sgb, w_a, w_b, w_o)


def kernel(x, c, w_in, gmlp_ws, gmlp_b, hgrn_lb_logits, w_branch_a, w_branch_b, w_out,
           ffn_w13, ffn_w2, norm_w, ada_w, ada_b, final_norm_w):
    batch, seq, d = x.shape
    assert d == D_MODEL and seq % HG_TM == 0 and seq % FFN_TM == 0
    n = batch * seq

    mod_all = _ada_mod(c, ada_w, ada_b).reshape(DEPTH, batch, N_ADA, D_MODEL)
    lb_all = _lower_bounds(hgrn_lb_logits)
    perm = jnp.array([_block_token(p) for p in range(BLOCK)], dtype=jnp.int32)
    ws_p = gmlp_ws[:, :, perm, :][:, :, :, perm]
    bs_p = jnp.swapaxes(gmlp_b[:, :, perm], 1, 2)

    h16 = x.reshape(n // POSITIONS, CHUNK_WIDTH)
    for l in range(DEPTH):
        mod = mod_all[l]
        nw = norm_w[l].reshape(3, 1, D_MODEL)
        h16 = _ffn(h16, mod, nw[0], ffn_w13[l, 0].astype(BF16), ffn_w2[l, 0].astype(BF16),
                   mod_off=0, seq=seq)
        yap, q, k, f, v, sog, sga, sgb = _mix_in(
            h16, mod, nw[1], w_in[l].astype(BF16), ws_p[l], bs_p[l],
            lb_all[l].reshape(1, D_MODEL), seq=seq)
        o = _hgrn(q, k, f, v, batch=batch, seq=seq)
        h16 = _mix_out(h16, mod, yap, o, sog, sga, sgb, w_branch_a[l].astype(BF16),
                       w_branch_b[l].astype(BF16), w_out[l].astype(BF16), seq=seq)
        final_w = final_norm_w.reshape(1, D_MODEL) if l == DEPTH - 1 else None
        h16 = _ffn(h16, mod, nw[2], ffn_w13[l, 1].astype(BF16), ffn_w2[l, 1].astype(BF16),
                   mod_off=6, seq=seq, final_w=final_w)
    return h16.reshape(batch, seq, D_MODEL)
```

```python
import functools

import jax
import jax.numpy as jnp
from jax import lax
from jax.experimental import pallas as pl
from jax.experimental.pallas import tpu as pltpu

F32 = jnp.float32
BF16 = jnp.bfloat16

D_MODEL = 1024
DEPTH = 4
FFN_HIDDEN = 2816
N_ADA = 9
EPS = 1e-6
GMLP_GROUPS = 8
HEAD_DIM = 128
N_HEADS = D_MODEL // HEAD_DIM
N_PIECES = 8

BLOCK = 128
POSITIONS = 16
CHUNKS = BLOCK // POSITIONS
CHUNK_WIDTH = POSITIONS * D_MODEL

VMEM_LIMIT = 56 * 1024 * 1024


def _sigmoid(x):
    return 1.0 / (1.0 + jnp.exp(-x))


def _sigmoid_t(x):
    return 0.5 * jnp.tanh(0.5 * x) + 0.5


def _gelu_tanh(x):
    c = 0.7978845608028654
    return 0.5 * x * (1.0 + jnp.tanh(c * (x + 0.044715 * (x * x * x))))


def _rms_modulate(x, nw, scale, shift):
    ms = jnp.mean(x * x, axis=-1, keepdims=True)
    return (x * lax.rsqrt(ms + EPS) * nw) * (1.0 + scale) + shift


def _block_token(p):
    return (p % CHUNKS) * POSITIONS + p // CHUNKS


def _gather_block_rows(x_ref, b0, n_blocks):
    pieces = [x_ref[n * CHUNKS:(n + 1) * CHUNKS, j * D_MODEL:(j + 1) * D_MODEL]
              for n in range(b0, b0 + n_blocks) for j in range(POSITIONS)]
    return jnp.concatenate(pieces, axis=0)


def _scatter_block_rows(out_ref, y, b0, n_blocks):
    for n in range(n_blocks):
        for j in range(POSITIONS):
            r = (n * POSITIONS + j) * CHUNKS
            out_ref[(b0 + n) * CHUNKS:(b0 + n + 1) * CHUNKS,
                    j * D_MODEL:(j + 1) * D_MODEL] = y[r:r + CHUNKS, :]


def _ada_kernel(c_ref, w_ref, b_ref, o_ref):
    c = c_ref[...]
    ca = c * _sigmoid(c)
    o_ref[...] = jnp.dot(ca, w_ref[...], preferred_element_type=F32,
                         precision=lax.Precision.HIGHEST) + b_ref[...]


def _ada_mod(c, ada_w, ada_b):
    b = c.shape[0]
    return pl.pallas_call(
        _ada_kernel,
        grid=(DEPTH, N_ADA),
        in_specs=[
            pl.BlockSpec((b, D_MODEL), lambda l, j: (0, 0)),
            pl.BlockSpec((None, D_MODEL, D_MODEL), lambda l, j: (l, 0, j)),
            pl.BlockSpec((None, 1, D_MODEL), lambda l, j: (l, 0, j)),
        ],
        out_specs=pl.BlockSpec((None, b, D_MODEL), lambda l, j: (l, 0, j)),
        out_shape=jax.ShapeDtypeStruct((DEPTH, b, N_ADA * D_MODEL), F32),
        name="ada_mod",
    )(c, ada_w, ada_b.reshape(DEPTH, 1, N_ADA * D_MODEL))


def _lb_kernel(x_ref, o_ref):
    x = x_ref[...]
    rows = [x[i:i + 1, :] for i in range(DEPTH)]
    m = rows[0]
    for r in rows[1:]:
        m = jnp.maximum(m, r)
    e = [jnp.exp(r - m) for r in rows]
    tot = e[0]
    for t in e[1:]:
        tot = tot + t
    p = [t / tot for t in e]
    cum = p[0]
    first = cum
    for i in range(DEPTH):
        if i > 0:
            cum = cum + p[i]
        o_ref[i:i + 1, :] = cum - first


def _lower_bounds(logits):
    return pl.pallas_call(
        _lb_kernel,
        out_shape=jax.ShapeDtypeStruct(logits.shape, F32),
        name="hgrn_lower_bounds",
    )(logits.astype(F32))


FFN_TM = 1024
FFN_CHAINS = 2
FFN_COL_CHUNKS = ((0, 1024), (1024, 2048), (2048, FFN_HIDDEN))


def _ffn_kernel(x_ref, mod_ref, nw_ref, w13_ref, w2_ref, *rest, mod_off, final):
    if final:
        fnw_ref, o_ref = rest
    else:
        (o_ref,) = rest
    shift = mod_ref[mod_off:mod_off + 1, :]
    scale = mod_ref[mod_off + 1:mod_off + 2, :]
    gate = mod_ref[mod_off + 2:mod_off + 3, :]
    rows = x_ref.shape[0] // FFN_CHAINS
    for r in range(FFN_CHAINS):
        rs = slice(r * rows, (r + 1) * rows)
        x = jnp.concatenate(
            [x_ref[rs, j * D_MODEL:(j + 1) * D_MODEL] for j in range(POSITIONS)], axis=0)
        hb = _rms_modulate(x, nw_ref[...], scale, shift).astype(BF16)
        acc = None
        for c0, c1 in FFN_COL_CHUNKS:
            a = jnp.dot(hb, w13_ref[:, c0:c1], preferred_element_type=F32)
            b = jnp.dot(hb, w13_ref[:, FFN_HIDDEN + c0:FFN_HIDDEN + c1],
                        preferred_element_type=F32)
            act = (a * _sigmoid_t(a) * b).astype(BF16)
            part = jnp.dot(act, w2_ref[c0:c1, :], preferred_element_type=F32)
            acc = part if acc is None else acc + part
        y = x + (0.5 * (1.0 + gate)) * acc
        if final:
            ms = jnp.mean(y * y, axis=-1, keepdims=True)
            y = y * lax.rsqrt(ms + EPS) * fnw_ref[...]
        for j in range(POSITIONS):
            o_ref[rs, j * D_MODEL:(j + 1) * D_MODEL] = y[j * rows:(j + 1) * rows, :]


def _ffn(h16, mod, nw, w13, w2, *, mod_off, seq, final_w=None):
    n = h16.shape[0] * POSITIONS
    tm = FFN_TM
    tiles_per_seq = seq // tm
    final = final_w is not None
    tile16 = pl.BlockSpec((tm // POSITIONS, CHUNK_WIDTH), lambda i: (i, 0))
    in_specs = [
        tile16,
        pl.BlockSpec((None, N_ADA, D_MODEL), lambda i: (i // tiles_per_seq, 0, 0)),
        pl.BlockSpec((1, D_MODEL), lambda i: (0, 0)),
        pl.BlockSpec((D_MODEL, 2 * FFN_HIDDEN), lambda i: (0, 0),
                     pipeline_mode=pl.Buffered(1)),
        pl.BlockSpec((FFN_HIDDEN, D_MODEL), lambda i: (0, 0),
                     pipeline_mode=pl.Buffered(1)),
    ]
    args = [h16, mod, nw, w13, w2]
    if final:
        in_specs.append(pl.BlockSpec((1, D_MODEL), lambda i: (0, 0)))
        args.append(final_w)
    return pl.pallas_call(
        functools.partial(_ffn_kernel, mod_off=mod_off, final=final),
        grid=(n // tm,),
        in_specs=in_specs,
        out_specs=tile16,
        out_shape=jax.ShapeDtypeStruct(h16.shape, F32),
        compiler_params=pltpu.CompilerParams(
            dimension_semantics=("parallel",), vmem_limit_bytes=VMEM_LIMIT),
        name="ffn_half_step",
    )(*args)


M1_TM = 512
M1_PIECES = 2
M1_CHAINS = 2
N_MASKS = 7


def _seg_scan_rows(x, rowid, width, reverse):
    d = 1
    y = x
    while d < width:
        if reverse:
            shifted = pltpu.roll(y, CHUNKS - d, axis=0)
            ok = (rowid & (width - 1)) < width - d
        else:
            shifted = pltpu.roll(y, d, axis=0)
            ok = (rowid & (width - 1)) >= d
        y = y * jnp.where(ok, shifted, 1.0)
        d *= 2
    return y


def _write_level_masks(mask_ref):
    pr = lax.broadcasted_iota(jnp.int32, (BLOCK, BLOCK), 0)
    pc = lax.broadcasted_iota(jnp.int32, (BLOCK, BLOCK), 1)
    same_chunk = ((pr ^ pc) & (CHUNKS - 1)) == 0
    mask_ref[0] = jnp.where(pr == pc, 1.0, 0.0)
    for beta in range(4):
        sh = 3 + beta + 1
        mask_ref[1 + beta] = jnp.where(same_chunk & ((pr >> sh) == (pc >> sh)), 1.0, 0.0)
    for gamma in range(2):
        sh = gamma + 1
        mask_ref[5 + gamma] = jnp.where(
            ((pr & (CHUNKS - 1)) >> sh) == ((pc & (CHUNKS - 1)) >> sh), 1.0, 0.0)


def _hgrn_head_block(f, q, k, vb, st_ref, h, mask_ref):
    rowid = lax.broadcasted_iota(jnp.int32, (CHUNKS, HEAD_DIM), 0)
    zero = jnp.zeros((CHUNKS, HEAD_DIM), F32)

    def slabs(val):
        return [val[j * CHUNKS:(j + 1) * CHUNKS, :] for j in range(POSITIONS)]

    def stack(sl):
        return jnp.concatenate(sl, axis=0).astype(BF16)

    def transposed(key_slabs):
        return jnp.transpose(jnp.concatenate(key_slabs, axis=0)).astype(BF16)

    def mask_rows(level, j):
        return mask_ref[level, j * CHUNKS:(j + 1) * CHUNKS, :]

    fs, qs, ks = slabs(f), slabs(q), slabs(k)

    p = [jnp.sum(qs[j] * ks[j], axis=-1, keepdims=True) * mask_rows(0, j)
         for j in range(POSITIONS)]
    levels = []
    key_mats = []

    for beta in range(4):
        w = 1 << beta
        a = [None] * POSITIONS
        for j in range(POSITIONS):
            if j & w:
                a[j] = fs[j] if (j & (w - 1)) == 0 else a[j - 1] * fs[j]
        b = [None] * POSITIONS
        for j in range(POSITIONS - 1, -1, -1):
            if not (j & w) and (j & (w - 1)) != w - 1:
                b[j] = fs[j + 1] if b[j + 1] is None else b[j + 1] * fs[j + 1]
        upper = [j for j in range(POSITIONS) if j & w]
        levels.append(([qs[j] * a[j] for j in upper], upper, 1 + beta))
        key_mats.append([zero if (j & w) else (ks[j] if b[j] is None else ks[j] * b[j])
                         for j in range(POSITIONS)])

    gcum = [fs[0]]
    for j in range(1, POSITIONS):
        gcum.append(gcum[-1] * fs[j])
    hsuf = [None] * POSITIONS
    for j in range(POSITIONS - 2, -1, -1):
        hsuf[j] = fs[j + 1] if hsuf[j + 1] is None else hsuf[j + 1] * fs[j + 1]
    g = gcum[-1]
    qg = [qs[j] * gcum[j] for j in range(POSITIONS)]
    kh = [ks[j] if hsuf[j] is None else ks[j] * hsuf[j] for j in range(POSITIONS)]
    g_prev = pltpu.roll(g, 1, axis=0)
    g_next = pltpu.roll(g, CHUNKS - 1, axis=0)

    for gamma in range(3):
        w = 1 << gamma
        low = rowid & (w - 1)
        ca = _seg_scan_rows(jnp.where(low == 0, 1.0, g_prev), rowid, w, False)
        ca = jnp.where((rowid & w) != 0, ca, 0.0)
        cb = _seg_scan_rows(jnp.where(low == w - 1, 1.0, g_next), rowid, w, True)
        cb = jnp.where((rowid & w) == 0, cb, 0.0)
        levels.append(([x * ca for x in qg], list(range(POSITIONS)),
                       5 + gamma if gamma < 2 else None))
        key_mats.append([x * cb for x in kh])

    for (q_slabs, targets, mask_idx), key_slabs in zip(levels, key_mats):
        s = jnp.dot(stack(q_slabs), transposed(key_slabs), preferred_element_type=F32)
        for i, j in enumerate(targets):
            sj = s[i * CHUNKS:(i + 1) * CHUNKS, :]
            p[j] = p[j] + (sj if mask_idx is None else sj * mask_rows(mask_idx, j))

    cp = _seg_scan_rows(jnp.where(rowid == 0, 1.0, g_prev), rowid, CHUNKS, False)
    cs = _seg_scan_rows(jnp.where(rowid == CHUNKS - 1, 1.0, g_next), rowid, CHUNKS, True)
    total = (cp * g)[CHUNKS - 1:CHUNKS, :]
    st = st_ref[h]
    lhs = jnp.concatenate([stack(p), stack([x * cp for x in qg])], axis=1)
    rhs = jnp.concatenate([vb, st.astype(BF16)], axis=0)
    out = jnp.dot(lhs, rhs, preferred_element_type=F32)
    upd = jnp.dot(transposed([x * cs for x in kh]), vb, preferred_element_type=F32)
    decay_col = jnp.transpose(jnp.broadcast_to(total, (HEAD_DIM, HEAD_DIM)))
    st_ref[h] = st * decay_col + upd
    return out


def _mix_in_kernel(x_ref, mod_ref, nw_ref, win_ref, ws_ref, bs_ref, lb_ref,
                   yap_ref, o_ref, sog_ref, sga_ref, sgb_ref,
                   gu_scr, st_ref, mask_ref):
    tm = yap_ref.shape[0]
    nblk = tm // BLOCK // M1_CHAINS

    @pl.when(pl.program_id(1) == 0)
    def _():
        st_ref[...] = jnp.zeros_like(st_ref)

    _write_level_masks(mask_ref)
    shift = mod_ref[3:4, :]
    scale = mod_ref[4:5, :]
    row = _block_token(lax.broadcasted_iota(jnp.int32, (BLOCK, BLOCK), 0))
    col = _block_token(lax.broadcasted_iota(jnp.int32, (BLOCK, BLOCK), 1))
    causal = row >= col
    lb = lb_ref[...]

    def project_stage(chain, res):
        b0 = chain * nblk
        rows = slice(b0 * BLOCK, (b0 + nblk) * BLOCK)
        x = _gather_block_rows(x_ref, b0, nblk)
        hb = _rms_modulate(x, nw_ref[...], scale, shift).astype(BF16)

        def project(p0):
            z = jnp.dot(hb, win_ref[:, p0 * D_MODEL:(p0 + M1_PIECES) * D_MODEL],
                        preferred_element_type=F32)
            return [z[:, i * D_MODEL:(i + 1) * D_MODEL] for i in range(M1_PIECES)]

        zu, zv = project(0)
        gu_scr[rows, :] = _gelu_tanh(zu)
        res["gv"] = _gelu_tanh(zv)
        yield
        zq, zf = project(2)
        res["q"] = zq * _sigmoid_t(zq)
        th = 0.5 * jnp.tanh(0.5 * zf)
        res["f"] = lb + (1.0 - lb) * (0.5 + th)
        res["k"] = (1.0 - lb) * (0.5 - th)
        yield
        zi, zog = project(4)
        res["vb"] = zi.astype(BF16)
        sog_ref[rows, :] = _sigmoid_t(zog).astype(BF16)
        yield
        zga, zgb = project(6)
        sga_ref[rows, :] = _sigmoid_t(zga).astype(BF16)
        sgb_ref[rows, :] = _sigmoid_t(zgb).astype(BF16)
        yield

    def gmlp_stage(chain, res):
        b0 = chain * nblk
        gv = res["gv"]
        for g in range(GMLP_GROUPS):
            ls = slice(g * HEAD_DIM, (g + 1) * HEAD_DIM)
            xg = gv[:, ls]
            mu = jnp.mean(xg, axis=-1, keepdims=True)
            xc = xg - mu
            var = jnp.mean(xc * xc, axis=-1, keepdims=True)
            vn = (xc * lax.rsqrt(var + EPS)).astype(BF16)
            rhs = jnp.concatenate(
                [vn[n * BLOCK:(n + 1) * BLOCK, :] for n in range(nblk)], axis=1)
            wc = jnp.where(causal, ws_ref[g], 0.0).astype(BF16)
            mixed = jnp.dot(wc, rhs, preferred_element_type=F32)
            bias = jnp.broadcast_to(bs_ref[:, g:g + 1], (BLOCK, BLOCK))
            for n in range(nblk):
                rs = slice((b0 + n) * BLOCK, (b0 + n + 1) * BLOCK)
                mx = mixed[:, n * BLOCK:(n + 1) * BLOCK] + bias
                yap_ref[rs, ls] = (gu_scr[rs, ls] * mx).astype(BF16)

    def hgrn_stage(chain, res):
        b0 = chain * nblk
        for n in range(nblk):
            br = slice(n * BLOCK, (n + 1) * BLOCK)
            orows = slice((b0 + n) * BLOCK, (b0 + n + 1) * BLOCK)
            for h in range(N_HEADS):
                ls = slice(h * HEAD_DIM, (h + 1) * HEAD_DIM)
                o_ref[orows, ls] = _hgrn_head_block(
                    res["f"][br, ls], res["q"][br, ls], res["k"][br, ls], res["vb"][br, ls],
                    st_ref, h, mask_ref)
                yield

    for chain in range(M1_CHAINS):
        res = {}
        for _ in project_stage(chain, res):
            pass
        gmlp_stage(chain, res)
        for _ in hgrn_stage(chain, res):
            pass


def _mix_in(h16, mod, nw, w_in, ws_p, bs_p, lb, *, batch, seq):
    n = h16.shape[0] * POSITIONS
    tm = M1_TM
    tiles_per_seq = seq // tm
    tile = pl.BlockSpec((tm, D_MODEL), lambda b, t: (b * tiles_per_seq + t, 0))
    const2 = lambda b, t: (0, 0)
    f32_out = jax.ShapeDtypeStruct((n, D_MODEL), F32)
    bf_out = jax.ShapeDtypeStruct((n, D_MODEL), BF16)
    return pl.pallas_call(
        _mix_in_kernel,
        grid=(batch, tiles_per_seq),
        in_specs=[
            pl.BlockSpec((tm // POSITIONS, CHUNK_WIDTH), lambda b, t: (b * tiles_per_seq + t, 0)),
            pl.BlockSpec((None, N_ADA, D_MODEL), lambda b, t: (b, 0, 0)),
            pl.BlockSpec((1, D_MODEL), const2),
            pl.BlockSpec((D_MODEL, N_PIECES * D_MODEL), const2,
                         pipeline_mode=pl.Buffered(1)),
            pl.BlockSpec((GMLP_GROUPS, BLOCK, BLOCK), lambda b, t: (0, 0, 0)),
            pl.BlockSpec((BLOCK, GMLP_GROUPS), const2),
            pl.BlockSpec((1, D_MODEL), const2),
        ],
        out_specs=[tile] * 5,
        out_shape=[bf_out, f32_out, bf_out, bf_out, bf_out],
        scratch_shapes=[
            pltpu.VMEM((tm, D_MODEL), F32),
            pltpu.VMEM((N_HEADS, HEAD_DIM, HEAD_DIM), F32),
            pltpu.VMEM((N_MASKS, BLOCK, BLOCK), F32),
        ],
        compiler_params=pltpu.CompilerParams(
            dimension_semantics=("parallel", "arbitrary"), vmem_limit_bytes=VMEM_LIMIT),
        name="mixer_in_proj_hgrn2",
    )(h16, mod, nw, w_in, ws_p, bs_p, lb)


M3_TM = 1024
M3_CHAINS = 2


def _mix_out_kernel(x_ref, mod_ref, yap_ref, o_ref, sog_ref, sga_ref, sgb_ref,
                    wa_ref, wb_ref, wo_ref, out_ref):
    nblk = yap_ref.shape[0] // BLOCK // M3_CHAINS
    gate = mod_ref[5:6, :]
    for chain in range(M3_CHAINS):
        b0 = chain * nblk
        rows = slice(b0 * BLOCK, (b0 + nblk) * BLOCK)
        o = o_ref[rows, :]
        parts = []
        for h in range(N_HEADS):
            oh = o[:, h * HEAD_DIM:(h + 1) * HEAD_DIM]
            ms = jnp.mean(oh * oh, axis=-1, keepdims=True)
            parts.append(oh * lax.rsqrt(ms + EPS))
        on = (jnp.concatenate(parts, axis=1) * sog_ref[rows, :].astype(F32)).astype(BF16)
        ya = jnp.dot(yap_ref[rows, :], wa_ref[...], preferred_element_type=F32)
        yb = jnp.dot(on, wb_ref[...], preferred_element_type=F32)
        m = (sga_ref[rows, :].astype(F32) * ya + sgb_ref[rows, :].astype(F32) * yb).astype(BF16)
        out = jnp.dot(m, wo_ref[...], preferred_element_type=F32)
        y = _gather_block_rows(x_ref, b0, nblk) + (1.0 + gate) * out
        _scatter_block_rows(out_ref, y, b0, nblk)


def _mix_out(h16, mod, yap, o, sog, sga, sgb, w_a, w_b, w_o, *, seq):
    n = h16.shape[0] * POSITIONS
    tm = M3_TM
    tiles_per_seq = seq // tm
    tile = pl.BlockSpec((tm, D_MODEL), lambda i: (i, 0))
    tile16 = pl.BlockSpec((tm // POSITIONS, CHUNK_WIDTH), lambda i: (i, 0))
    wspec = pl.BlockSpec((D_MODEL, D_MODEL), lambda i: (0, 0),
                         pipeline_mode=pl.Buffered(1))
    return pl.pallas_call(
        _mix_out_kernel,
        grid=(n // tm,),
        in_specs=[
            tile16,
            pl.BlockSpec((None, N_ADA, D_MODEL), lambda i: (i // tiles_per_seq, 0, 0)),
            tile, tile, tile, tile, tile,
            wspec, wspec, wspec,
        ],
        out_specs=tile16,
        out_shape=jax.ShapeDtypeStruct(h16.shape, F32),
        compiler_params=pltpu.CompilerParams(
            dimension_semantics=("parallel",), vmem_limit_bytes=VMEM_LIMIT),
        name="mixer_out_proj",
    )(h16, mod, yap, o, sog, sga, sgb, w_a, w_b, w_o)


def kernel(x, c, w_in, gmlp_ws, gmlp_b, hgrn_lb_logits, w_branch_a, w_branch_b, w_out,
           ffn_w13, ffn_w2, norm_w, ada_w, ada_b, final_norm_w):
    batch, seq, d = x.shape
    assert d == D_MODEL and seq % M1_TM == 0 and seq % FFN_TM == 0 and seq % M3_TM == 0
    n = batch * seq

    mod_all = _ada_mod(c, ada_w, ada_b).reshape(DEPTH, batch, N_ADA, D_MODEL)
    lb_all = _lower_bounds(hgrn_lb_logits)
    perm = jnp.array([_block_token(p) for p in range(BLOCK)], dtype=jnp.int32)
    ws_p = gmlp_ws[:, :, perm, :][:, :, :, perm]
    bs_p = jnp.swapaxes(gmlp_b[:, :, perm], 1, 2)

    h16 = x.reshape(n // POSITIONS, CHUNK_WIDTH)
    for l in range(DEPTH):
        mod = mod_all[l]
        nw = norm_w[l].reshape(3, 1, D_MODEL)
        h16 = _ffn(h16, mod, nw[0], ffn_w13[l, 0].astype(BF16), ffn_w2[l, 0].astype(BF16),
                   mod_off=0, seq=seq)
        yap, o, sog, sga, sgb = _mix_in(
            h16, mod, nw[1], w_in[l].astype(BF16), ws_p[l], bs_p[l],
            lb_all[l].reshape(1, D_MODEL), batch=batch, seq=seq)
        h16 = _mix_out(h16, mod, yap, o, sog, sga, sgb, w_branch_a[l].astype(BF16),
                       w_branch_b[l].astype(BF16), w_out[l].astype(BF16), seq=seq)
        final_w = final_norm_w.reshape(1, D_MODEL) if l == DEPTH - 1 else None
        h16 = _ffn(h16, mod, nw[2], ffn_w13[l, 1].astype(BF16), ffn_w2[l, 1].astype(BF16),
                   mod_off=6, seq=seq, final_w=final_w)
    return h16.reshape(batch, seq, D_MODEL)
```

```python
import functools

import jax
import jax.numpy as jnp
from jax import lax
from jax.experimental import pallas as pl
from jax.experimental.pallas import tpu as pltpu

F32 = jnp.float32
BF16 = jnp.bfloat16

D_MODEL = 1024
DEPTH = 4
FFN_HIDDEN = 2816
N_ADA = 9
EPS = 1e-6
GMLP_GROUPS = 8
HEAD_DIM = 128
N_HEADS = D_MODEL // HEAD_DIM
N_PIECES = 8

BLOCK = 128
POSITIONS = 16
CHUNKS = BLOCK // POSITIONS
CHUNK_WIDTH = POSITIONS * D_MODEL

VMEM_LIMIT = 56 * 1024 * 1024


def _sigmoid(x):
    return 1.0 / (1.0 + jnp.exp(-x))


def _sigmoid_t(x):
    return 0.5 * jnp.tanh(0.5 * x) + 0.5


def _gelu_tanh(x):
    c = 0.7978845608028654
    return 0.5 * x * (1.0 + jnp.tanh(c * (x + 0.044715 * (x * x * x))))


def _rms_modulate(x, nw, scale, shift):
    ms = jnp.mean(x * x, axis=-1, keepdims=True)
    return (x * lax.rsqrt(ms + EPS) * nw) * (1.0 + scale) + shift


def _block_token(p):
    return (p % CHUNKS) * POSITIONS + p // CHUNKS


def _gather_block_rows(x_ref, b0, n_blocks):
    pieces = [x_ref[n * CHUNKS:(n + 1) * CHUNKS, j * D_MODEL:(j + 1) * D_MODEL]
              for n in range(b0, b0 + n_blocks) for j in range(POSITIONS)]
    return jnp.concatenate(pieces, axis=0)


def _scatter_block_rows(out_ref, y, b0, n_blocks):
    for n in range(n_blocks):
        for j in range(POSITIONS):
            r = (n * POSITIONS + j) * CHUNKS
            out_ref[(b0 + n) * CHUNKS:(b0 + n + 1) * CHUNKS,
                    j * D_MODEL:(j + 1) * D_MODEL] = y[r:r + CHUNKS, :]


def _ada_kernel(c_ref, w_ref, b_ref, o_ref):
    c = c_ref[...]
    ca = c * _sigmoid(c)
    o_ref[...] = jnp.dot(ca, w_ref[...], preferred_element_type=F32,
                         precision=lax.Precision.HIGHEST) + b_ref[...]


def _ada_mod(c, ada_w, ada_b):
    b = c.shape[0]
    return pl.pallas_call(
        _ada_kernel,
        grid=(DEPTH, N_ADA),
        in_specs=[
            pl.BlockSpec((b, D_MODEL), lambda l, j: (0, 0)),
            pl.BlockSpec((None, D_MODEL, D_MODEL), lambda l, j: (l, 0, j)),
            pl.BlockSpec((None, 1, D_MODEL), lambda l, j: (l, 0, j)),
        ],
        out_specs=pl.BlockSpec((None, b, D_MODEL), lambda l, j: (l, 0, j)),
        out_shape=jax.ShapeDtypeStruct((DEPTH, b, N_ADA * D_MODEL), F32),
        name="ada_mod",
    )(c, ada_w, ada_b.reshape(DEPTH, 1, N_ADA * D_MODEL))


def _lb_kernel(x_ref, o_ref):
    x = x_ref[...]
    rows = [x[i:i + 1, :] for i in range(DEPTH)]
    m = rows[0]
    for r in rows[1:]:
        m = jnp.maximum(m, r)
    e = [jnp.exp(r - m) for r in rows]
    tot = e[0]
    for t in e[1:]:
        tot = tot + t
    p = [t / tot for t in e]
    cum = p[0]
    first = cum
    for i in range(DEPTH):
        if i > 0:
            cum = cum + p[i]
        o_ref[i:i + 1, :] = cum - first


def _lower_bounds(logits):
    return pl.pallas_call(
        _lb_kernel,
        out_shape=jax.ShapeDtypeStruct(logits.shape, F32),
        name="hgrn_lower_bounds",
    )(logits.astype(F32))


FFN_TM = 1024
FFN_CHAINS = 2
FFN_COL_CHUNKS = ((0, 1024), (1024, 2048), (2048, FFN_HIDDEN))


def _ffn_kernel(x_ref, mod_ref, nw_ref, w13_ref, w2_ref, *rest, mod_off, final):
    if final:
        fnw_ref, o_ref = rest
    else:
        (o_ref,) = rest
    shift = mod_ref[mod_off:mod_off + 1, :]
    scale = mod_ref[mod_off + 1:mod_off + 2, :]
    gate = mod_ref[mod_off + 2:mod_off + 3, :]
    rows = x_ref.shape[0] // FFN_CHAINS
    for r in range(FFN_CHAINS):
        rs = slice(r * rows, (r + 1) * rows)
        x = jnp.concatenate(
            [x_ref[rs, j * D_MODEL:(j + 1) * D_MODEL] for j in range(POSITIONS)], axis=0)
        hb = _rms_modulate(x, nw_ref[...], scale, shift).astype(BF16)
        acc = None
        for c0, c1 in FFN_COL_CHUNKS:
            a = jnp.dot(hb, w13_ref[:, c0:c1], preferred_element_type=F32)
            b = jnp.dot(hb, w13_ref[:, FFN_HIDDEN + c0:FFN_HIDDEN + c1],
                        preferred_element_type=F32)
            act = (a * _sigmoid_t(a) * b).astype(BF16)
            part = jnp.dot(act, w2_ref[c0:c1, :], preferred_element_type=F32)
            acc = part if acc is None else acc + part
        y = x + (0.5 * (1.0 + gate)) * acc
        if final:
            ms = jnp.mean(y * y, axis=-1, keepdims=True)
            y = y * lax.rsqrt(ms + EPS) * fnw_ref[...]
        for j in range(POSITIONS):
            o_ref[rs, j * D_MODEL:(j + 1) * D_MODEL] = y[j * rows:(j + 1) * rows, :]


def _ffn(h16, mod, nw, w13, w2, *, mod_off, seq, final_w=None):
    n = h16.shape[0] * POSITIONS
    tm = FFN_TM
    tiles_per_seq = seq // tm
    final = final_w is not None
    tile16 = pl.BlockSpec((tm // POSITIONS, CHUNK_WIDTH), lambda i: (i, 0))
    in_specs = [
        tile16,
        pl.BlockSpec((None, N_ADA, D_MODEL), lambda i: (i // tiles_per_seq, 0, 0)),
        pl.BlockSpec((1, D_MODEL), lambda i: (0, 0)),
        pl.BlockSpec((D_MODEL, 2 * FFN_HIDDEN), lambda i: (0, 0),
                     pipeline_mode=pl.Buffered(1)),
        pl.BlockSpec((FFN_HIDDEN, D_MODEL), lambda i: (0, 0),
                     pipeline_mode=pl.Buffered(1)),
    ]
    args = [h16, mod, nw, w13, w2]
    if final:
        in_specs.append(pl.BlockSpec((1, D_MODEL), lambda i: (0, 0)))
        args.append(final_w)
    return pl.pallas_call(
        functools.partial(_ffn_kernel, mod_off=mod_off, final=final),
        grid=(n // tm,),
        in_specs=in_specs,
        out_specs=tile16,
        out_shape=jax.ShapeDtypeStruct(h16.shape, F32),
        compiler_params=pltpu.CompilerParams(
            dimension_semantics=("parallel",), vmem_limit_bytes=VMEM_LIMIT),
        name="ffn_half_step",
    )(*args)


M1_TM = 512
M1_PIECES = 2
M1_CHAINS = 2
N_MASKS = 7


def _seg_scan_rows(x, rowid, width, reverse):
    d = 1
    y = x
    while d < width:
        if reverse:
            shifted = pltpu.roll(y, CHUNKS - d, axis=0)
            ok = (rowid & (width - 1)) < width - d
        else:
            shifted = pltpu.roll(y, d, axis=0)
            ok = (rowid & (width - 1)) >= d
        y = y * jnp.where(ok, shifted, 1.0)
        d *= 2
    return y


def _write_level_masks(mask_ref):
    pr = lax.broadcasted_iota(jnp.int32, (BLOCK, BLOCK), 0)
    pc = lax.broadcasted_iota(jnp.int32, (BLOCK, BLOCK), 1)
    same_chunk = ((pr ^ pc) & (CHUNKS - 1)) == 0
    mask_ref[0] = jnp.where(pr == pc, 1.0, 0.0)
    for beta in range(4):
        sh = 3 + beta + 1
        mask_ref[1 + beta] = jnp.where(same_chunk & ((pr >> sh) == (pc >> sh)), 1.0, 0.0)
    for gamma in range(2):
        sh = gamma + 1
        mask_ref[5 + gamma] = jnp.where(
            ((pr & (CHUNKS - 1)) >> sh) == ((pc & (CHUNKS - 1)) >> sh), 1.0, 0.0)


def _hgrn_head_block(f, q, k, vb, st_ref, h, mask_ref):
    rowid = lax.broadcasted_iota(jnp.int32, (CHUNKS, HEAD_DIM), 0)
    zero = jnp.zeros((CHUNKS, HEAD_DIM), F32)

    def slabs(val):
        return [val[j * CHUNKS:(j + 1) * CHUNKS, :] for j in range(POSITIONS)]

    def stack(sl):
        return jnp.concatenate(sl, axis=0).astype(BF16)

    def transposed(key_slabs):
        return jnp.transpose(jnp.concatenate(key_slabs, axis=0)).astype(BF16)

    def mask_rows(level, j):
        return mask_ref[level, j * CHUNKS:(j + 1) * CHUNKS, :]

    fs, qs, ks = slabs(f), slabs(q), slabs(k)

    p = [jnp.sum(qs[j] * ks[j], axis=-1, keepdims=True) * mask_rows(0, j)
         for j in range(POSITIONS)]
    levels = []
    key_mats = []

    for beta in range(4):
        w = 1 << beta
        a = [None] * POSITIONS
        for j in range(POSITIONS):
            if j & w:
                a[j] = fs[j] if (j & (w - 1)) == 0 else a[j - 1] * fs[j]
        b = [None] * POSITIONS
        for j in range(POSITIONS - 1, -1, -1):
            if not (j & w) and (j & (w - 1)) != w - 1:
                b[j] = fs[j + 1] if b[j + 1] is None else b[j + 1] * fs[j + 1]
        upper = [j for j in range(POSITIONS) if j & w]
        levels.append(([qs[j] * a[j] for j in upper], upper, 1 + beta))
        key_mats.append([zero if (j & w) else (ks[j] if b[j] is None else ks[j] * b[j])
                         for j in range(POSITIONS)])

    gcum = [fs[0]]
    for j in range(1, POSITIONS):
        gcum.append(gcum[-1] * fs[j])
    hsuf = [None] * POSITIONS
    for j in range(POSITIONS - 2, -1, -1):
        hsuf[j] = fs[j + 1] if hsuf[j + 1] is None else hsuf[j + 1] * fs[j + 1]
    g = gcum[-1]
    qg = [qs[j] * gcum[j] for j in range(POSITIONS)]
    kh = [ks[j] if hsuf[j] is None else ks[j] * hsuf[j] for j in range(POSITIONS)]
    g_prev = pltpu.roll(g, 1, axis=0)
    g_next = pltpu.roll(g, CHUNKS - 1, axis=0)

    for gamma in range(3):
        w = 1 << gamma
        low = rowid & (w - 1)
        ca = _seg_scan_rows(jnp.where(low == 0, 1.0, g_prev), rowid, w, False)
        ca = jnp.where((rowid & w) != 0, ca, 0.0)
        cb = _seg_scan_rows(jnp.where(low == w - 1, 1.0, g_next), rowid, w, True)
        cb = jnp.where((rowid & w) == 0, cb, 0.0)
        levels.append(([x * ca for x in qg], list(range(POSITIONS)),
                       5 + gamma if gamma < 2 else None))
        key_mats.append([x * cb for x in kh])

    for (q_slabs, targets, mask_idx), key_slabs in zip(levels, key_mats):
        s = jnp.dot(stack(q_slabs), transposed(key_slabs), preferred_element_type=F32)
        for i, j in enumerate(targets):
            sj = s[i * CHUNKS:(i + 1) * CHUNKS, :]
            p[j] = p[j] + (sj if mask_idx is None else sj * mask_rows(mask_idx, j))

    cp = _seg_scan_rows(jnp.where(rowid == 0, 1.0, g_prev), rowid, CHUNKS, False)
    cs = _seg_scan_rows(jnp.where(rowid == CHUNKS - 1, 1.0, g_next), rowid, CHUNKS, True)
    total = (cp * g)[CHUNKS - 1:CHUNKS, :]
    st = st_ref[h]
    lhs = jnp.concatenate([stack(p), stack([x * cp for x in qg])], axis=1)
    rhs = jnp.concatenate([vb, st.astype(BF16)], axis=0)
    out = jnp.dot(lhs, rhs, preferred_element_type=F32)
    upd = jnp.dot(transposed([x * cs for x in kh]), vb, preferred_element_type=F32)
    decay_col = jnp.transpose(jnp.broadcast_to(total, (HEAD_DIM, HEAD_DIM)))
    st_ref[h] = st * decay_col + upd
    return out


N_LEVELS = 7
LEVEL_TARGETS = (
    [[j for j in range(POSITIONS) if j & (1 << beta)] for beta in range(4)]
    + [list(range(POSITIONS))] * 3)
LEVEL_MASK = (1, 2, 3, 4, 5, 6, None)


def _hgrn_block_staged(f, q, k, vb, st_ref, mask_ref, qs_scr, ks_scr, p_scr, q2_scr, tot_scr,
                       o_ref, orows):
    rowid = lax.broadcasted_iota(jnp.int32, (CHUNKS, HEAD_DIM), 0)
    zero = jnp.zeros((CHUNKS, HEAD_DIM), F32)

    def slabs(val):
        return [val[j * CHUNKS:(j + 1) * CHUNKS, :] for j in range(POSITIONS)]

    def cat(sl):
        return jnp.concatenate(sl, axis=0)

    def mask_rows(level, j):
        return mask_ref[level, j * CHUNKS:(j + 1) * CHUNKS, :]

    for h in range(N_HEADS):
        ls = slice(h * HEAD_DIM, (h + 1) * HEAD_DIM)
        fs, qs, ks = slabs(f[:, ls]), slabs(q[:, ls]), slabs(k[:, ls])
        for j in range(POSITIONS):
            p_scr[h, j * CHUNKS:(j + 1) * CHUNKS, :] = (
                jnp.sum(qs[j] * ks[j], axis=-1, keepdims=True) * mask_rows(0, j))

        for beta in range(4):
            w = 1 << beta
            a = [None] * POSITIONS
            for j in range(POSITIONS):
                if j & w:
                    a[j] = fs[j] if (j & (w - 1)) == 0 else a[j - 1] * fs[j]
            b = [None] * POSITIONS
            for j in range(POSITIONS - 1, -1, -1):
                if not (j & w) and (j & (w - 1)) != w - 1:
                    b[j] = fs[j + 1] if b[j + 1] is None else b[j + 1] * fs[j + 1]
            upper = LEVEL_TARGETS[beta]
            qs_scr[h, beta, 0:len(upper) * CHUNKS, :] = cat(
                [qs[j] * a[j] for j in upper]).astype(BF16)
            ks_scr[h, beta] = cat(
                [zero if (j & w) else (ks[j] if b[j] is None else ks[j] * b[j])
                 for j in range(POSITIONS)])

        gcum = [fs[0]]
        for j in range(1, POSITIONS):
            gcum.append(gcum[-1] * fs[j])
        hsuf = [None] * POSITIONS
        for j in range(POSITIONS - 2, -1, -1):
            hsuf[j] = fs[j + 1] if hsuf[j + 1] is None else hsuf[j + 1] * fs[j + 1]
        g = gcum[-1]
        qg = [qs[j] * gcum[j] for j in range(POSITIONS)]
        kh = [ks[j] if hsuf[j] is None else ks[j] * hsuf[j] for j in range(POSITIONS)]
        g_prev = pltpu.roll(g, 1, axis=0)
        g_next = pltpu.roll(g, CHUNKS - 1, axis=0)

        for gamma in range(3):
            w = 1 << gamma
            low = rowid & (w - 1)
            ca = _seg_scan_rows(jnp.where(low == 0, 1.0, g_prev), rowid, w, False)
            ca = jnp.where((rowid & w) != 0, ca, 0.0)
            cb = _seg_scan_rows(jnp.where(low == w - 1, 1.0, g_next), rowid, w, True)
            cb = jnp.where((rowid & w) == 0, cb, 0.0)
            qs_scr[h, 4 + gamma] = cat([x * ca for x in qg]).astype(BF16)
            ks_scr[h, 4 + gamma] = cat([x * cb for x in kh])

        cp = _seg_scan_rows(jnp.where(rowid == 0, 1.0, g_prev), rowid, CHUNKS, False)
        cs = _seg_scan_rows(jnp.where(rowid == CHUNKS - 1, 1.0, g_next), rowid, CHUNKS, True)
        q2_scr[h] = cat([x * cp for x in qg]).astype(BF16)
        ks_scr[h, N_LEVELS] = cat([x * cs for x in kh])
        tot_scr[h] = jnp.broadcast_to((cp * g)[CHUNKS - 1:CHUNKS, :], (CHUNKS, HEAD_DIM))

    for h in range(N_HEADS):
        for lvl in range(N_LEVELS):
            targets = LEVEL_TARGETS[lvl]
            kt = jnp.transpose(ks_scr[h, lvl]).astype(BF16)
            s = jnp.dot(qs_scr[h, lvl, 0:len(targets) * CHUNKS, :], kt,
                        preferred_element_type=F32)
            for i, j in enumerate(targets):
                rs = slice(j * CHUNKS, (j + 1) * CHUNKS)
                sj = s[i * CHUNKS:(i + 1) * CHUNKS, :]
                if LEVEL_MASK[lvl] is not None:
                    sj = sj * mask_rows(LEVEL_MASK[lvl], j)
                p_scr[h, rs, :] = p_scr[h, rs, :] + sj

    for h in range(N_HEADS):
        ls = slice(h * HEAD_DIM, (h + 1) * HEAD_DIM)
        st = st_ref[h]
        lhs = jnp.concatenate([p_scr[h].astype(BF16), q2_scr[h]], axis=1)
        rhs = jnp.concatenate([vb[:, ls], st.astype(BF16)], axis=0)
        o_ref[orows, ls] = jnp.dot(lhs, rhs, preferred_element_type=F32)
        k2t = jnp.transpose(ks_scr[h, N_LEVELS]).astype(BF16)
        upd = jnp.dot(k2t, vb[:, ls], preferred_element_type=F32)
        decay_col = jnp.transpose(jnp.broadcast_to(tot_scr[h, 0:1, :], (HEAD_DIM, HEAD_DIM)))
        st_ref[h] = st * decay_col + upd


def _mix_in_kernel(x_ref, mod_ref, nw_ref, win_ref, ws_ref, bs_ref, lb_ref,
                   yap_ref, o_ref, sog_ref, sga_ref, sgb_ref,
                   gu_scr, st_ref, mask_ref, qs_scr, ks_scr, p_scr, q2_scr, tot_scr):
    tm = yap_ref.shape[0]
    nblk = tm // BLOCK // M1_CHAINS

    @pl.when(pl.program_id(1) == 0)
    def _():
        st_ref[...] = jnp.zeros_like(st_ref)

    _write_level_masks(mask_ref)
    shift = mod_ref[3:4, :]
    scale = mod_ref[4:5, :]
    row = _block_token(lax.broadcasted_iota(jnp.int32, (BLOCK, BLOCK), 0))
    col = _block_token(lax.broadcasted_iota(jnp.int32, (BLOCK, BLOCK), 1))
    causal = row >= col
    lb = lb_ref[...]

    def project_stage(chain, res):
        b0 = chain * nblk
        rows = slice(b0 * BLOCK, (b0 + nblk) * BLOCK)
        x = _gather_block_rows(x_ref, b0, nblk)
        hb = _rms_modulate(x, nw_ref[...], scale, shift).astype(BF16)

        def project(p0):
            z = jnp.dot(hb, win_ref[:, p0 * D_MODEL:(p0 + M1_PIECES) * D_MODEL],
                        preferred_element_type=F32)
            return [z[:, i * D_MODEL:(i + 1) * D_MODEL] for i in range(M1_PIECES)]

        zu, zv = project(0)
        gu_scr[rows, :] = _gelu_tanh(zu)
        res["gv"] = _gelu_tanh(zv)
        yield
        zq, zf = project(2)
        res["q"] = zq * _sigmoid_t(zq)
        th = 0.5 * jnp.tanh(0.5 * zf)
        res["f"] = lb + (1.0 - lb) * (0.5 + th)
        res["k"] = (1.0 - lb) * (0.5 - th)
        yield
        zi, zog = project(4)
        res["vb"] = zi.astype(BF16)
        sog_ref[rows, :] = _sigmoid_t(zog).astype(BF16)
        yield
        zga, zgb = project(6)
        sga_ref[rows, :] = _sigmoid_t(zga).astype(BF16)
        sgb_ref[rows, :] = _sigmoid_t(zgb).astype(BF16)
        yield

    def gmlp_stage(chain, res):
        b0 = chain * nblk
        gv = res["gv"]
        for g in range(GMLP_GROUPS):
            ls = slice(g * HEAD_DIM, (g + 1) * HEAD_DIM)
            xg = gv[:, ls]
            mu = jnp.mean(xg, axis=-1, keepdims=True)
            xc = xg - mu
            var = jnp.mean(xc * xc, axis=-1, keepdims=True)
            vn = (xc * lax.rsqrt(var + EPS)).astype(BF16)
            rhs = jnp.concatenate(
                [vn[n * BLOCK:(n + 1) * BLOCK, :] for n in range(nblk)], axis=1)
            wc = jnp.where(causal, ws_ref[g], 0.0).astype(BF16)
            mixed = jnp.dot(wc, rhs, preferred_element_type=F32)
            bias = jnp.broadcast_to(bs_ref[:, g:g + 1], (BLOCK, BLOCK))
            for n in range(nblk):
                rs = slice((b0 + n) * BLOCK, (b0 + n + 1) * BLOCK)
                mx = mixed[:, n * BLOCK:(n + 1) * BLOCK] + bias
                yap_ref[rs, ls] = (gu_scr[rs, ls] * mx).astype(BF16)

    def hgrn_stage(chain, res):
        b0 = chain * nblk
        for n in range(nblk):
            br = slice(n * BLOCK, (n + 1) * BLOCK)
            orows = slice((b0 + n) * BLOCK, (b0 + n + 1) * BLOCK)
            _hgrn_block_staged(res["f"][br, :], res["q"][br, :], res["k"][br, :],
                               res["vb"][br, :], st_ref, mask_ref,
                               qs_scr, ks_scr, p_scr, q2_scr, tot_scr, o_ref, orows)

    for chain in range(M1_CHAINS):
        res = {}
        for _ in project_stage(chain, res):
            pass
        gmlp_stage(chain, res)
        hgrn_stage(chain, res)


def _mix_in(h16, mod, nw, w_in, ws_p, bs_p, lb, *, batch, seq):
    n = h16.shape[0] * POSITIONS
    tm = M1_TM
    tiles_per_seq = seq // tm
    tile = pl.BlockSpec((tm, D_MODEL), lambda b, t: (b * tiles_per_seq + t, 0))
    const2 = lambda b, t: (0, 0)
    f32_out = jax.ShapeDtypeStruct((n, D_MODEL), F32)
    bf_out = jax.ShapeDtypeStruct((n, D_MODEL), BF16)
    return pl.pallas_call(
        _mix_in_kernel,
        grid=(batch, tiles_per_seq),
        in_specs=[
            pl.BlockSpec((tm // POSITIONS, CHUNK_WIDTH), lambda b, t: (b * tiles_per_seq + t, 0)),
            pl.BlockSpec((None, N_ADA, D_MODEL), lambda b, t: (b, 0, 0)),
            pl.BlockSpec((1, D_MODEL), const2),
            pl.BlockSpec((D_MODEL, N_PIECES * D_MODEL), const2,
                         pipeline_mode=pl.Buffered(1)),
            pl.BlockSpec((GMLP_GROUPS, BLOCK, BLOCK), lambda b, t: (0, 0, 0)),
            pl.BlockSpec((BLOCK, GMLP_GROUPS), const2),
            pl.BlockSpec((1, D_MODEL), const2),
        ],
        out_specs=[tile] * 5,
        out_shape=[bf_out, f32_out, bf_out, bf_out, bf_out],
        scratch_shapes=[
            pltpu.VMEM((tm, D_MODEL), F32),
            pltpu.VMEM((N_HEADS, HEAD_DIM, HEAD_DIM), F32),
            pltpu.VMEM((N_MASKS, BLOCK, BLOCK), F32),
            pltpu.VMEM((N_HEADS, N_LEVELS, BLOCK, HEAD_DIM), BF16),
            pltpu.VMEM((N_HEADS, N_LEVELS + 1, BLOCK, HEAD_DIM), F32),
            pltpu.VMEM((N_HEADS, BLOCK, BLOCK), F32),
            pltpu.VMEM((N_HEADS, BLOCK, HEAD_DIM), BF16),
            pltpu.VMEM((N_HEADS, CHUNKS, HEAD_DIM), F32),
        ],
        compiler_params=pltpu.CompilerParams(
            dimension_semantics=("parallel", "arbitrary"), vmem_limit_bytes=VMEM_LIMIT),
        name="mixer_in_proj_hgrn2",
    )(h16, mod, nw, w_in, ws_p, bs_p, lb)


M3_TM = 1024
M3_CHAINS = 2


def _mix_out_kernel(x_ref, mod_ref, yap_ref, o_ref, sog_ref, sga_ref, sgb_ref,
                    wa_ref, wb_ref, wo_ref, out_ref):
    nblk = yap_ref.shape[0] // BLOCK // M3_CHAINS
    gate = mod_ref[5:6, :]
    for chain in range(M3_CHAINS):
        b0 = chain * nblk
        rows = slice(b0 * BLOCK, (b0 + nblk) * BLOCK)
        o = o_ref[rows, :]
        parts = []
        for h in range(N_HEADS):
            oh = o[:, h * HEAD_DIM:(h + 1) * HEAD_DIM]
            ms = jnp.mean(oh * oh, axis=-1, keepdims=True)
            parts.append(oh * lax.rsqrt(ms + EPS))
        on = (jnp.concatenate(parts, axis=1) * sog_ref[rows, :].astype(F32)).astype(BF16)
        ya = jnp.dot(yap_ref[rows, :], wa_ref[...], preferred_element_type=F32)
        yb = jnp.dot(on, wb_ref[...], preferred_element_type=F32)
        m = (sga_ref[rows, :].astype(F32) * ya + sgb_ref[rows, :].astype(F32) * yb).astype(BF16)
        out = jnp.dot(m, wo_ref[...], preferred_element_type=F32)
        y = _gather_block_rows(x_ref, b0, nblk) + (1.0 + gate) * out
        _scatter_block_rows(out_ref, y, b0, nblk)


def _mix_out(h16, mod, yap, o, sog, sga, sgb, w_a, w_b, w_o, *, seq):
    n = h16.shape[0] * POSITIONS
    tm = M3_TM
    tiles_per_seq = seq // tm
    tile = pl.BlockSpec((tm, D_MODEL), lambda i: (i, 0))
    tile16 = pl.BlockSpec((tm // POSITIONS, CHUNK_WIDTH), lambda i: (i, 0))
    wspec = pl.BlockSpec((D_MODEL, D_MODEL), lambda i: (0, 0),
                         pipeline_mode=pl.Buffered(1))
    return pl.pallas_call(
        _mix_out_kernel,
        grid=(n // tm,),
        in_specs=[
            tile16,
            pl.BlockSpec((None, N_ADA, D_MODEL), lambda i: (i // tiles_per_seq, 0, 0)),
            tile, tile, tile, tile, tile,
            wspec, wspec, wspec,
        ],
        out_specs=tile16,
        out_shape=jax.ShapeDtypeStruct(h16.shape, F32),
        compiler_params=pltpu.CompilerParams(
            dimension_semantics=("parallel",), vmem_limit_bytes=VMEM_LIMIT),
        name="mixer_out_proj",
    )(h16, mod, yap, o, sog, sga, sgb, w_a, w_b, w_o)


def kernel(x, c, w_in, gmlp_ws, gmlp_b, hgrn_lb_logits, w_branch_a, w_branch_b, w_out,
           ffn_w13, ffn_w2, norm_w, ada_w, ada_b, final_norm_w):
    batch, seq, d = x.shape
    assert d == D_MODEL and seq % M1_TM == 0 and seq % FFN_TM == 0 and seq % M3_TM == 0
    n = batch * seq

    mod_all = _ada_mod(c, ada_w, ada_b).reshape(DEPTH, batch, N_ADA, D_MODEL)
    lb_all = _lower_bounds(hgrn_lb_logits)
    perm = jnp.array([_block_token(p) for p in range(BLOCK)], dtype=jnp.int32)
    ws_p = gmlp_ws[:, :, perm, :][:, :, :, perm]
    bs_p = jnp.swapaxes(gmlp_b[:, :, perm], 1, 2)

    h16 = x.reshape(n // POSITIONS, CHUNK_WIDTH)
    for l in range(DEPTH):
        mod = mod_all[l]
        nw = norm_w[l].reshape(3, 1, D_MODEL)
        h16 = _ffn(h16, mod, nw[0], ffn_w13[l, 0].astype(BF16), ffn_w2[l, 0].astype(BF16),
                   mod_off=0, seq=seq)
        yap, o, sog, sga, sgb = _mix_in(
            h16, mod, nw[1], w_in[l].astype(BF16), ws_p[l], bs_p[l],
            lb_all[l].reshape(1, D_MODEL), batch=batch, seq=seq)
        h16 = _mix_out(h16, mod, yap, o, sog, sga, sgb, w_branch_a[l].astype(BF16),
                       w_branch_b[l].astype(BF16), w_out[l].astype(BF16), seq=seq)
        final_w = final_norm_w.reshape(1, D_MODEL) if l == DEPTH - 1 else None
        h16 = _ffn(h16, mod, nw[2], ffn_w13[l, 1].astype(BF16), ffn_w2[l, 1].astype(BF16),
                   mod_off=6, seq=seq, final_w=final_w)
    return h16.reshape(batch, seq, D_MODEL)
```

```python
import functools

import jax
import jax.numpy as jnp
from jax import lax
from jax.experimental import pallas as pl
from jax.experimental.pallas import tpu as pltpu

F32 = jnp.float32
BF16 = jnp.bfloat16

D_MODEL = 1024
DEPTH = 4
FFN_HIDDEN = 2816
N_ADA = 9
EPS = 1e-6
GMLP_GROUPS = 8
HEAD_DIM = 128
N_HEADS = D_MODEL // HEAD_DIM
N_PIECES = 8

BLOCK = 128
POSITIONS = 16
CHUNKS = BLOCK // POSITIONS
CHUNK_WIDTH = POSITIONS * D_MODEL

VMEM_LIMIT = 56 * 1024 * 1024


def _sigmoid(x):
    return 1.0 / (1.0 + jnp.exp(-x))


def _sigmoid_t(x):
    return 0.5 * jnp.tanh(0.5 * x) + 0.5


def _gelu_tanh(x):
    c = 0.7978845608028654
    return 0.5 * x * (1.0 + jnp.tanh(c * (x + 0.044715 * (x * x * x))))


def _rms_modulate(x, nw, scale, shift):
    ms = jnp.mean(x * x, axis=-1, keepdims=True)
    return (x * lax.rsqrt(ms + EPS) * nw) * (1.0 + scale) + shift


def _block_token(p):
    return (p % CHUNKS) * POSITIONS + p // CHUNKS


def _gather_block_rows(x_ref, b0, n_blocks):
    pieces = [x_ref[n * CHUNKS:(n + 1) * CHUNKS, j * D_MODEL:(j + 1) * D_MODEL]
              for n in range(b0, b0 + n_blocks) for j in range(POSITIONS)]
    return jnp.concatenate(pieces, axis=0)


def _scatter_block_rows(out_ref, y, b0, n_blocks):
    for n in range(n_blocks):
        for j in range(POSITIONS):
            r = (n * POSITIONS + j) * CHUNKS
            out_ref[(b0 + n) * CHUNKS:(b0 + n + 1) * CHUNKS,
                    j * D_MODEL:(j + 1) * D_MODEL] = y[r:r + CHUNKS, :]


def _ada_kernel(c_ref, w_ref, b_ref, o_ref):
    c = c_ref[...]
    ca = c * _sigmoid(c)
    o_ref[...] = jnp.dot(ca, w_ref[...], preferred_element_type=F32,
                         precision=lax.Precision.HIGHEST) + b_ref[...]


def _ada_mod(c, ada_w, ada_b):
    b = c.shape[0]
    return pl.pallas_call(
        _ada_kernel,
        grid=(DEPTH, N_ADA),
        in_specs=[
            pl.BlockSpec((b, D_MODEL), lambda l, j: (0, 0)),
            pl.BlockSpec((None, D_MODEL, D_MODEL), lambda l, j: (l, 0, j)),
            pl.BlockSpec((None, 1, D_MODEL), lambda l, j: (l, 0, j)),
        ],
        out_specs=pl.BlockSpec((None, b, D_MODEL), lambda l, j: (l, 0, j)),
        out_shape=jax.ShapeDtypeStruct((DEPTH, b, N_ADA * D_MODEL), F32),
        name="ada_mod",
    )(c, ada_w, ada_b.reshape(DEPTH, 1, N_ADA * D_MODEL))


def _lb_kernel(x_ref, o_ref):
    x = x_ref[...]
    rows = [x[i:i + 1, :] for i in range(DEPTH)]
    m = rows[0]
    for r in rows[1:]:
        m = jnp.maximum(m, r)
    e = [jnp.exp(r - m) for r in rows]
    tot = e[0]
    for t in e[1:]:
        tot = tot + t
    p = [t / tot for t in e]
    cum = p[0]
    first = cum
    for i in range(DEPTH):
        if i > 0:
            cum = cum + p[i]
        o_ref[i:i + 1, :] = cum - first


def _lower_bounds(logits):
    return pl.pallas_call(
        _lb_kernel,
        out_shape=jax.ShapeDtypeStruct(logits.shape, F32),
        name="hgrn_lower_bounds",
    )(logits.astype(F32))


FFN_TM = 1024
FFN_CHAINS = 2
FFN_COL_CHUNKS = ((0, 1024), (1024, 2048), (2048, FFN_HIDDEN))


def _ffn_kernel(x_ref, mod_ref, nw_ref, w13_ref, w2_ref, *rest, mod_off, final, in_natural):
    if final:
        fnw_ref, o_ref = rest
    else:
        (o_ref,) = rest
    shift = mod_ref[mod_off:mod_off + 1, :]
    scale = mod_ref[mod_off + 1:mod_off + 2, :]
    gate = mod_ref[mod_off + 2:mod_off + 3, :]
    rows = FFN_TM // POSITIONS // FFN_CHAINS
    for r in range(FFN_CHAINS):
        rs = slice(r * rows, (r + 1) * rows)
        if in_natural:
            nat = x_ref[r * rows * POSITIONS:(r + 1) * rows * POSITIONS, :]
            x = pltpu.einshape("(cj)d->(jc)d", nat, j=POSITIONS)
        else:
            x = jnp.concatenate(
                [x_ref[rs, j * D_MODEL:(j + 1) * D_MODEL] for j in range(POSITIONS)], axis=0)
        hb = _rms_modulate(x, nw_ref[...], scale, shift).astype(BF16)
        acc = None
        for c0, c1 in FFN_COL_CHUNKS:
            a = jnp.dot(hb, w13_ref[:, c0:c1], preferred_element_type=F32)
            b = jnp.dot(hb, w13_ref[:, FFN_HIDDEN + c0:FFN_HIDDEN + c1],
                        preferred_element_type=F32)
            act = (a * _sigmoid_t(a) * b).astype(BF16)
            part = jnp.dot(act, w2_ref[c0:c1, :], preferred_element_type=F32)
            acc = part if acc is None else acc + part
        y = x + (0.5 * (1.0 + gate)) * acc
        if final:
            ms = jnp.mean(y * y, axis=-1, keepdims=True)
            y = y * lax.rsqrt(ms + EPS) * fnw_ref[...]
            o_ref[r * rows * POSITIONS:(r + 1) * rows * POSITIONS, :] = pltpu.einshape(
                "(jc)d->(cj)d", y, j=POSITIONS)
        else:
            for j in range(POSITIONS):
                o_ref[rs, j * D_MODEL:(j + 1) * D_MODEL] = y[j * rows:(j + 1) * rows, :]


def _ffn(h_in, mod, nw, w13, w2, *, mod_off, seq, final_w=None, in_natural=False):
    n = h_in.shape[0] if in_natural else h_in.shape[0] * POSITIONS
    tm = FFN_TM
    tiles_per_seq = seq // tm
    final = final_w is not None
    tile16 = pl.BlockSpec((tm // POSITIONS, CHUNK_WIDTH), lambda i: (i, 0))
    in_specs = [
        pl.BlockSpec((tm, D_MODEL), lambda i: (i, 0)) if in_natural else tile16,
        pl.BlockSpec((None, N_ADA, D_MODEL), lambda i: (i // tiles_per_seq, 0, 0)),
        pl.BlockSpec((1, D_MODEL), lambda i: (0, 0)),
        pl.BlockSpec((D_MODEL, 2 * FFN_HIDDEN), lambda i: (0, 0),
                     pipeline_mode=pl.Buffered(1)),
        pl.BlockSpec((FFN_HIDDEN, D_MODEL), lambda i: (0, 0),
                     pipeline_mode=pl.Buffered(1)),
    ]
    args = [h_in, mod, nw, w13, w2]
    if final:
        in_specs.append(pl.BlockSpec((1, D_MODEL), lambda i: (0, 0)))
        args.append(final_w)
    return pl.pallas_call(
        functools.partial(_ffn_kernel, mod_off=mod_off, final=final, in_natural=in_natural),
        grid=(n // tm,),
        in_specs=in_specs,
        out_specs=pl.BlockSpec((tm, D_MODEL), lambda i: (i, 0)) if final else tile16,
        out_shape=jax.ShapeDtypeStruct(
            (n, D_MODEL) if final else (n // POSITIONS, CHUNK_WIDTH), F32),
        compiler_params=pltpu.CompilerParams(
            dimension_semantics=("parallel",), vmem_limit_bytes=VMEM_LIMIT),
        name="ffn_half_step",
    )(*args)


M1_TM = 512
M1_PIECES = 2
M1_CHAINS = 2
N_MASKS = 7


def _seg_scan_rows(x, rowid, width, reverse):
    d = 1
    y = x
    while d < width:
        if reverse:
            shifted = pltpu.roll(y, CHUNKS - d, axis=0)
            ok = (rowid & (width - 1)) < width - d
        else:
            shifted = pltpu.roll(y, d, axis=0)
            ok = (rowid & (width - 1)) >= d
        y = y * jnp.where(ok, shifted, 1.0)
        d *= 2
    return y


def _write_level_masks(mask_ref):
    pr = lax.broadcasted_iota(jnp.int32, (BLOCK, BLOCK), 0)
    pc = lax.broadcasted_iota(jnp.int32, (BLOCK, BLOCK), 1)
    same_chunk = ((pr ^ pc) & (CHUNKS - 1)) == 0
    mask_ref[0] = jnp.where(pr == pc, 1.0, 0.0)
    for beta in range(4):
        sh = 3 + beta + 1
        mask_ref[1 + beta] = jnp.where(same_chunk & ((pr >> sh) == (pc >> sh)), 1.0, 0.0)
    for gamma in range(2):
        sh = gamma + 1
        mask_ref[5 + gamma] = jnp.where(
            ((pr & (CHUNKS - 1)) >> sh) == ((pc & (CHUNKS - 1)) >> sh), 1.0, 0.0)


def _hgrn_head_block(f, q, k, vb, st_ref, h, mask_ref):
    rowid = lax.broadcasted_iota(jnp.int32, (CHUNKS, HEAD_DIM), 0)
    zero = jnp.zeros((CHUNKS, HEAD_DIM), F32)

    def slabs(val):
        return [val[j * CHUNKS:(j + 1) * CHUNKS, :] for j in range(POSITIONS)]

    def stack(sl):
        return jnp.concatenate(sl, axis=0).astype(BF16)

    def transposed(key_slabs):
        return jnp.transpose(jnp.concatenate(key_slabs, axis=0)).astype(BF16)

    def mask_rows(level, j):
        return mask_ref[level, j * CHUNKS:(j + 1) * CHUNKS, :]

    fs, qs, ks = slabs(f), slabs(q), slabs(k)

    p = [jnp.sum(qs[j] * ks[j], axis=-1, keepdims=True) * mask_rows(0, j)
         for j in range(POSITIONS)]
    levels = []
    key_mats = []

    for beta in range(4):
        w = 1 << beta
        a = [None] * POSITIONS
        for j in range(POSITIONS):
            if j & w:
                a[j] = fs[j] if (j & (w - 1)) == 0 else a[j - 1] * fs[j]
        b = [None] * POSITIONS
        for j in range(POSITIONS - 1, -1, -1):
            if not (j & w) and (j & (w - 1)) != w - 1:
                b[j] = fs[j + 1] if b[j + 1] is None else b[j + 1] * fs[j + 1]
        upper = [j for j in range(POSITIONS) if j & w]
        levels.append(([qs[j] * a[j] for j in upper], upper, 1 + beta))
        key_mats.append([zero if (j & w) else (ks[j] if b[j] is None else ks[j] * b[j])
                         for j in range(POSITIONS)])

    gcum = [fs[0]]
    for j in range(1, POSITIONS):
        gcum.append(gcum[-1] * fs[j])
    hsuf = [None] * POSITIONS
    for j in range(POSITIONS - 2, -1, -1):
        hsuf[j] = fs[j + 1] if hsuf[j + 1] is None else hsuf[j + 1] * fs[j + 1]
    g = gcum[-1]
    qg = [qs[j] * gcum[j] for j in range(POSITIONS)]
    kh = [ks[j] if hsuf[j] is None else ks[j] * hsuf[j] for j in range(POSITIONS)]
    g_prev = pltpu.roll(g, 1, axis=0)
    g_next = pltpu.roll(g, CHUNKS - 1, axis=0)

    for gamma in range(3):
        w = 1 << gamma
        low = rowid & (w - 1)
        ca = _seg_scan_rows(jnp.where(low == 0, 1.0, g_prev), rowid, w, False)
        ca = jnp.where((rowid & w) != 0, ca, 0.0)
        cb = _seg_scan_rows(jnp.where(low == w - 1, 1.0, g_next), rowid, w, True)
        cb = jnp.where((rowid & w) == 0, cb, 0.0)
        levels.append(([x * ca for x in qg], list(range(POSITIONS)),
                       5 + gamma if gamma < 2 else None))
        key_mats.append([x * cb for x in kh])

    for (q_slabs, targets, mask_idx), key_slabs in zip(levels, key_mats):
        s = jnp.dot(stack(q_slabs), transposed(key_slabs), preferred_element_type=F32)
        for i, j in enumerate(targets):
            sj = s[i * CHUNKS:(i + 1) * CHUNKS, :]
            p[j] = p[j] + (sj if mask_idx is None else sj * mask_rows(mask_idx, j))

    cp = _seg_scan_rows(jnp.where(rowid == 0, 1.0, g_prev), rowid, CHUNKS, False)
    cs = _seg_scan_rows(jnp.where(rowid == CHUNKS - 1, 1.0, g_next), rowid, CHUNKS, True)
    total = (cp * g)[CHUNKS - 1:CHUNKS, :]
    st = st_ref[h]
    lhs = jnp.concatenate([stack(p), stack([x * cp for x in qg])], axis=1)
    rhs = jnp.concatenate([vb, st.astype(BF16)], axis=0)
    out = jnp.dot(lhs, rhs, preferred_element_type=F32)
    upd = jnp.dot(transposed([x * cs for x in kh]), vb, preferred_element_type=F32)
    decay_col = jnp.transpose(jnp.broadcast_to(total, (HEAD_DIM, HEAD_DIM)))
    st_ref[h] = st * decay_col + upd
    return out


N_LEVELS = 7
LEVEL_TARGETS = (
    [[j for j in range(POSITIONS) if j & (1 << beta)] for beta in range(4)]
    + [list(range(POSITIONS))] * 3)
LEVEL_MASK = (1, 2, 3, 4, 5, 6, None)


def _hgrn_block_staged(f, q, k, vb, st_ref, mask_ref, qs_scr, ks_scr, p_scr, q2_scr, tot_scr,
                       o_ref, orows):
    rowid = lax.broadcasted_iota(jnp.int32, (CHUNKS, HEAD_DIM), 0)
    zero = jnp.zeros((CHUNKS, HEAD_DIM), F32)

    def slabs(val):
        return [val[j * CHUNKS:(j + 1) * CHUNKS, :] for j in range(POSITIONS)]

    def cat(sl):
        return jnp.concatenate(sl, axis=0)

    def mask_rows(level, j):
        return mask_ref[level, j * CHUNKS:(j + 1) * CHUNKS, :]

    for h in range(N_HEADS):
        ls = slice(h * HEAD_DIM, (h + 1) * HEAD_DIM)
        fs, qs, ks = slabs(f[:, ls]), slabs(q[:, ls]), slabs(k[:, ls])
        for j in range(POSITIONS):
            p_scr[h, j * CHUNKS:(j + 1) * CHUNKS, :] = (
                jnp.sum(qs[j] * ks[j], axis=-1, keepdims=True) * mask_rows(0, j))

        for beta in range(4):
            w = 1 << beta
            a = [None] * POSITIONS
            for j in range(POSITIONS):
                if j & w:
                    a[j] = fs[j] if (j & (w - 1)) == 0 else a[j - 1] * fs[j]
            b = [None] * POSITIONS
            for j in range(POSITIONS - 1, -1, -1):
                if not (j & w) and (j & (w - 1)) != w - 1:
                    b[j] = fs[j + 1] if b[j + 1] is None else b[j + 1] * fs[j + 1]
            upper = LEVEL_TARGETS[beta]
            qs_scr[h, beta, 0:len(upper) * CHUNKS, :] = cat(
                [qs[j] * a[j] for j in upper]).astype(BF16)
            ks_scr[h, beta] = cat(
                [zero if (j & w) else (ks[j] if b[j] is None else ks[j] * b[j])
                 for j in range(POSITIONS)])

        gcum = [fs[0]]
        for j in range(1, POSITIONS):
            gcum.append(gcum[-1] * fs[j])
        hsuf = [None] * POSITIONS
        for j in range(POSITIONS - 2, -1, -1):
            hsuf[j] = fs[j + 1] if hsuf[j + 1] is None else hsuf[j + 1] * fs[j + 1]
        g = gcum[-1]
        qg = [qs[j] * gcum[j] for j in range(POSITIONS)]
        kh = [ks[j] if hsuf[j] is None else ks[j] * hsuf[j] for j in range(POSITIONS)]
        g_prev = pltpu.roll(g, 1, axis=0)
        g_next = pltpu.roll(g, CHUNKS - 1, axis=0)

        for gamma in range(3):
            w = 1 << gamma
            low = rowid & (w - 1)
            ca = _seg_scan_rows(jnp.where(low == 0, 1.0, g_prev), rowid, w, False)
            ca = jnp.where((rowid & w) != 0, ca, 0.0)
            cb = _seg_scan_rows(jnp.where(low == w - 1, 1.0, g_next), rowid, w, True)
            cb = jnp.where((rowid & w) == 0, cb, 0.0)
            qs_scr[h, 4 + gamma] = cat([x * ca for x in qg]).astype(BF16)
            ks_scr[h, 4 + gamma] = cat([x * cb for x in kh])

        cp = _seg_scan_rows(jnp.where(rowid == 0, 1.0, g_prev), rowid, CHUNKS, False)
        cs = _seg_scan_rows(jnp.where(rowid == CHUNKS - 1, 1.0, g_next), rowid, CHUNKS, True)
        q2_scr[h] = cat([x * cp for x in qg]).astype(BF16)
        ks_scr[h, N_LEVELS] = cat([x * cs for x in kh])
        tot_scr[h] = jnp.broadcast_to((cp * g)[CHUNKS - 1:CHUNKS, :], (CHUNKS, HEAD_DIM))

    for h in range(N_HEADS):
        for lvl in range(N_LEVELS):
            targets = LEVEL_TARGETS[lvl]
            kt = jnp.transpose(ks_scr[h, lvl]).astype(BF16)
            s = jnp.dot(qs_scr[h, lvl, 0:len(targets) * CHUNKS, :], kt,
                        preferred_element_type=F32)
            for i, j in enumerate(targets):
                rs = slice(j * CHUNKS, (j + 1) * CHUNKS)
                sj = s[i * CHUNKS:(i + 1) * CHUNKS, :]
                if LEVEL_MASK[lvl] is not None:
                    sj = sj * mask_rows(LEVEL_MASK[lvl], j)
                p_scr[h, rs, :] = p_scr[h, rs, :] + sj

    for h in range(N_HEADS):
        ls = slice(h * HEAD_DIM, (h + 1) * HEAD_DIM)
        st = st_ref[h]
        lhs = jnp.concatenate([p_scr[h].astype(BF16), q2_scr[h]], axis=1)
        rhs = jnp.concatenate([vb[:, ls], st.astype(BF16)], axis=0)
        o_ref[orows, ls] = jnp.dot(lhs, rhs, preferred_element_type=F32)
        k2t = jnp.transpose(ks_scr[h, N_LEVELS]).astype(BF16)
        upd = jnp.dot(k2t, vb[:, ls], preferred_element_type=F32)
        decay_col = jnp.transpose(jnp.broadcast_to(tot_scr[h, 0:1, :], (HEAD_DIM, HEAD_DIM)))
        st_ref[h] = st * decay_col + upd


def _mix_in_kernel(x_ref, mod_ref, nw_ref, win_ref, ws_ref, bs_ref, lb_ref,
                   yap_ref, o_ref, sog_ref, sga_ref, sgb_ref,
                   gu_scr, st_ref, mask_ref, qs_scr, ks_scr, p_scr, q2_scr, tot_scr):
    tm = yap_ref.shape[0]
    nblk = tm // BLOCK // M1_CHAINS

    @pl.when(pl.program_id(1) == 0)
    def _():
        st_ref[...] = jnp.zeros_like(st_ref)

    _write_level_masks(mask_ref)
    shift = mod_ref[3:4, :]
    scale = mod_ref[4:5, :]
    row = _block_token(lax.broadcasted_iota(jnp.int32, (BLOCK, BLOCK), 0))
    col = _block_token(lax.broadcasted_iota(jnp.int32, (BLOCK, BLOCK), 1))
    causal = row >= col
    lb = lb_ref[...]

    def project_stage(chain, res):
        b0 = chain * nblk
        rows = slice(b0 * BLOCK, (b0 + nblk) * BLOCK)
        x = _gather_block_rows(x_ref, b0, nblk)
        hb = _rms_modulate(x, nw_ref[...], scale, shift).astype(BF16)

        def project(p0):
            z = jnp.dot(hb, win_ref[:, p0 * D_MODEL:(p0 + M1_PIECES) * D_MODEL],
                        preferred_element_type=F32)
            return [z[:, i * D_MODEL:(i + 1) * D_MODEL] for i in range(M1_PIECES)]

        zu, zv = project(0)
        gu_scr[rows, :] = _gelu_tanh(zu)
        res["gv"] = _gelu_tanh(zv)
        yield
        zq, zf = project(2)
        res["q"] = zq * _sigmoid_t(zq)
        th = 0.5 * jnp.tanh(0.5 * zf)
        res["f"] = lb + (1.0 - lb) * (0.5 + th)
        res["k"] = (1.0 - lb) * (0.5 - th)
        yield
        zi, zog = project(4)
        res["vb"] = zi.astype(BF16)
        sog_ref[rows, :] = _sigmoid_t(zog).astype(BF16)
        yield
        zga, zgb = project(6)
        sga_ref[rows, :] = _sigmoid_t(zga).astype(BF16)
        sgb_ref[rows, :] = _sigmoid_t(zgb).astype(BF16)
        yield

    def gmlp_stage(chain, res):
        b0 = chain * nblk
        gv = res["gv"]
        for g in range(GMLP_GROUPS):
            ls = slice(g * HEAD_DIM, (g + 1) * HEAD_DIM)
            xg = gv[:, ls]
            mu = jnp.mean(xg, axis=-1, keepdims=True)
            xc = xg - mu
            var = jnp.mean(xc * xc, axis=-1, keepdims=True)
            vn = (xc * lax.rsqrt(var + EPS)).astype(BF16)
            rhs = jnp.concatenate(
                [vn[n * BLOCK:(n + 1) * BLOCK, :] for n in range(nblk)], axis=1)
            wc = jnp.where(causal, ws_ref[g], 0.0).astype(BF16)
            mixed = jnp.dot(wc, rhs, preferred_element_type=F32)
            bias = jnp.broadcast_to(bs_ref[:, g:g + 1], (BLOCK, BLOCK))
            for n in range(nblk):
                rs = slice((b0 + n) * BLOCK, (b0 + n + 1) * BLOCK)
                mx = mixed[:, n * BLOCK:(n + 1) * BLOCK] + bias
                yap_ref[rs, ls] = (gu_scr[rs, ls] * mx).astype(BF16)

    def hgrn_stage(chain, res):
        b0 = chain * nblk
        for n in range(nblk):
            br = slice(n * BLOCK, (n + 1) * BLOCK)
            orows = slice((b0 + n) * BLOCK, (b0 + n + 1) * BLOCK)
            _hgrn_block_staged(res["f"][br, :], res["q"][br, :], res["k"][br, :],
                               res["vb"][br, :], st_ref, mask_ref,
                               qs_scr, ks_scr, p_scr, q2_scr, tot_scr, o_ref, orows)

    for chain in range(M1_CHAINS):
        res = {}
        for _ in project_stage(chain, res):
            pass
        gmlp_stage(chain, res)
        hgrn_stage(chain, res)


def _mix_in(h16, mod, nw, w_in, ws_p, bs_p, lb, *, batch, seq):
    n = h16.shape[0] * POSITIONS
    tm = M1_TM
    tiles_per_seq = seq // tm
    tile = pl.BlockSpec((tm, D_MODEL), lambda b, t: (b * tiles_per_seq + t, 0))
    const2 = lambda b, t: (0, 0)
    f32_out = jax.ShapeDtypeStruct((n, D_MODEL), F32)
    bf_out = jax.ShapeDtypeStruct((n, D_MODEL), BF16)
    return pl.pallas_call(
        _mix_in_kernel,
        grid=(batch, tiles_per_seq),
        in_specs=[
            pl.BlockSpec((tm // POSITIONS, CHUNK_WIDTH), lambda b, t: (b * tiles_per_seq + t, 0)),
            pl.BlockSpec((None, N_ADA, D_MODEL), lambda b, t: (b, 0, 0)),
            pl.BlockSpec((1, D_MODEL), const2),
            pl.BlockSpec((D_MODEL, N_PIECES * D_MODEL), const2,
                         pipeline_mode=pl.Buffered(1)),
            pl.BlockSpec((GMLP_GROUPS, BLOCK, BLOCK), lambda b, t: (0, 0, 0)),
            pl.BlockSpec((BLOCK, GMLP_GROUPS), const2),
            pl.BlockSpec((1, D_MODEL), const2),
        ],
        out_specs=[tile] * 5,
        out_shape=[bf_out, f32_out, bf_out, bf_out, bf_out],
        scratch_shapes=[
            pltpu.VMEM((tm, D_MODEL), F32),
            pltpu.VMEM((N_HEADS, HEAD_DIM, HEAD_DIM), F32),
            pltpu.VMEM((N_MASKS, BLOCK, BLOCK), F32),
            pltpu.VMEM((N_HEADS, N_LEVELS, BLOCK, HEAD_DIM), BF16),
            pltpu.VMEM((N_HEADS, N_LEVELS + 1, BLOCK, HEAD_DIM), F32),
            pltpu.VMEM((N_HEADS, BLOCK, BLOCK), F32),
            pltpu.VMEM((N_HEADS, BLOCK, HEAD_DIM), BF16),
            pltpu.VMEM((N_HEADS, CHUNKS, HEAD_DIM), F32),
        ],
        compiler_params=pltpu.CompilerParams(
            dimension_semantics=("parallel", "arbitrary"), vmem_limit_bytes=VMEM_LIMIT),
        name="mixer_in_proj_hgrn2",
    )(h16, mod, nw, w_in, ws_p, bs_p, lb)


M3_TM = 1024
M3_CHAINS = 2


def _mix_out_kernel(x_ref, mod_ref, yap_ref, o_ref, sog_ref, sga_ref, sgb_ref,
                    wa_ref, wb_ref, wo_ref, out_ref):
    nblk = yap_ref.shape[0] // BLOCK // M3_CHAINS
    gate = mod_ref[5:6, :]
    for chain in range(M3_CHAINS):
        b0 = chain * nblk
        rows = slice(b0 * BLOCK, (b0 + nblk) * BLOCK)
        o = o_ref[rows, :]
        parts = []
        for h in range(N_HEADS):
            oh = o[:, h * HEAD_DIM:(h + 1) * HEAD_DIM]
            ms = jnp.mean(oh * oh, axis=-1, keepdims=True)
            parts.append(oh * lax.rsqrt(ms + EPS))
        on = (jnp.concatenate(parts, axis=1) * sog_ref[rows, :].astype(F32)).astype(BF16)
        ya = jnp.dot(yap_ref[rows, :], wa_ref[...], preferred_element_type=F32)
        yb = jnp.dot(on, wb_ref[...], preferred_element_type=F32)
        m = (sga_ref[rows, :].astype(F32) * ya + sgb_ref[rows, :].astype(F32) * yb).astype(BF16)
        out = jnp.dot(m, wo_ref[...], preferred_element_type=F32)
        y = _gather_block_rows(x_ref, b0, nblk) + (1.0 + gate) * out
        _scatter_block_rows(out_ref, y, b0, nblk)


def _mix_out(h16, mod, yap, o, sog, sga, sgb, w_a, w_b, w_o, *, seq):
    n = h16.shape[0] * POSITIONS
    tm = M3_TM
    tiles_per_seq = seq // tm
    tile = pl.BlockSpec((tm, D_MODEL), lambda i: (i, 0))
    tile16 = pl.BlockSpec((tm // POSITIONS, CHUNK_WIDTH), lambda i: (i, 0))
    wspec = pl.BlockSpec((D_MODEL, D_MODEL), lambda i: (0, 0),
                         pipeline_mode=pl.Buffered(1))
    return pl.pallas_call(
        _mix_out_kernel,
        grid=(n // tm,),
        in_specs=[
            tile16,
            pl.BlockSpec((None, N_ADA, D_MODEL), lambda i: (i // tiles_per_seq, 0, 0)),
            tile, tile, tile, tile, tile,
            wspec, wspec, wspec,
        ],
        out_specs=tile16,
        out_shape=jax.ShapeDtypeStruct(h16.shape, F32),
        compiler_params=pltpu.CompilerParams(
            dimension_semantics=("parallel",), vmem_limit_bytes=VMEM_LIMIT),
        name="mixer_out_proj",
    )(h16, mod, yap, o, sog, sga, sgb, w_a, w_b, w_o)


def kernel(x, c, w_in, gmlp_ws, gmlp_b, hgrn_lb_logits, w_branch_a, w_branch_b, w_out,
           ffn_w13, ffn_w2, norm_w, ada_w, ada_b, final_norm_w):
    batch, seq, d = x.shape
    assert d == D_MODEL and seq % M1_TM == 0 and seq % FFN_TM == 0 and seq % M3_TM == 0
    n = batch * seq

    mod_all = _ada_mod(c, ada_w, ada_b).reshape(DEPTH, batch, N_ADA, D_MODEL)
    lb_all = _lower_bounds(hgrn_lb_logits)
    perm = jnp.array([_block_token(p) for p in range(BLOCK)], dtype=jnp.int32)
    ws_p = gmlp_ws[:, :, perm, :][:, :, :, perm]
    bs_p = jnp.swapaxes(gmlp_b[:, :, perm], 1, 2)

    h16 = x.reshape(n, D_MODEL)
    for l in range(DEPTH):
        mod = mod_all[l]
        nw = norm_w[l].reshape(3, 1, D_MODEL)
        h16 = _ffn(h16, mod, nw[0], ffn_w13[l, 0].astype(BF16), ffn_w2[l, 0].astype(BF16),
                   mod_off=0, seq=seq, in_natural=(l == 0))
        yap, o, sog, sga, sgb = _mix_in(
            h16, mod, nw[1], w_in[l].astype(BF16), ws_p[l], bs_p[l],
            lb_all[l].reshape(1, D_MODEL), batch=batch, seq=seq)
        h16 = _mix_out(h16, mod, yap, o, sog, sga, sgb, w_branch_a[l].astype(BF16),
                       w_branch_b[l].astype(BF16), w_out[l].astype(BF16), seq=seq)
        final_w = final_norm_w.reshape(1, D_MODEL) if l == DEPTH - 1 else None
        h16 = _ffn(h16, mod, nw[2], ffn_w13[l, 1].astype(BF16), ffn_w2[l, 1].astype(BF16),
                   mod_off=6, seq=seq, final_w=final_w)
    return h16.reshape(batch, seq, D_MODEL)
```

```python
import functools

import jax
import jax.numpy as jnp
from jax import lax
from jax.experimental import pallas as pl
from jax.experimental.pallas import tpu as pltpu

F32 = jnp.float32
BF16 = jnp.bfloat16

D_MODEL = 1024
DEPTH = 4
FFN_HIDDEN = 2816
N_ADA = 9
EPS = 1e-6
GMLP_GROUPS = 8
HEAD_DIM = 128
N_HEADS = D_MODEL // HEAD_DIM
N_PIECES = 8

BLOCK = 128
POSITIONS = 16
CHUNKS = BLOCK // POSITIONS
CHUNK_WIDTH = POSITIONS * D_MODEL

VMEM_LIMIT = 56 * 1024 * 1024


def _sigmoid(x):
    return 1.0 / (1.0 + jnp.exp(-x))


def _sigmoid_t(x):
    return 0.5 * jnp.tanh(0.5 * x) + 0.5


def _gelu_tanh(x):
    c = 0.7978845608028654
    return 0.5 * x * (1.0 + jnp.tanh(c * (x + 0.044715 * (x * x * x))))


def _rms_modulate(x, nw, scale, shift):
    ms = jnp.mean(x * x, axis=-1, keepdims=True)
    return (x * lax.rsqrt(ms + EPS) * nw) * (1.0 + scale) + shift


def _block_token(p):
    return (p % CHUNKS) * POSITIONS + p // CHUNKS


def _gather_block_rows(x_ref, b0, n_blocks):
    pieces = [x_ref[n * CHUNKS:(n + 1) * CHUNKS, j * D_MODEL:(j + 1) * D_MODEL]
              for n in range(b0, b0 + n_blocks) for j in range(POSITIONS)]
    return jnp.concatenate(pieces, axis=0)


def _scatter_block_rows(out_ref, y, b0, n_blocks):
    for n in range(n_blocks):
        for j in range(POSITIONS):
            r = (n * POSITIONS + j) * CHUNKS
            out_ref[(b0 + n) * CHUNKS:(b0 + n + 1) * CHUNKS,
                    j * D_MODEL:(j + 1) * D_MODEL] = y[r:r + CHUNKS, :]


def _ada_kernel(c_ref, w_ref, b_ref, o_ref):
    c = c_ref[...]
    ca = c * _sigmoid(c)
    o_ref[...] = jnp.dot(ca, w_ref[...], preferred_element_type=F32,
                         precision=lax.Precision.HIGHEST) + b_ref[...]


def _ada_mod(c, ada_w, ada_b):
    b = c.shape[0]
    return pl.pallas_call(
        _ada_kernel,
        grid=(DEPTH, N_ADA),
        in_specs=[
            pl.BlockSpec((b, D_MODEL), lambda l, j: (0, 0)),
            pl.BlockSpec((None, D_MODEL, D_MODEL), lambda l, j: (l, 0, j)),
            pl.BlockSpec((None, 1, D_MODEL), lambda l, j: (l, 0, j)),
        ],
        out_specs=pl.BlockSpec((None, b, D_MODEL), lambda l, j: (l, 0, j)),
        out_shape=jax.ShapeDtypeStruct((DEPTH, b, N_ADA * D_MODEL), F32),
        name="ada_mod",
    )(c, ada_w, ada_b.reshape(DEPTH, 1, N_ADA * D_MODEL))


def _lb_kernel(x_ref, o_ref):
    x = x_ref[...]
    rows = [x[i:i + 1, :] for i in range(DEPTH)]
    m = rows[0]
    for r in rows[1:]:
        m = jnp.maximum(m, r)
    e = [jnp.exp(r - m) for r in rows]
    tot = e[0]
    for t in e[1:]:
        tot = tot + t
    p = [t / tot for t in e]
    cum = p[0]
    first = cum
    for i in range(DEPTH):
        if i > 0:
            cum = cum + p[i]
        o_ref[i:i + 1, :] = cum - first


def _lower_bounds(logits):
    return pl.pallas_call(
        _lb_kernel,
        out_shape=jax.ShapeDtypeStruct(logits.shape, F32),
        name="hgrn_lower_bounds",
    )(logits.astype(F32))


FFN_TM = 1024
FFN_CHAINS = 2
FFN_COL_CHUNKS = ((0, 1024), (1024, 2048), (2048, FFN_HIDDEN))


def _ffn_kernel(x_ref, mod_ref, nw_ref, w13_ref, w2_ref, *rest, mod_off, final, in_natural):
    if final:
        fnw_ref, o_ref = rest
    else:
        (o_ref,) = rest
    shift = mod_ref[mod_off:mod_off + 1, :]
    scale = mod_ref[mod_off + 1:mod_off + 2, :]
    gate = mod_ref[mod_off + 2:mod_off + 3, :]
    rows = FFN_TM // POSITIONS // FFN_CHAINS
    for r in range(FFN_CHAINS):
        rs = slice(r * rows, (r + 1) * rows)
        if in_natural:
            nat = x_ref[r * rows * POSITIONS:(r + 1) * rows * POSITIONS, :]
            x = pltpu.einshape("(cj)d->(jc)d", nat, j=POSITIONS)
        else:
            x = jnp.concatenate(
                [x_ref[rs, j * D_MODEL:(j + 1) * D_MODEL] for j in range(POSITIONS)], axis=0)
        hb = _rms_modulate(x, nw_ref[...], scale, shift).astype(BF16)
        acc = None
        for c0, c1 in FFN_COL_CHUNKS:
            a = jnp.dot(hb, w13_ref[:, c0:c1], preferred_element_type=F32)
            b = jnp.dot(hb, w13_ref[:, FFN_HIDDEN + c0:FFN_HIDDEN + c1],
                        preferred_element_type=F32)
            act = (a * _sigmoid_t(a) * b).astype(BF16)
            part = jnp.dot(act, w2_ref[c0:c1, :], preferred_element_type=F32)
            acc = part if acc is None else acc + part
        y = x + (0.5 * (1.0 + gate)) * acc
        if final:
            ms = jnp.mean(y * y, axis=-1, keepdims=True)
            y = y * lax.rsqrt(ms + EPS) * fnw_ref[...]
            o_ref[r * rows * POSITIONS:(r + 1) * rows * POSITIONS, :] = pltpu.einshape(
                "(jc)d->(cj)d", y, j=POSITIONS)
        else:
            for j in range(POSITIONS):
                o_ref[rs, j * D_MODEL:(j + 1) * D_MODEL] = y[j * rows:(j + 1) * rows, :]


def _ffn(h_in, mod, nw, w13, w2, *, mod_off, seq, final_w=None, in_natural=False):
    n = h_in.shape[0] if in_natural else h_in.shape[0] * POSITIONS
    tm = FFN_TM
    tiles_per_seq = seq // tm
    final = final_w is not None
    tile16 = pl.BlockSpec((tm // POSITIONS, CHUNK_WIDTH), lambda i: (i, 0))
    in_specs = [
        pl.BlockSpec((tm, D_MODEL), lambda i: (i, 0)) if in_natural else tile16,
        pl.BlockSpec((None, N_ADA, D_MODEL), lambda i: (i // tiles_per_seq, 0, 0)),
        pl.BlockSpec((1, D_MODEL), lambda i: (0, 0)),
        pl.BlockSpec((D_MODEL, 2 * FFN_HIDDEN), lambda i: (0, 0),
                     pipeline_mode=pl.Buffered(1)),
        pl.BlockSpec((FFN_HIDDEN, D_MODEL), lambda i: (0, 0),
                     pipeline_mode=pl.Buffered(1)),
    ]
    args = [h_in, mod, nw, w13, w2]
    if final:
        in_specs.append(pl.BlockSpec((1, D_MODEL), lambda i: (0, 0)))
        args.append(final_w)
    return pl.pallas_call(
        functools.partial(_ffn_kernel, mod_off=mod_off, final=final, in_natural=in_natural),
        grid=(n // tm,),
        in_specs=in_specs,
        out_specs=pl.BlockSpec((tm, D_MODEL), lambda i: (i, 0)) if final else tile16,
        out_shape=jax.ShapeDtypeStruct(
            (n, D_MODEL) if final else (n // POSITIONS, CHUNK_WIDTH), F32),
        compiler_params=pltpu.CompilerParams(
            dimension_semantics=("parallel",), vmem_limit_bytes=VMEM_LIMIT),
        name="ffn_half_step",
    )(*args)


M1_TM = 512
M1_PIECES = 2
M1_CHAINS = 2
N_MASKS = 7


def _seg_scan_rows(x, rowid, width, reverse):
    d = 1
    y = x
    while d < width:
        if reverse:
            shifted = pltpu.roll(y, CHUNKS - d, axis=0)
            ok = (rowid & (width - 1)) < width - d
        else:
            shifted = pltpu.roll(y, d, axis=0)
            ok = (rowid & (width - 1)) >= d
        y = y * jnp.where(ok, shifted, 1.0)
        d *= 2
    return y


def _write_level_masks(mask_ref):
    pr = lax.broadcasted_iota(jnp.int32, (BLOCK, BLOCK), 0)
    pc = lax.broadcasted_iota(jnp.int32, (BLOCK, BLOCK), 1)
    same_chunk = ((pr ^ pc) & (CHUNKS - 1)) == 0
    mask_ref[0] = jnp.where(pr == pc, 1.0, 0.0)
    for beta in range(4):
        sh = 3 + beta + 1
        mask_ref[1 + beta] = jnp.where(same_chunk & ((pr >> sh) == (pc >> sh)), 1.0, 0.0)
    for gamma in range(2):
        sh = gamma + 1
        mask_ref[5 + gamma] = jnp.where(
            ((pr & (CHUNKS - 1)) >> sh) == ((pc & (CHUNKS - 1)) >> sh), 1.0, 0.0)


N_LEVELS = 7
HGRN_STAGED_BLOCKS = 1
LEVEL_TARGETS = (
    [[j for j in range(POSITIONS) if j & (1 << beta)] for beta in range(4)]
    + [list(range(POSITIONS))] * 3)
LEVEL_MASK = (1, 2, 3, 4, 5, 6, None)


def _hgrn_staged(blocks, st_ref, mask_ref, qs_scr, ks_scr, p_scr, q2_scr, tot_scr, o_ref):
    n_units = len(blocks) * N_HEADS
    rowid = lax.broadcasted_iota(jnp.int32, (CHUNKS, HEAD_DIM), 0)
    zero = jnp.zeros((CHUNKS, HEAD_DIM), F32)

    def slabs(val):
        return [val[j * CHUNKS:(j + 1) * CHUNKS, :] for j in range(POSITIONS)]

    def cat(sl):
        return jnp.concatenate(sl, axis=0)

    def mask_rows(level, j):
        return mask_ref[level, j * CHUNKS:(j + 1) * CHUNKS, :]

    for u in range(n_units):
        h = u % N_HEADS
        f, q, k, _, _ = blocks[u // N_HEADS]
        ls = slice(h * HEAD_DIM, (h + 1) * HEAD_DIM)
        fs, qs, ks = slabs(f[:, ls]), slabs(q[:, ls]), slabs(k[:, ls])
        for j in range(POSITIONS):
            p_scr[u, j * CHUNKS:(j + 1) * CHUNKS, :] = (
                jnp.sum(qs[j] * ks[j], axis=-1, keepdims=True) * mask_rows(0, j))

        for beta in range(4):
            w = 1 << beta
            a = [None] * POSITIONS
            for j in range(POSITIONS):
                if j & w:
                    a[j] = fs[j] if (j & (w - 1)) == 0 else a[j - 1] * fs[j]
            b = [None] * POSITIONS
            for j in range(POSITIONS - 1, -1, -1):
                if not (j & w) and (j & (w - 1)) != w - 1:
                    b[j] = fs[j + 1] if b[j + 1] is None else b[j + 1] * fs[j + 1]
            upper = LEVEL_TARGETS[beta]
            qs_scr[u, beta, 0:len(upper) * CHUNKS, :] = cat(
                [qs[j] * a[j] for j in upper]).astype(BF16)
            ks_scr[u, beta] = cat(
                [zero if (j & w) else (ks[j] if b[j] is None else ks[j] * b[j])
                 for j in range(POSITIONS)])

        gcum = [fs[0]]
        for j in range(1, POSITIONS):
            gcum.append(gcum[-1] * fs[j])
        hsuf = [None] * POSITIONS
        for j in range(POSITIONS - 2, -1, -1):
            hsuf[j] = fs[j + 1] if hsuf[j + 1] is None else hsuf[j + 1] * fs[j + 1]
        g = gcum[-1]
        qg = [qs[j] * gcum[j] for j in range(POSITIONS)]
        kh = [ks[j] if hsuf[j] is None else ks[j] * hsuf[j] for j in range(POSITIONS)]
        g_prev = pltpu.roll(g, 1, axis=0)
        g_next = pltpu.roll(g, CHUNKS - 1, axis=0)

        for gamma in range(3):
            w = 1 << gamma
            low = rowid & (w - 1)
            ca = _seg_scan_rows(jnp.where(low == 0, 1.0, g_prev), rowid, w, False)
            ca = jnp.where((rowid & w) != 0, ca, 0.0)
            cb = _seg_scan_rows(jnp.where(low == w - 1, 1.0, g_next), rowid, w, True)
            cb = jnp.where((rowid & w) == 0, cb, 0.0)
            qs_scr[u, 4 + gamma] = cat([x * ca for x in qg]).astype(BF16)
            ks_scr[u, 4 + gamma] = cat([x * cb for x in kh])

        cp = _seg_scan_rows(jnp.where(rowid == 0, 1.0, g_prev), rowid, CHUNKS, False)
        cs = _seg_scan_rows(jnp.where(rowid == CHUNKS - 1, 1.0, g_next), rowid, CHUNKS, True)
        q2_scr[u] = cat([x * cp for x in qg]).astype(BF16)
        ks_scr[u, N_LEVELS] = cat([x * cs for x in kh])
        tot_scr[u] = jnp.broadcast_to((cp * g)[CHUNKS - 1:CHUNKS, :], (CHUNKS, HEAD_DIM))

    for u in range(n_units):
        for lvl in range(N_LEVELS):
            targets = LEVEL_TARGETS[lvl]
            kt = jnp.transpose(ks_scr[u, lvl]).astype(BF16)
            s = jnp.dot(qs_scr[u, lvl, 0:len(targets) * CHUNKS, :], kt,
                        preferred_element_type=F32)
            for i, j in enumerate(targets):
                rs = slice(j * CHUNKS, (j + 1) * CHUNKS)
                sj = s[i * CHUNKS:(i + 1) * CHUNKS, :]
                if LEVEL_MASK[lvl] is not None:
                    sj = sj * mask_rows(LEVEL_MASK[lvl], j)
                p_scr[u, rs, :] = p_scr[u, rs, :] + sj

    for u in range(n_units):
        h = u % N_HEADS
        _, _, _, vb, orows = blocks[u // N_HEADS]
        ls = slice(h * HEAD_DIM, (h + 1) * HEAD_DIM)
        st = st_ref[h]
        lhs = jnp.concatenate([p_scr[u].astype(BF16), q2_scr[u]], axis=1)
        rhs = jnp.concatenate([vb[:, ls], st.astype(BF16)], axis=0)
        o_ref[orows, ls] = jnp.dot(lhs, rhs, preferred_element_type=F32)
        k2t = jnp.transpose(ks_scr[u, N_LEVELS]).astype(BF16)
        upd = jnp.dot(k2t, vb[:, ls], preferred_element_type=F32)
        decay_col = jnp.transpose(jnp.broadcast_to(tot_scr[u, 0:1, :], (HEAD_DIM, HEAD_DIM)))
        st_ref[h] = st * decay_col + upd


def _mix_in_kernel(x_ref, mod_ref, nw_ref, win_ref, ws_ref, bs_ref, lb_ref,
                   yap_ref, o_ref, sog_ref, sga_ref, sgb_ref,
                   gu_scr, st_ref, mask_ref, qs_scr, ks_scr, p_scr, q2_scr, tot_scr):
    tm = yap_ref.shape[0]
    nblk = tm // BLOCK // M1_CHAINS

    @pl.when(pl.program_id(1) == 0)
    def _():
        st_ref[...] = jnp.zeros_like(st_ref)

    _write_level_masks(mask_ref)
    shift = mod_ref[3:4, :]
    scale = mod_ref[4:5, :]
    row = _block_token(lax.broadcasted_iota(jnp.int32, (BLOCK, BLOCK), 0))
    col = _block_token(lax.broadcasted_iota(jnp.int32, (BLOCK, BLOCK), 1))
    causal = row >= col
    lb = lb_ref[...]

    def project_stage(chain, res):
        b0 = chain * nblk
        rows = slice(b0 * BLOCK, (b0 + nblk) * BLOCK)
        x = _gather_block_rows(x_ref, b0, nblk)
        hb = _rms_modulate(x, nw_ref[...], scale, shift).astype(BF16)

        def project(p0):
            z = jnp.dot(hb, win_ref[:, p0 * D_MODEL:(p0 + M1_PIECES) * D_MODEL],
                        preferred_element_type=F32)
            return [z[:, i * D_MODEL:(i + 1) * D_MODEL] for i in range(M1_PIECES)]

        zu, zv = project(0)
        gu_scr[rows, :] = _gelu_tanh(zu)
        res["gv"] = _gelu_tanh(zv)
        yield
        zq, zf = project(2)
        res["q"] = zq * _sigmoid_t(zq)
        th = 0.5 * jnp.tanh(0.5 * zf)
        res["f"] = lb + (1.0 - lb) * (0.5 + th)
        res["k"] = (1.0 - lb) * (0.5 - th)
        yield
        zi, zog = project(4)
        res["vb"] = zi.astype(BF16)
        sog_ref[rows, :] = _sigmoid_t(zog).astype(BF16)
        yield
        zga, zgb = project(6)
        sga_ref[rows, :] = _sigmoid_t(zga).astype(BF16)
        sgb_ref[rows, :] = _sigmoid_t(zgb).astype(BF16)
        yield

    def gmlp_stage(chain, res):
        b0 = chain * nblk
        gv = res["gv"]
        for g in range(GMLP_GROUPS):
            ls = slice(g * HEAD_DIM, (g + 1) * HEAD_DIM)
            xg = gv[:, ls]
            mu = jnp.mean(xg, axis=-1, keepdims=True)
            xc = xg - mu
            var = jnp.mean(xc * xc, axis=-1, keepdims=True)
            vn = (xc * lax.rsqrt(var + EPS)).astype(BF16)
            rhs = jnp.concatenate(
                [vn[n * BLOCK:(n + 1) * BLOCK, :] for n in range(nblk)], axis=1)
            wc = jnp.where(causal, ws_ref[g], 0.0).astype(BF16)
            mixed = jnp.dot(wc, rhs, preferred_element_type=F32)
            bias = jnp.broadcast_to(bs_ref[:, g:g + 1], (BLOCK, BLOCK))
            for n in range(nblk):
                rs = slice((b0 + n) * BLOCK, (b0 + n + 1) * BLOCK)
                mx = mixed[:, n * BLOCK:(n + 1) * BLOCK] + bias
                yap_ref[rs, ls] = (gu_scr[rs, ls] * mx).astype(BF16)

    def hgrn_stage(chain, res):
        b0 = chain * nblk
        for n in range(0, nblk, HGRN_STAGED_BLOCKS):
            blocks = []
            for m in range(n, n + HGRN_STAGED_BLOCKS):
                br = slice(m * BLOCK, (m + 1) * BLOCK)
                orows = slice((b0 + m) * BLOCK, (b0 + m + 1) * BLOCK)
                blocks.append((res["f"][br, :], res["q"][br, :], res["k"][br, :],
                               res["vb"][br, :], orows))
            _hgrn_staged(blocks, st_ref, mask_ref, qs_scr, ks_scr, p_scr, q2_scr, tot_scr, o_ref)

    for chain in range(M1_CHAINS):
        res = {}
        for _ in project_stage(chain, res):
            pass
        gmlp_stage(chain, res)
        hgrn_stage(chain, res)


def _mix_in(h16, mod, nw, w_in, ws_p, bs_p, lb, *, batch, seq):
    n = h16.shape[0] * POSITIONS
    tm = M1_TM
    tiles_per_seq = seq // tm
    tile = pl.BlockSpec((tm, D_MODEL), lambda b, t: (b * tiles_per_seq + t, 0))
    const2 = lambda b, t: (0, 0)
    f32_out = jax.ShapeDtypeStruct((n, D_MODEL), F32)
    bf_out = jax.ShapeDtypeStruct((n, D_MODEL), BF16)
    units = HGRN_STAGED_BLOCKS * N_HEADS
    return pl.pallas_call(
        _mix_in_kernel,
        grid=(batch, tiles_per_seq),
        in_specs=[
            pl.BlockSpec((tm // POSITIONS, CHUNK_WIDTH), lambda b, t: (b * tiles_per_seq + t, 0)),
            pl.BlockSpec((None, N_ADA, D_MODEL), lambda b, t: (b, 0, 0)),
            pl.BlockSpec((1, D_MODEL), const2),
            pl.BlockSpec((D_MODEL, N_PIECES * D_MODEL), const2,
                         pipeline_mode=pl.Buffered(1)),
            pl.BlockSpec((GMLP_GROUPS, BLOCK, BLOCK), lambda b, t: (0, 0, 0)),
            pl.BlockSpec((BLOCK, GMLP_GROUPS), const2),
            pl.BlockSpec((1, D_MODEL), const2),
        ],
        out_specs=[tile] * 5,
        out_shape=[bf_out, f32_out, bf_out, bf_out, bf_out],
        scratch_shapes=[
            pltpu.VMEM((tm, D_MODEL), F32),
            pltpu.VMEM((N_HEADS, HEAD_DIM, HEAD_DIM), F32),
            pltpu.VMEM((N_MASKS, BLOCK, BLOCK), F32),
            pltpu.VMEM((units, N_LEVELS, BLOCK, HEAD_DIM), BF16),
            pltpu.VMEM((units, N_LEVELS + 1, BLOCK, HEAD_DIM), F32),
            pltpu.VMEM((units, BLOCK, BLOCK), F32),
            pltpu.VMEM((units, BLOCK, HEAD_DIM), BF16),
            pltpu.VMEM((units, CHUNKS, HEAD_DIM), F32),
        ],
        compiler_params=pltpu.CompilerParams(
            dimension_semantics=("parallel", "arbitrary"), vmem_limit_bytes=VMEM_LIMIT),
        name="mixer_in_proj_hgrn2",
    )(h16, mod, nw, w_in, ws_p, bs_p, lb)


M3_TM = 1024
M3_CHAINS = 2


def _mix_out_kernel(x_ref, mod_ref, yap_ref, o_ref, sog_ref, sga_ref, sgb_ref,
                    wa_ref, wb_ref, wo_ref, out_ref):
    nblk = yap_ref.shape[0] // BLOCK // M3_CHAINS
    gate = mod_ref[5:6, :]
    for chain in range(M3_CHAINS):
        b0 = chain * nblk
        rows = slice(b0 * BLOCK, (b0 + nblk) * BLOCK)
        o = o_ref[rows, :]
        parts = []
        for h in range(N_HEADS):
            oh = o[:, h * HEAD_DIM:(h + 1) * HEAD_DIM]
            ms = jnp.mean(oh * oh, axis=-1, keepdims=True)
            parts.append(oh * lax.rsqrt(ms + EPS))
        on = (jnp.concatenate(parts, axis=1) * sog_ref[rows, :].astype(F32)).astype(BF16)
        ya = jnp.dot(yap_ref[rows, :], wa_ref[...], preferred_element_type=F32)
        yb = jnp.dot(on, wb_ref[...], preferred_element_type=F32)
        m = (sga_ref[rows, :].astype(F32) * ya + sgb_ref[rows, :].astype(F32) * yb).astype(BF16)
        out = jnp.dot(m, wo_ref[...], preferred_element_type=F32)
        y = _gather_block_rows(x_ref, b0, nblk) + (1.0 + gate) * out
        _scatter_block_rows(out_ref, y, b0, nblk)


def _mix_out(h16, mod, yap, o, sog, sga, sgb, w_a, w_b, w_o, *, seq):
    n = h16.shape[0] * POSITIONS
    tm = M3_TM
    tiles_per_seq = seq // tm
    tile = pl.BlockSpec((tm, D_MODEL), lambda i: (i, 0))
    tile16 = pl.BlockSpec((tm // POSITIONS, CHUNK_WIDTH), lambda i: (i, 0))
    wspec = pl.BlockSpec((D_MODEL, D_MODEL), lambda i: (0, 0),
                         pipeline_mode=pl.Buffered(1))
    return pl.pallas_call(
        _mix_out_kernel,
        grid=(n // tm,),
        in_specs=[
            tile16,
            pl.BlockSpec((None, N_ADA, D_MODEL), lambda i: (i // tiles_per_seq, 0, 0)),
            tile, tile, tile, tile, tile,
            wspec, wspec, wspec,
        ],
        out_specs=tile16,
        out_shape=jax.ShapeDtypeStruct(h16.shape, F32),
        compiler_params=pltpu.CompilerParams(
            dimension_semantics=("parallel",), vmem_limit_bytes=VMEM_LIMIT),
        name="mixer_out_proj",
    )(h16, mod, yap, o, sog, sga, sgb, w_a, w_b, w_o)


def kernel(x, c, w_in, gmlp_ws, gmlp_b, hgrn_lb_logits, w_branch_a, w_branch_b, w_out,
           ffn_w13, ffn_w2, norm_w, ada_w, ada_b, final_norm_w):
    batch, seq, d = x.shape
    assert d == D_MODEL and seq % M1_TM == 0 and seq % FFN_TM == 0 and seq % M3_TM == 0
    n = batch * seq

    mod_all = _ada_mod(c, ada_w, ada_b).reshape(DEPTH, batch, N_ADA, D_MODEL)
    lb_all = _lower_bounds(hgrn_lb_logits)
    perm = jnp.array([_block_token(p) for p in range(BLOCK)], dtype=jnp.int32)
    ws_p = gmlp_ws[:, :, perm, :][:, :, :, perm]
    bs_p = jnp.swapaxes(gmlp_b[:, :, perm], 1, 2)

    h16 = x.reshape(n, D_MODEL)
    for l in range(DEPTH):
        mod = mod_all[l]
        nw = norm_w[l].reshape(3, 1, D_MODEL)
        h16 = _ffn(h16, mod, nw[0], ffn_w13[l, 0].astype(BF16), ffn_w2[l, 0].astype(BF16),
                   mod_off=0, seq=seq, in_natural=(l == 0))
        yap, o, sog, sga, sgb = _mix_in(
            h16, mod, nw[1], w_in[l].astype(BF16), ws_p[l], bs_p[l],
            lb_all[l].reshape(1, D_MODEL), batch=batch, seq=seq)
        h16 = _mix_out(h16, mod, yap, o, sog, sga, sgb, w_branch_a[l].astype(BF16),
                       w_branch_b[l].astype(BF16), w_out[l].astype(BF16), seq=seq)
        final_w = final_norm_w.reshape(1, D_MODEL) if l == DEPTH - 1 else None
        h16 = _ffn(h16, mod, nw[2], ffn_w13[l, 1].astype(BF16), ffn_w2[l, 1].astype(BF16),
                   mod_off=6, seq=seq, final_w=final_w)
    return h16.reshape(batch, seq, D_MODEL)
```

```python
import functools

import jax
import jax.numpy as jnp
from jax import lax
from jax.experimental import pallas as pl
from jax.experimental.pallas import tpu as pltpu

F32 = jnp.float32
BF16 = jnp.bfloat16

D_MODEL = 1024
DEPTH = 4
FFN_HIDDEN = 2816
N_ADA = 9
EPS = 1e-6
GMLP_GROUPS = 8
HEAD_DIM = 128
N_HEADS = D_MODEL // HEAD_DIM
N_PIECES = 8

BLOCK = 128
POSITIONS = 16
CHUNKS = BLOCK // POSITIONS
CHUNK_WIDTH = POSITIONS * D_MODEL

VMEM_LIMIT = 56 * 1024 * 1024


def _sigmoid(x):
    return 1.0 / (1.0 + jnp.exp(-x))


def _sigmoid_t(x):
    return 0.5 * jnp.tanh(0.5 * x) + 0.5


def _gelu_tanh(x):
    c = 0.7978845608028654
    return 0.5 * x * (1.0 + jnp.tanh(c * (x + 0.044715 * (x * x * x))))


def _rms_modulate(x, nw, scale, shift):
    ms = jnp.mean(x * x, axis=-1, keepdims=True)
    return (x * lax.rsqrt(ms + EPS) * nw) * (1.0 + scale) + shift


def _block_token(p):
    return (p % CHUNKS) * POSITIONS + p // CHUNKS


def _gather_block_rows(x_ref, b0, n_blocks):
    pieces = [x_ref[n * CHUNKS:(n + 1) * CHUNKS, j * D_MODEL:(j + 1) * D_MODEL]
              for n in range(b0, b0 + n_blocks) for j in range(POSITIONS)]
    return jnp.concatenate(pieces, axis=0)


def _scatter_block_rows(out_ref, y, b0, n_blocks):
    for n in range(n_blocks):
        for j in range(POSITIONS):
            r = (n * POSITIONS + j) * CHUNKS
            out_ref[(b0 + n) * CHUNKS:(b0 + n + 1) * CHUNKS,
                    j * D_MODEL:(j + 1) * D_MODEL] = y[r:r + CHUNKS, :]


def _ada_kernel(c_ref, w_ref, b_ref, o_ref):
    c = c_ref[...]
    ca = c * _sigmoid(c)
    o_ref[...] = jnp.dot(ca, w_ref[...], preferred_element_type=F32,
                         precision=lax.Precision.HIGHEST) + b_ref[...]


def _ada_mod(c, ada_w, ada_b):
    b = c.shape[0]
    return pl.pallas_call(
        _ada_kernel,
        grid=(DEPTH, N_ADA),
        in_specs=[
            pl.BlockSpec((b, D_MODEL), lambda l, j: (0, 0)),
            pl.BlockSpec((None, D_MODEL, D_MODEL), lambda l, j: (l, 0, j)),
            pl.BlockSpec((None, 1, D_MODEL), lambda l, j: (l, 0, j)),
        ],
        out_specs=pl.BlockSpec((None, b, D_MODEL), lambda l, j: (l, 0, j)),
        out_shape=jax.ShapeDtypeStruct((DEPTH, b, N_ADA * D_MODEL), F32),
        name="ada_mod",
    )(c, ada_w, ada_b.reshape(DEPTH, 1, N_ADA * D_MODEL))


def _lb_kernel(x_ref, o_ref):
    x = x_ref[...]
    rows = [x[i:i + 1, :] for i in range(DEPTH)]
    m = rows[0]
    for r in rows[1:]:
        m = jnp.maximum(m, r)
    e = [jnp.exp(r - m) for r in rows]
    tot = e[0]
    for t in e[1:]:
        tot = tot + t
    p = [t / tot for t in e]
    cum = p[0]
    first = cum
    for i in range(DEPTH):
        if i > 0:
            cum = cum + p[i]
        o_ref[i:i + 1, :] = cum - first


def _lower_bounds(logits):
    return pl.pallas_call(
        _lb_kernel,
        out_shape=jax.ShapeDtypeStruct(logits.shape, F32),
        name="hgrn_lower_bounds",
    )(logits.astype(F32))


FFN_TM = 1024
FFN_CHAINS = 4
FFN_COL_CHUNKS = ((0, 1024), (1024, 2048), (2048, FFN_HIDDEN))


def _ffn_kernel(x_ref, mod_ref, nw_ref, w13_ref, w2_ref, *rest, mod_off, final, in_natural):
    if final:
        fnw_ref, o_ref = rest
    else:
        (o_ref,) = rest
    shift = mod_ref[mod_off:mod_off + 1, :]
    scale = mod_ref[mod_off + 1:mod_off + 2, :]
    gate = mod_ref[mod_off + 2:mod_off + 3, :]
    rows = FFN_TM // POSITIONS // FFN_CHAINS
    for r in range(FFN_CHAINS):
        rs = slice(r * rows, (r + 1) * rows)
        if in_natural:
            nat = x_ref[r * rows * POSITIONS:(r + 1) * rows * POSITIONS, :]
            x = pltpu.einshape("(cj)d->(jc)d", nat, j=POSITIONS)
        else:
            x = jnp.concatenate(
                [x_ref[rs, j * D_MODEL:(j + 1) * D_MODEL] for j in range(POSITIONS)], axis=0)
        hb = _rms_modulate(x, nw_ref[...], scale, shift).astype(BF16)
        acc = None
        for c0, c1 in FFN_COL_CHUNKS:
            a = jnp.dot(hb, w13_ref[:, c0:c1], preferred_element_type=F32)
            b = jnp.dot(hb, w13_ref[:, FFN_HIDDEN + c0:FFN_HIDDEN + c1],
                        preferred_element_type=F32)
            act = (a * _sigmoid_t(a) * b).astype(BF16)
            part = jnp.dot(act, w2_ref[c0:c1, :], preferred_element_type=F32)
            acc = part if acc is None else acc + part
        y = x + (0.5 * (1.0 + gate)) * acc
        if final:
            ms = jnp.mean(y * y, axis=-1, keepdims=True)
            y = y * lax.rsqrt(ms + EPS) * fnw_ref[...]
            o_ref[r * rows * POSITIONS:(r + 1) * rows * POSITIONS, :] = pltpu.einshape(
                "(jc)d->(cj)d", y, j=POSITIONS)
        else:
            for j in range(POSITIONS):
                o_ref[rs, j * D_MODEL:(j + 1) * D_MODEL] = y[j * rows:(j + 1) * rows, :]


def _ffn(h_in, mod, nw, w13, w2, *, mod_off, seq, final_w=None, in_natural=False):
    n = h_in.shape[0] if in_natural else h_in.shape[0] * POSITIONS
    tm = FFN_TM
    tiles_per_seq = seq // tm
    final = final_w is not None
    tile16 = pl.BlockSpec((tm // POSITIONS, CHUNK_WIDTH), lambda i: (i, 0))
    in_specs = [
        pl.BlockSpec((tm, D_MODEL), lambda i: (i, 0)) if in_natural else tile16,
        pl.BlockSpec((None, N_ADA, D_MODEL), lambda i: (i // tiles_per_seq, 0, 0)),
        pl.BlockSpec((1, D_MODEL), lambda i: (0, 0)),
        pl.BlockSpec((D_MODEL, 2 * FFN_HIDDEN), lambda i: (0, 0),
                     pipeline_mode=pl.Buffered(1)),
        pl.BlockSpec((FFN_HIDDEN, D_MODEL), lambda i: (0, 0),
                     pipeline_mode=pl.Buffered(1)),
    ]
    args = [h_in, mod, nw, w13, w2]
    if final:
        in_specs.append(pl.BlockSpec((1, D_MODEL), lambda i: (0, 0)))
        args.append(final_w)
    return pl.pallas_call(
        functools.partial(_ffn_kernel, mod_off=mod_off, final=final, in_natural=in_natural),
        grid=(n // tm,),
        in_specs=in_specs,
        out_specs=pl.BlockSpec((tm, D_MODEL), lambda i: (i, 0)) if final else tile16,
        out_shape=jax.ShapeDtypeStruct(
            (n, D_MODEL) if final else (n // POSITIONS, CHUNK_WIDTH), F32),
        compiler_params=pltpu.CompilerParams(
            dimension_semantics=("parallel",), vmem_limit_bytes=VMEM_LIMIT),
        name="ffn_half_step",
    )(*args)


M1_TM = 512
M1_PIECES = 2
M1_CHAINS = 2
N_MASKS = 7


def _seg_scan_rows(x, rowid, width, reverse):
    d = 1
    y = x
    while d < width:
        if reverse:
            shifted = pltpu.roll(y, CHUNKS - d, axis=0)
            ok = (rowid & (width - 1)) < width - d
        else:
            shifted = pltpu.roll(y, d, axis=0)
            ok = (rowid & (width - 1)) >= d
        y = y * jnp.where(ok, shifted, 1.0)
        d *= 2
    return y


def _write_level_masks(mask_ref):
    pr = lax.broadcasted_iota(jnp.int32, (BLOCK, BLOCK), 0)
    pc = lax.broadcasted_iota(jnp.int32, (BLOCK, BLOCK), 1)
    same_chunk = ((pr ^ pc) & (CHUNKS - 1)) == 0
    mask_ref[0] = jnp.where(pr == pc, 1.0, 0.0)
    for beta in range(4):
        sh = 3 + beta + 1
        mask_ref[1 + beta] = jnp.where(same_chunk & ((pr >> sh) == (pc >> sh)), 1.0, 0.0)
    for gamma in range(2):
        sh = gamma + 1
        mask_ref[5 + gamma] = jnp.where(
            ((pr & (CHUNKS - 1)) >> sh) == ((pc & (CHUNKS - 1)) >> sh), 1.0, 0.0)


N_LEVELS = 7
HGRN_STAGED_BLOCKS = 1
LEVEL_TARGETS = (
    [[j for j in range(POSITIONS) if j & (1 << beta)] for beta in range(4)]
    + [list(range(POSITIONS))] * 3)
LEVEL_MASK = (1, 2, 3, 4, 5, 6, None)


def _hgrn_staged(blocks, st_ref, mask_ref, qs_scr, ks_scr, p_scr, q2_scr, tot_scr, o_ref):
    n_units = len(blocks) * N_HEADS
    rowid = lax.broadcasted_iota(jnp.int32, (CHUNKS, HEAD_DIM), 0)
    zero = jnp.zeros((CHUNKS, HEAD_DIM), F32)

    def slabs(val):
        return [val[j * CHUNKS:(j + 1) * CHUNKS, :] for j in range(POSITIONS)]

    def cat(sl):
        return jnp.concatenate(sl, axis=0)

    def mask_rows(level, j):
        return mask_ref[level, j * CHUNKS:(j + 1) * CHUNKS, :]

    for u in range(n_units):
        h = u % N_HEADS
        f, q, k, _, _ = blocks[u // N_HEADS]
        ls = slice(h * HEAD_DIM, (h + 1) * HEAD_DIM)
        fs, qs, ks = slabs(f[:, ls]), slabs(q[:, ls]), slabs(k[:, ls])
        for j in range(POSITIONS):
            p_scr[u, j * CHUNKS:(j + 1) * CHUNKS, :] = (
                jnp.sum(qs[j] * ks[j], axis=-1, keepdims=True) * mask_rows(0, j))

        for beta in range(4):
            w = 1 << beta
            a = [None] * POSITIONS
            for j in range(POSITIONS):
                if j & w:
                    a[j] = fs[j] if (j & (w - 1)) == 0 else a[j - 1] * fs[j]
            b = [None] * POSITIONS
            for j in range(POSITIONS - 1, -1, -1):
                if not (j & w) and (j & (w - 1)) != w - 1:
                    b[j] = fs[j + 1] if b[j + 1] is None else b[j + 1] * fs[j + 1]
            upper = LEVEL_TARGETS[beta]
            qs_scr[u, beta, 0:len(upper) * CHUNKS, :] = cat(
                [qs[j] * a[j] for j in upper]).astype(BF16)
            ks_scr[u, beta] = cat(
                [zero if (j & w) else (ks[j] if b[j] is None else ks[j] * b[j])
                 for j in range(POSITIONS)])

        gcum = [fs[0]]
        for j in range(1, POSITIONS):
            gcum.append(gcum[-1] * fs[j])
        hsuf = [None] * POSITIONS
        for j in range(POSITIONS - 2, -1, -1):
            hsuf[j] = fs[j + 1] if hsuf[j + 1] is None else hsuf[j + 1] * fs[j + 1]
        g = gcum[-1]
        qg = [qs[j] * gcum[j] for j in range(POSITIONS)]
        kh = [ks[j] if hsuf[j] is None else ks[j] * hsuf[j] for j in range(POSITIONS)]
        g_prev = pltpu.roll(g, 1, axis=0)
        g_next = pltpu.roll(g, CHUNKS - 1, axis=0)

        for gamma in range(3):
            w = 1 << gamma
            low = rowid & (w - 1)
            ca = _seg_scan_rows(jnp.where(low == 0, 1.0, g_prev), rowid, w, False)
            ca = jnp.where((rowid & w) != 0, ca, 0.0)
            cb = _seg_scan_rows(jnp.where(low == w - 1, 1.0, g_next), rowid, w, True)
            cb = jnp.where((rowid & w) == 0, cb, 0.0)
            qs_scr[u, 4 + gamma] = cat([x * ca for x in qg]).astype(BF16)
            ks_scr[u, 4 + gamma] = cat([x * cb for x in kh])

        cp = _seg_scan_rows(jnp.where(rowid == 0, 1.0, g_prev), rowid, CHUNKS, False)
        cs = _seg_scan_rows(jnp.where(rowid == CHUNKS - 1, 1.0, g_next), rowid, CHUNKS, True)
        q2_scr[u] = cat([x * cp for x in qg]).astype(BF16)
        ks_scr[u, N_LEVELS] = cat([x * cs for x in kh])
        tot_scr[u] = jnp.broadcast_to((cp * g)[CHUNKS - 1:CHUNKS, :], (CHUNKS, HEAD_DIM))

    for u in range(n_units):
        for lvl in range(N_LEVELS):
            targets = LEVEL_TARGETS[lvl]
            kt = jnp.transpose(ks_scr[u, lvl]).astype(BF16)
            s = jnp.dot(qs_scr[u, lvl, 0:len(targets) * CHUNKS, :], kt,
                        preferred_element_type=F32)
            for i, j in enumerate(targets):
                rs = slice(j * CHUNKS, (j + 1) * CHUNKS)
                sj = s[i * CHUNKS:(i + 1) * CHUNKS, :]
                if LEVEL_MASK[lvl] is not None:
                    sj = sj * mask_rows(LEVEL_MASK[lvl], j)
                p_scr[u, rs, :] = p_scr[u, rs, :] + sj

    for u in range(n_units):
        h = u % N_HEADS
        _, _, _, vb, orows = blocks[u // N_HEADS]
        ls = slice(h * HEAD_DIM, (h + 1) * HEAD_DIM)
        st = st_ref[h]
        lhs = jnp.concatenate([p_scr[u].astype(BF16), q2_scr[u]], axis=1)
        rhs = jnp.concatenate([vb[:, ls], st.astype(BF16)], axis=0)
        o_ref[orows, ls] = jnp.dot(lhs, rhs, preferred_element_type=F32)
        k2t = jnp.transpose(ks_scr[u, N_LEVELS]).astype(BF16)
        upd = jnp.dot(k2t, vb[:, ls], preferred_element_type=F32)
        decay_col = jnp.transpose(jnp.broadcast_to(tot_scr[u, 0:1, :], (HEAD_DIM, HEAD_DIM)))
        st_ref[h] = st * decay_col + upd


def _mix_in_kernel(x_ref, mod_ref, nw_ref, win_ref, ws_ref, bs_ref, lb_ref,
                   yap_ref, o_ref, sog_ref, sga_ref, sgb_ref,
                   gu_scr, st_ref, mask_ref, qs_scr, ks_scr, p_scr, q2_scr, tot_scr):
    tm = yap_ref.shape[0]
    nblk = tm // BLOCK // M1_CHAINS

    @pl.when(pl.program_id(1) == 0)
    def _():
        st_ref[...] = jnp.zeros_like(st_ref)

    _write_level_masks(mask_ref)
    shift = mod_ref[3:4, :]
    scale = mod_ref[4:5, :]
    row = _block_token(lax.broadcasted_iota(jnp.int32, (BLOCK, BLOCK), 0))
    col = _block_token(lax.broadcasted_iota(jnp.int32, (BLOCK, BLOCK), 1))
    causal = row >= col
    lb = lb_ref[...]

    def project_stage(chain, res):
        b0 = chain * nblk
        rows = slice(b0 * BLOCK, (b0 + nblk) * BLOCK)
        x = _gather_block_rows(x_ref, b0, nblk)
        hb = _rms_modulate(x, nw_ref[...], scale, shift).astype(BF16)

        def project(p0):
            z = jnp.dot(hb, win_ref[:, p0 * D_MODEL:(p0 + M1_PIECES) * D_MODEL],
                        preferred_element_type=F32)
            return [z[:, i * D_MODEL:(i + 1) * D_MODEL] for i in range(M1_PIECES)]

        zu, zv = project(0)
        gu_scr[rows, :] = _gelu_tanh(zu)
        res["gv"] = _gelu_tanh(zv)
        yield
        zq, zf = project(2)
        res["q"] = zq * _sigmoid_t(zq)
        th = 0.5 * jnp.tanh(0.5 * zf)
        res["f"] = lb + (1.0 - lb) * (0.5 + th)
        res["k"] = (1.0 - lb) * (0.5 - th)
        yield
        zi, zog = project(4)
        res["vb"] = zi.astype(BF16)
        sog_ref[rows, :] = _sigmoid_t(zog).astype(BF16)
        yield
        zga, zgb = project(6)
        sga_ref[rows, :] = _sigmoid_t(zga).astype(BF16)
        sgb_ref[rows, :] = _sigmoid_t(zgb).astype(BF16)
        yield

    def gmlp_stage(chain, res):
        b0 = chain * nblk
        gv = res["gv"]
        for g in range(GMLP_GROUPS):
            ls = slice(g * HEAD_DIM, (g + 1) * HEAD_DIM)
            xg = gv[:, ls]
            mu = jnp.mean(xg, axis=-1, keepdims=True)
            xc = xg - mu
            var = jnp.mean(xc * xc, axis=-1, keepdims=True)
            vn = (xc * lax.rsqrt(var + EPS)).astype(BF16)
            rhs = jnp.concatenate(
                [vn[n * BLOCK:(n + 1) * BLOCK, :] for n in range(nblk)], axis=1)
            wc = jnp.where(causal, ws_ref[g], 0.0).astype(BF16)
            mixed = jnp.dot(wc, rhs, preferred_element_type=F32)
            bias = jnp.broadcast_to(bs_ref[:, g:g + 1], (BLOCK, BLOCK))
            for n in range(nblk):
                rs = slice((b0 + n) * BLOCK, (b0 + n + 1) * BLOCK)
                mx = mixed[:, n * BLOCK:(n + 1) * BLOCK] + bias
                yap_ref[rs, ls] = (gu_scr[rs, ls] * mx).astype(BF16)

    def hgrn_stage(chain, res):
        b0 = chain * nblk
        for n in range(0, nblk, HGRN_STAGED_BLOCKS):
            blocks = []
            for m in range(n, n + HGRN_STAGED_BLOCKS):
                br = slice(m * BLOCK, (m + 1) * BLOCK)
                orows = slice((b0 + m) * BLOCK, (b0 + m + 1) * BLOCK)
                blocks.append((res["f"][br, :], res["q"][br, :], res["k"][br, :],
                               res["vb"][br, :], orows))
            _hgrn_staged(blocks, st_ref, mask_ref, qs_scr, ks_scr, p_scr, q2_scr, tot_scr, o_ref)

    for chain in range(M1_CHAINS):
        res = {}
        for _ in project_stage(chain, res):
            pass
        gmlp_stage(chain, res)
        hgrn_stage(chain, res)


def _mix_in(h16, mod, nw, w_in, ws_p, bs_p, lb, *, batch, seq):
    n = h16.shape[0] * POSITIONS
    tm = M1_TM
    tiles_per_seq = seq // tm
    tile = pl.BlockSpec((tm, D_MODEL), lambda b, t: (b * tiles_per_seq + t, 0))
    const2 = lambda b, t: (0, 0)
    f32_out = jax.ShapeDtypeStruct((n, D_MODEL), F32)
    bf_out = jax.ShapeDtypeStruct((n, D_MODEL), BF16)
    units = HGRN_STAGED_BLOCKS * N_HEADS
    return pl.pallas_call(
        _mix_in_kernel,
        grid=(batch, tiles_per_seq),
        in_specs=[
            pl.BlockSpec((tm // POSITIONS, CHUNK_WIDTH), lambda b, t: (b * tiles_per_seq + t, 0)),
            pl.BlockSpec((None, N_ADA, D_MODEL), lambda b, t: (b, 0, 0)),
            pl.BlockSpec((1, D_MODEL), const2),
            pl.BlockSpec((D_MODEL, N_PIECES * D_MODEL), const2,
                         pipeline_mode=pl.Buffered(1)),
            pl.BlockSpec((GMLP_GROUPS, BLOCK, BLOCK), lambda b, t: (0, 0, 0)),
            pl.BlockSpec((BLOCK, GMLP_GROUPS), const2),
            pl.BlockSpec((1, D_MODEL), const2),
        ],
        out_specs=[tile] * 5,
        out_shape=[bf_out, f32_out, bf_out, bf_out, bf_out],
        scratch_shapes=[
            pltpu.VMEM((tm, D_MODEL), F32),
            pltpu.VMEM((N_HEADS, HEAD_DIM, HEAD_DIM), F32),
            pltpu.VMEM((N_MASKS, BLOCK, BLOCK), F32),
            pltpu.VMEM((units, N_LEVELS, BLOCK, HEAD_DIM), BF16),
            pltpu.VMEM((units, N_LEVELS + 1, BLOCK, HEAD_DIM), F32),
            pltpu.VMEM((units, BLOCK, BLOCK), F32),
            pltpu.VMEM((units, BLOCK, HEAD_DIM), BF16),
            pltpu.VMEM((units, CHUNKS, HEAD_DIM), F32),
        ],
        compiler_params=pltpu.CompilerParams(
            dimension_semantics=("parallel", "arbitrary"), vmem_limit_bytes=VMEM_LIMIT),
        name="mixer_in_proj_hgrn2",
    )(h16, mod, nw, w_in, ws_p, bs_p, lb)


M3_TM = 1024
M3_CHAINS = 2


def _mix_out_kernel(x_ref, mod_ref, yap_ref, o_ref, sog_ref, sga_ref, sgb_ref,
                    wa_ref, wb_ref, wo_ref, out_ref):
    nblk = yap_ref.shape[0] // BLOCK // M3_CHAINS
    gate = mod_ref[5:6, :]
    for chain in range(M3_CHAINS):
        b0 = chain * nblk
        rows = slice(b0 * BLOCK, (b0 + nblk) * BLOCK)
        o = o_ref[rows, :]
        parts = []
        for h in range(N_HEADS):
            oh = o[:, h * HEAD_DIM:(h + 1) * HEAD_DIM]
            ms = jnp.mean(oh * oh, axis=-1, keepdims=True)
            parts.append(oh * lax.rsqrt(ms + EPS))
        on = (jnp.concatenate(parts, axis=1) * sog_ref[rows, :].astype(F32)).astype(BF16)
        ya = jnp.dot(yap_ref[rows, :], wa_ref[...], preferred_element_type=F32)
        yb = jnp.dot(on, wb_ref[...], preferred_element_type=F32)
        m = (sga_ref[rows, :].astype(F32) * ya + sgb_ref[rows, :].astype(F32) * yb).astype(BF16)
        out = jnp.dot(m, wo_ref[...], preferred_element_type=F32)
        y = _gather_block_rows(x_ref, b0, nblk) + (1.0 + gate) * out
        _scatter_block_rows(out_ref, y, b0, nblk)


def _mix_out(h16, mod, yap, o, sog, sga, sgb, w_a, w_b, w_o, *, seq):
    n = h16.shape[0] * POSITIONS
    tm = M3_TM
    tiles_per_seq = seq // tm
    tile = pl.BlockSpec((tm, D_MODEL), lambda i: (i, 0))
    tile16 = pl.BlockSpec((tm // POSITIONS, CHUNK_WIDTH), lambda i: (i, 0))
    wspec = pl.BlockSpec((D_MODEL, D_MODEL), lambda i: (0, 0),
                         pipeline_mode=pl.Buffered(1))
    return pl.pallas_call(
        _mix_out_kernel,
        grid=(n // tm,),
        in_specs=[
            tile16,
            pl.BlockSpec((None, N_ADA, D_MODEL), lambda i: (i // tiles_per_seq, 0, 0)),
            tile, tile, tile, tile, tile,
            wspec, wspec, wspec,
        ],
        out_specs=tile16,
        out_shape=jax.ShapeDtypeStruct(h16.shape, F32),
        compiler_params=pltpu.CompilerParams(
            dimension_semantics=("parallel",), vmem_limit_bytes=VMEM_LIMIT),
        name="mixer_out_proj",
    )(h16, mod, yap, o, sog, sga, sgb, w_a, w_b, w_o)


def kernel(x, c, w_in, gmlp_ws, gmlp_b, hgrn_lb_logits, w_branch_a, w_branch_b, w_out,
           ffn_w13, ffn_w2, norm_w, ada_w, ada_b, final_norm_w):
    batch, seq, d = x.shape
    assert d == D_MODEL and seq % M1_TM == 0 and seq % FFN_TM == 0 and seq % M3_TM == 0
    n = batch * seq

    mod_all = _ada_mod(c, ada_w, ada_b).reshape(DEPTH, batch, N_ADA, D_MODEL)
    lb_all = _lower_bounds(hgrn_lb_logits)
    perm = jnp.array([_block_token(p) for p in range(BLOCK)], dtype=jnp.int32)
    ws_p = gmlp_ws[:, :, perm, :][:, :, :, perm]
    bs_p = jnp.swapaxes(gmlp_b[:, :, perm], 1, 2)

    h16 = x.reshape(n, D_MODEL)
    for l in range(DEPTH):
        mod = mod_all[l]
        nw = norm_w[l].reshape(3, 1, D_MODEL)
        h16 = _ffn(h16, mod, nw[0], ffn_w13[l, 0].astype(BF16), ffn_w2[l, 0].astype(BF16),
                   mod_off=0, seq=seq, in_natural=(l == 0))
        yap, o, sog, sga, sgb = _mix_in(
            h16, mod, nw[1], w_in[l].astype(BF16), ws_p[l], bs_p[l],
            lb_all[l].reshape(1, D_MODEL), batch=batch, seq=seq)
        h16 = _mix_out(h16, mod, yap, o, sog, sga, sgb, w_branch_a[l].astype(BF16),
                       w_branch_b[l].astype(BF16), w_out[l].astype(BF16), seq=seq)
        final_w = final_norm_w.reshape(1, D_MODEL) if l == DEPTH - 1 else None
        h16 = _ffn(h16, mod, nw[2], ffn_w13[l, 1].astype(BF16), ffn_w2[l, 1].astype(BF16),
                   mod_off=6, seq=seq, final_w=final_w)
    return h16.reshape(batch, seq, D_MODEL)
```

```python
import functools

import jax
import jax.numpy as jnp
from jax import lax
from jax.experimental import pallas as pl
from jax.experimental.pallas import tpu as pltpu

F32 = jnp.float32
BF16 = jnp.bfloat16

D_MODEL = 1024
DEPTH = 4
FFN_HIDDEN = 2816
N_ADA = 9
EPS = 1e-6
GMLP_GROUPS = 8
HEAD_DIM = 128
N_HEADS = D_MODEL // HEAD_DIM
N_PIECES = 8

BLOCK = 128
POSITIONS = 16
CHUNKS = BLOCK // POSITIONS
CHUNK_WIDTH = POSITIONS * D_MODEL

VMEM_LIMIT = 56 * 1024 * 1024


def _sigmoid(x):
    return 1.0 / (1.0 + jnp.exp(-x))


def _sigmoid_t(x):
    return 0.5 * jnp.tanh(0.5 * x) + 0.5


def _gelu_tanh(x):
    c = 0.7978845608028654
    return 0.5 * x * (1.0 + jnp.tanh(c * (x + 0.044715 * (x * x * x))))


def _rms_modulate(x, nw, scale, shift):
    ms = jnp.mean(x * x, axis=-1, keepdims=True)
    return (x * lax.rsqrt(ms + EPS) * nw) * (1.0 + scale) + shift


def _block_token(p):
    return (p % CHUNKS) * POSITIONS + p // CHUNKS


def _gather_block_rows(x_ref, b0, n_blocks):
    pieces = [x_ref[n * CHUNKS:(n + 1) * CHUNKS, j * D_MODEL:(j + 1) * D_MODEL]
              for n in range(b0, b0 + n_blocks) for j in range(POSITIONS)]
    return jnp.concatenate(pieces, axis=0)


def _scatter_block_rows(out_ref, y, b0, n_blocks):
    for n in range(n_blocks):
        for j in range(POSITIONS):
            r = (n * POSITIONS + j) * CHUNKS
            out_ref[(b0 + n) * CHUNKS:(b0 + n + 1) * CHUNKS,
                    j * D_MODEL:(j + 1) * D_MODEL] = y[r:r + CHUNKS, :]


def _ada_kernel(c_ref, w_ref, b_ref, o_ref):
    c = c_ref[...]
    ca = c * _sigmoid(c)
    o_ref[...] = jnp.dot(ca, w_ref[...], preferred_element_type=F32,
                         precision=lax.Precision.HIGHEST) + b_ref[...]


def _ada_mod(c, ada_w, ada_b):
    b = c.shape[0]
    return pl.pallas_call(
        _ada_kernel,
        grid=(DEPTH, N_ADA),
        in_specs=[
            pl.BlockSpec((b, D_MODEL), lambda l, j: (0, 0)),
            pl.BlockSpec((None, D_MODEL, D_MODEL), lambda l, j: (l, 0, j)),
            pl.BlockSpec((None, 1, D_MODEL), lambda l, j: (l, 0, j)),
        ],
        out_specs=pl.BlockSpec((None, b, D_MODEL), lambda l, j: (l, 0, j)),
        out_shape=jax.ShapeDtypeStruct((DEPTH, b, N_ADA * D_MODEL), F32),
        name="ada_mod",
    )(c, ada_w, ada_b.reshape(DEPTH, 1, N_ADA * D_MODEL))


def _lb_kernel(x_ref, o_ref):
    x = x_ref[...]
    rows = [x[i:i + 1, :] for i in range(DEPTH)]
    m = rows[0]
    for r in rows[1:]:
        m = jnp.maximum(m, r)
    e = [jnp.exp(r - m) for r in rows]
    tot = e[0]
    for t in e[1:]:
        tot = tot + t
    p = [t / tot for t in e]
    cum = p[0]
    first = cum
    for i in range(DEPTH):
        if i > 0:
            cum = cum + p[i]
        o_ref[i:i + 1, :] = cum - first


def _lower_bounds(logits):
    return pl.pallas_call(
        _lb_kernel,
        out_shape=jax.ShapeDtypeStruct(logits.shape, F32),
        name="hgrn_lower_bounds",
    )(logits.astype(F32))


FFN_TM = 1024
FFN_CHAINS = 4
FFN_COL_CHUNKS = ((0, 1024), (1024, 2048), (2048, FFN_HIDDEN))


def _ffn_kernel(x_ref, mod_ref, nw_ref, w13_ref, w2_ref, *rest, mod_off, final, in_natural):
    if final:
        fnw_ref, o_ref = rest
    else:
        (o_ref,) = rest
    shift = mod_ref[mod_off:mod_off + 1, :]
    scale = mod_ref[mod_off + 1:mod_off + 2, :]
    gate = mod_ref[mod_off + 2:mod_off + 3, :]
    rows = FFN_TM // POSITIONS // FFN_CHAINS
    for r in range(FFN_CHAINS):
        rs = slice(r * rows, (r + 1) * rows)
        if in_natural:
            nat = x_ref[r * rows * POSITIONS:(r + 1) * rows * POSITIONS, :]
            x = pltpu.einshape("(cj)d->(jc)d", nat, j=POSITIONS)
        else:
            x = jnp.concatenate(
                [x_ref[rs, j * D_MODEL:(j + 1) * D_MODEL] for j in range(POSITIONS)], axis=0)
        hb = _rms_modulate(x, nw_ref[...], scale, shift).astype(BF16)
        acc = None
        for c0, c1 in FFN_COL_CHUNKS:
            a = jnp.dot(hb, w13_ref[:, c0:c1], preferred_element_type=F32)
            b = jnp.dot(hb, w13_ref[:, FFN_HIDDEN + c0:FFN_HIDDEN + c1],
                        preferred_element_type=F32)
            act = (a * _sigmoid_t(a) * b).astype(BF16)
            part = jnp.dot(act, w2_ref[c0:c1, :], preferred_element_type=F32)
            acc = part if acc is None else acc + part
        y = x + (0.5 * (1.0 + gate)) * acc
        if final:
            ms = jnp.mean(y * y, axis=-1, keepdims=True)
            y = y * lax.rsqrt(ms + EPS) * fnw_ref[...]
            o_ref[r * rows * POSITIONS:(r + 1) * rows * POSITIONS, :] = pltpu.einshape(
                "(jc)d->(cj)d", y, j=POSITIONS)
        else:
            for j in range(POSITIONS):
                o_ref[rs, j * D_MODEL:(j + 1) * D_MODEL] = y[j * rows:(j + 1) * rows, :]


def _ffn(h_in, mod, nw, w13, w2, *, mod_off, seq, final_w=None, in_natural=False):
    n = h_in.shape[0] if in_natural else h_in.shape[0] * POSITIONS
    tm = FFN_TM
    tiles_per_seq = seq // tm
    final = final_w is not None
    tile16 = pl.BlockSpec((tm // POSITIONS, CHUNK_WIDTH), lambda i: (i, 0))
    in_specs = [
        pl.BlockSpec((tm, D_MODEL), lambda i: (i, 0)) if in_natural else tile16,
        pl.BlockSpec((None, N_ADA, D_MODEL), lambda i: (i // tiles_per_seq, 0, 0)),
        pl.BlockSpec((1, D_MODEL), lambda i: (0, 0)),
        pl.BlockSpec((D_MODEL, 2 * FFN_HIDDEN), lambda i: (0, 0),
                     pipeline_mode=pl.Buffered(1)),
        pl.BlockSpec((FFN_HIDDEN, D_MODEL), lambda i: (0, 0),
                     pipeline_mode=pl.Buffered(1)),
    ]
    args = [h_in, mod, nw, w13, w2]
    if final:
        in_specs.append(pl.BlockSpec((1, D_MODEL), lambda i: (0, 0)))
        args.append(final_w)
    return pl.pallas_call(
        functools.partial(_ffn_kernel, mod_off=mod_off, final=final, in_natural=in_natural),
        grid=(n // tm,),
        in_specs=in_specs,
        out_specs=pl.BlockSpec((tm, D_MODEL), lambda i: (i, 0)) if final else tile16,
        out_shape=jax.ShapeDtypeStruct(
            (n, D_MODEL) if final else (n // POSITIONS, CHUNK_WIDTH), F32),
        compiler_params=pltpu.CompilerParams(
            dimension_semantics=("parallel",), vmem_limit_bytes=VMEM_LIMIT),
        name="ffn_half_step",
    )(*args)


M1_TM = 512
M1_PIECES = 2
M1_CHAINS = 2
PACKED_PARTS = 4
N_MASKS = 7


def _seg_scan_rows(x, rowid, width, reverse):
    d = 1
    y = x
    while d < width:
        if reverse:
            shifted = pltpu.roll(y, CHUNKS - d, axis=0)
            ok = (rowid & (width - 1)) < width - d
        else:
            shifted = pltpu.roll(y, d, axis=0)
            ok = (rowid & (width - 1)) >= d
        y = y * jnp.where(ok, shifted, 1.0)
        d *= 2
    return y


def _write_level_masks(mask_ref):
    pr = lax.broadcasted_iota(jnp.int32, (BLOCK, BLOCK), 0)
    pc = lax.broadcasted_iota(jnp.int32, (BLOCK, BLOCK), 1)
    same_chunk = ((pr ^ pc) & (CHUNKS - 1)) == 0
    mask_ref[0] = jnp.where(pr == pc, 1.0, 0.0)
    for beta in range(4):
        sh = 3 + beta + 1
        mask_ref[1 + beta] = jnp.where(same_chunk & ((pr >> sh) == (pc >> sh)), 1.0, 0.0)
    for gamma in range(2):
        sh = gamma + 1
        mask_ref[5 + gamma] = jnp.where(
            ((pr & (CHUNKS - 1)) >> sh) == ((pc & (CHUNKS - 1)) >> sh), 1.0, 0.0)


N_LEVELS = 7
HGRN_STAGED_BLOCKS = 1
LEVEL_TARGETS = (
    [[j for j in range(POSITIONS) if j & (1 << beta)] for beta in range(4)]
    + [list(range(POSITIONS))] * 3)
LEVEL_MASK = (1, 2, 3, 4, 5, 6, None)


def _hgrn_staged(blocks, st_ref, mask_ref, qs_scr, ks_scr, p_scr, q2_scr, tot_scr, o_ref):
    n_units = len(blocks) * N_HEADS
    rowid = lax.broadcasted_iota(jnp.int32, (CHUNKS, HEAD_DIM), 0)
    zero = jnp.zeros((CHUNKS, HEAD_DIM), F32)

    def slabs(val):
        return [val[j * CHUNKS:(j + 1) * CHUNKS, :] for j in range(POSITIONS)]

    def cat(sl):
        return jnp.concatenate(sl, axis=0)

    def mask_rows(level, j):
        return mask_ref[level, j * CHUNKS:(j + 1) * CHUNKS, :]

    for u in range(n_units):
        h = u % N_HEADS
        f, q, k, _, _ = blocks[u // N_HEADS]
        ls = slice(h * HEAD_DIM, (h + 1) * HEAD_DIM)
        fs, qs, ks = slabs(f[:, ls]), slabs(q[:, ls]), slabs(k[:, ls])
        for j in range(POSITIONS):
            p_scr[u, j * CHUNKS:(j + 1) * CHUNKS, :] = (
                jnp.sum(qs[j] * ks[j], axis=-1, keepdims=True) * mask_rows(0, j))

        for beta in range(4):
            w = 1 << beta
            a = [None] * POSITIONS
            for j in range(POSITIONS):
                if j & w:
                    a[j] = fs[j] if (j & (w - 1)) == 0 else a[j - 1] * fs[j]
            b = [None] * POSITIONS
            for j in range(POSITIONS - 1, -1, -1):
                if not (j & w) and (j & (w - 1)) != w - 1:
                    b[j] = fs[j + 1] if b[j + 1] is None else b[j + 1] * fs[j + 1]
            upper = LEVEL_TARGETS[beta]
            qs_scr[u, beta, 0:len(upper) * CHUNKS, :] = cat(
                [qs[j] * a[j] for j in upper]).astype(BF16)
            ks_scr[u, beta] = cat(
                [zero if (j & w) else (ks[j] if b[j] is None else ks[j] * b[j])
                 for j in range(POSITIONS)])

        gcum = [fs[0]]
        for j in range(1, POSITIONS):
            gcum.append(gcum[-1] * fs[j])
        hsuf = [None] * POSITIONS
        for j in range(POSITIONS - 2, -1, -1):
            hsuf[j] = fs[j + 1] if hsuf[j + 1] is None else hsuf[j + 1] * fs[j + 1]
        g = gcum[-1]
        qg = [qs[j] * gcum[j] for j in range(POSITIONS)]
        kh = [ks[j] if hsuf[j] is None else ks[j] * hsuf[j] for j in range(POSITIONS)]
        g_prev = pltpu.roll(g, 1, axis=0)
        g_next = pltpu.roll(g, CHUNKS - 1, axis=0)

        for gamma in range(3):
            w = 1 << gamma
            low = rowid & (w - 1)
            ca = _seg_scan_rows(jnp.where(low == 0, 1.0, g_prev), rowid, w, False)
            ca = jnp.where((rowid & w) != 0, ca, 0.0)
            cb = _seg_scan_rows(jnp.where(low == w - 1, 1.0, g_next), rowid, w, True)
            cb = jnp.where((rowid & w) == 0, cb, 0.0)
            qs_scr[u, 4 + gamma] = cat([x * ca for x in qg]).astype(BF16)
            ks_scr[u, 4 + gamma] = cat([x * cb for x in kh])

        cp = _seg_scan_rows(jnp.where(rowid == 0, 1.0, g_prev), rowid, CHUNKS, False)
        cs = _seg_scan_rows(jnp.where(rowid == CHUNKS - 1, 1.0, g_next), rowid, CHUNKS, True)
        q2_scr[u] = cat([x * cp for x in qg]).astype(BF16)
        ks_scr[u, N_LEVELS] = cat([x * cs for x in kh])
        tot_scr[u] = jnp.broadcast_to((cp * g)[CHUNKS - 1:CHUNKS, :], (CHUNKS, HEAD_DIM))

    for u in range(n_units):
        for lvl in range(N_LEVELS):
            targets = LEVEL_TARGETS[lvl]
            kt = jnp.transpose(ks_scr[u, lvl]).astype(BF16)
            s = jnp.dot(qs_scr[u, lvl, 0:len(targets) * CHUNKS, :], kt,
                        preferred_element_type=F32)
            for i, j in enumerate(targets):
                rs = slice(j * CHUNKS, (j + 1) * CHUNKS)
                sj = s[i * CHUNKS:(i + 1) * CHUNKS, :]
                if LEVEL_MASK[lvl] is not None:
                    sj = sj * mask_rows(LEVEL_MASK[lvl], j)
                p_scr[u, rs, :] = p_scr[u, rs, :] + sj

    for u in range(n_units):
        h = u % N_HEADS
        _, _, _, vb, orows = blocks[u // N_HEADS]
        ls = slice(h * HEAD_DIM, (h + 1) * HEAD_DIM)
        st = st_ref[h]
        lhs = jnp.concatenate([p_scr[u].astype(BF16), q2_scr[u]], axis=1)
        rhs = jnp.concatenate([vb[:, ls], st.astype(BF16)], axis=0)
        o_ref[orows, ls] = jnp.dot(lhs, rhs, preferred_element_type=F32)
        k2t = jnp.transpose(ks_scr[u, N_LEVELS]).astype(BF16)
        upd = jnp.dot(k2t, vb[:, ls], preferred_element_type=F32)
        decay_col = jnp.transpose(jnp.broadcast_to(tot_scr[u, 0:1, :], (HEAD_DIM, HEAD_DIM)))
        st_ref[h] = st * decay_col + upd


def _mix_in_kernel(x_ref, mod_ref, nw_ref, win_ref, ws_ref, bs_ref, lb_ref,
                   pk_ref, o_ref,
                   gu_scr, st_ref, mask_ref, qs_scr, ks_scr, p_scr, q2_scr, tot_scr):
    yap_ref, sog_ref, sga_ref, sgb_ref = [
        pk_ref.at[:, i * D_MODEL:(i + 1) * D_MODEL] for i in range(PACKED_PARTS)]
    tm = pk_ref.shape[0]
    nblk = tm // BLOCK // M1_CHAINS

    @pl.when(pl.program_id(1) == 0)
    def _():
        st_ref[...] = jnp.zeros_like(st_ref)

    _write_level_masks(mask_ref)
    shift = mod_ref[3:4, :]
    scale = mod_ref[4:5, :]
    row = _block_token(lax.broadcasted_iota(jnp.int32, (BLOCK, BLOCK), 0))
    col = _block_token(lax.broadcasted_iota(jnp.int32, (BLOCK, BLOCK), 1))
    causal = row >= col
    lb = lb_ref[...]

    def project_stage(chain, res):
        b0 = chain * nblk
        rows = slice(b0 * BLOCK, (b0 + nblk) * BLOCK)
        x = _gather_block_rows(x_ref, b0, nblk)
        hb = _rms_modulate(x, nw_ref[...], scale, shift).astype(BF16)

        def project(p0):
            z = jnp.dot(hb, win_ref[:, p0 * D_MODEL:(p0 + M1_PIECES) * D_MODEL],
                        preferred_element_type=F32)
            return [z[:, i * D_MODEL:(i + 1) * D_MODEL] for i in range(M1_PIECES)]

        zu, zv = project(0)
        gu_scr[rows, :] = _gelu_tanh(zu)
        res["gv"] = _gelu_tanh(zv)
        yield
        zq, zf = project(2)
        res["q"] = zq * _sigmoid_t(zq)
        th = 0.5 * jnp.tanh(0.5 * zf)
        res["f"] = lb + (1.0 - lb) * (0.5 + th)
        res["k"] = (1.0 - lb) * (0.5 - th)
        yield
        zi, zog = project(4)
        res["vb"] = zi.astype(BF16)
        sog_ref[rows, :] = _sigmoid_t(zog).astype(BF16)
        yield
        zga, zgb = project(6)
        sga_ref[rows, :] = _sigmoid_t(zga).astype(BF16)
        sgb_ref[rows, :] = _sigmoid_t(zgb).astype(BF16)
        yield

    def gmlp_stage(chain, res):
        b0 = chain * nblk
        gv = res["gv"]
        for g in range(GMLP_GROUPS):
            ls = slice(g * HEAD_DIM, (g + 1) * HEAD_DIM)
            xg = gv[:, ls]
            mu = jnp.mean(xg, axis=-1, keepdims=True)
            xc = xg - mu
            var = jnp.mean(xc * xc, axis=-1, keepdims=True)
            vn = (xc * lax.rsqrt(var + EPS)).astype(BF16)
            rhs = jnp.concatenate(
                [vn[n * BLOCK:(n + 1) * BLOCK, :] for n in range(nblk)], axis=1)
            wc = jnp.where(causal, ws_ref[g], 0.0).astype(BF16)
            mixed = jnp.dot(wc, rhs, preferred_element_type=F32)
            bias = jnp.broadcast_to(bs_ref[:, g:g + 1], (BLOCK, BLOCK))
            for n in range(nblk):
                rs = slice((b0 + n) * BLOCK, (b0 + n + 1) * BLOCK)
                mx = mixed[:, n * BLOCK:(n + 1) * BLOCK] + bias
                yap_ref[rs, ls] = (gu_scr[rs, ls] * mx).astype(BF16)

    def hgrn_stage(chain, res):
        b0 = chain * nblk
        for n in range(0, nblk, HGRN_STAGED_BLOCKS):
            blocks = []
            for m in range(n, n + HGRN_STAGED_BLOCKS):
                br = slice(m * BLOCK, (m + 1) * BLOCK)
                orows = slice((b0 + m) * BLOCK, (b0 + m + 1) * BLOCK)
                blocks.append((res["f"][br, :], res["q"][br, :], res["k"][br, :],
                               res["vb"][br, :], orows))
            _hgrn_staged(blocks, st_ref, mask_ref, qs_scr, ks_scr, p_scr, q2_scr, tot_scr, o_ref)

    for chain in range(M1_CHAINS):
        res = {}
        for _ in project_stage(chain, res):
            pass
        gmlp_stage(chain, res)
        hgrn_stage(chain, res)


def _mix_in(h16, mod, nw, w_in, ws_p, bs_p, lb, *, batch, seq):
    n = h16.shape[0] * POSITIONS
    tm = M1_TM
    tiles_per_seq = seq // tm
    tile = pl.BlockSpec((tm, D_MODEL), lambda b, t: (b * tiles_per_seq + t, 0))
    const2 = lambda b, t: (0, 0)
    f32_out = jax.ShapeDtypeStruct((n, D_MODEL), F32)
    bf_out = jax.ShapeDtypeStruct((n, PACKED_PARTS * D_MODEL), BF16)
    units = HGRN_STAGED_BLOCKS * N_HEADS
    return pl.pallas_call(
        _mix_in_kernel,
        grid=(batch, tiles_per_seq),
        in_specs=[
            pl.BlockSpec((tm // POSITIONS, CHUNK_WIDTH), lambda b, t: (b * tiles_per_seq + t, 0)),
            pl.BlockSpec((None, N_ADA, D_MODEL), lambda b, t: (b, 0, 0)),
            pl.BlockSpec((1, D_MODEL), const2),
            pl.BlockSpec((D_MODEL, N_PIECES * D_MODEL), const2,
                         pipeline_mode=pl.Buffered(1)),
            pl.BlockSpec((GMLP_GROUPS, BLOCK, BLOCK), lambda b, t: (0, 0, 0)),
            pl.BlockSpec((BLOCK, GMLP_GROUPS), const2),
            pl.BlockSpec((1, D_MODEL), const2),
        ],
        out_specs=[pl.BlockSpec((tm, PACKED_PARTS * D_MODEL),
                                lambda b, t: (b * tiles_per_seq + t, 0)), tile],
        out_shape=[bf_out, f32_out],
        scratch_shapes=[
            pltpu.VMEM((tm, D_MODEL), F32),
            pltpu.VMEM((N_HEADS, HEAD_DIM, HEAD_DIM), F32),
            pltpu.VMEM((N_MASKS, BLOCK, BLOCK), F32),
            pltpu.VMEM((units, N_LEVELS, BLOCK, HEAD_DIM), BF16),
            pltpu.VMEM((units, N_LEVELS + 1, BLOCK, HEAD_DIM), F32),
            pltpu.VMEM((units, BLOCK, BLOCK), F32),
            pltpu.VMEM((units, BLOCK, HEAD_DIM), BF16),
            pltpu.VMEM((units, CHUNKS, HEAD_DIM), F32),
        ],
        compiler_params=pltpu.CompilerParams(
            dimension_semantics=("parallel", "arbitrary"), vmem_limit_bytes=VMEM_LIMIT),
        name="mixer_in_proj_hgrn2",
    )(h16, mod, nw, w_in, ws_p, bs_p, lb)


M3_TM = 1024
M3_CHAINS = 2


def _mix_out_kernel(x_ref, mod_ref, pk_ref, o_ref, wa_ref, wb_ref, wo_ref, out_ref):
    yap_ref, sog_ref, sga_ref, sgb_ref = [
        pk_ref.at[:, i * D_MODEL:(i + 1) * D_MODEL] for i in range(PACKED_PARTS)]
    nblk = pk_ref.shape[0] // BLOCK // M3_CHAINS
    gate = mod_ref[5:6, :]
    for chain in range(M3_CHAINS):
        b0 = chain * nblk
        rows = slice(b0 * BLOCK, (b0 + nblk) * BLOCK)
        o = o_ref[rows, :]
        parts = []
        for h in range(N_HEADS):
            oh = o[:, h * HEAD_DIM:(h + 1) * HEAD_DIM]
            ms = jnp.mean(oh * oh, axis=-1, keepdims=True)
            parts.append(oh * lax.rsqrt(ms + EPS))
        on = (jnp.concatenate(parts, axis=1) * sog_ref[rows, :].astype(F32)).astype(BF16)
        ya = jnp.dot(yap_ref[rows, :], wa_ref[...], preferred_element_type=F32)
        yb = jnp.dot(on, wb_ref[...], preferred_element_type=F32)
        m = (sga_ref[rows, :].astype(F32) * ya + sgb_ref[rows, :].astype(F32) * yb).astype(BF16)
        out = jnp.dot(m, wo_ref[...], preferred_element_type=F32)
        y = _gather_block_rows(x_ref, b0, nblk) + (1.0 + gate) * out
        _scatter_block_rows(out_ref, y, b0, nblk)


def _mix_out(h16, mod, pk, o, w_a, w_b, w_o, *, seq):
    n = h16.shape[0] * POSITIONS
    tm = M3_TM
    tiles_per_seq = seq // tm
    tile = pl.BlockSpec((tm, D_MODEL), lambda i: (i, 0))
    tile16 = pl.BlockSpec((tm // POSITIONS, CHUNK_WIDTH), lambda i: (i, 0))
    wspec = pl.BlockSpec((D_MODEL, D_MODEL), lambda i: (0, 0),
                         pipeline_mode=pl.Buffered(1))
    return pl.pallas_call(
        _mix_out_kernel,
        grid=(n // tm,),
        in_specs=[
            tile16,
            pl.BlockSpec((None, N_ADA, D_MODEL), lambda i: (i // tiles_per_seq, 0, 0)),
            pl.BlockSpec((tm, PACKED_PARTS * D_MODEL), lambda i: (i, 0)),
            tile,
            wspec, wspec, wspec,
        ],
        out_specs=tile16,
        out_shape=jax.ShapeDtypeStruct(h16.shape, F32),
        compiler_params=pltpu.CompilerParams(
            dimension_semantics=("parallel",), vmem_limit_bytes=VMEM_LIMIT),
        name="mixer_out_proj",
    )(h16, mod, pk, o, w_a, w_b, w_o)


def kernel(x, c, w_in, gmlp_ws, gmlp_b, hgrn_lb_logits, w_branch_a, w_branch_b, w_out,
           ffn_w13, ffn_w2, norm_w, ada_w, ada_b, final_norm_w):
    batch, seq, d = x.shape
    assert d == D_MODEL and seq % M1_TM == 0 and seq % FFN_TM == 0 and seq % M3_TM == 0
    n = batch * seq

    mod_all = _ada_mod(c, ada_w, ada_b).reshape(DEPTH, batch, N_ADA, D_MODEL)
    lb_all = _lower_bounds(hgrn_lb_logits)
    perm = jnp.array([_block_token(p) for p in range(BLOCK)], dtype=jnp.int32)
    ws_p = gmlp_ws[:, :, perm, :][:, :, :, perm]
    bs_p = jnp.swapaxes(gmlp_b[:, :, perm], 1, 2)

    h16 = x.reshape(n, D_MODEL)
    for l in range(DEPTH):
        mod = mod_all[l]
        nw = norm_w[l].reshape(3, 1, D_MODEL)
        h16 = _ffn(h16, mod, nw[0], ffn_w13[l, 0].astype(BF16), ffn_w2[l, 0].astype(BF16),
                   mod_off=0, seq=seq, in_natural=(l == 0))
        pk, o = _mix_in(
            h16, mod, nw[1], w_in[l].astype(BF16), ws_p[l], bs_p[l],
            lb_all[l].reshape(1, D_MODEL), batch=batch, seq=seq)
        h16 = _mix_out(h16, mod, pk, o, w_branch_a[l].astype(BF16),
                       w_branch_b[l].astype(BF16), w_out[l].astype(BF16), seq=seq)
        final_w = final_norm_w.reshape(1, D_MODEL) if l == DEPTH - 1 else None
        h16 = _ffn(h16, mod, nw[2], ffn_w13[l, 1].astype(BF16), ffn_w2[l, 1].astype(BF16),
                   mod_off=6, seq=seq, final_w=final_w)
    return h16.reshape(batch, seq, D_MODEL)
```

```python
import functools

import jax
import jax.numpy as jnp
from jax import lax
from jax.experimental import pallas as pl
from jax.experimental.pallas import tpu as pltpu

F32 = jnp.float32
BF16 = jnp.bfloat16

D_MODEL = 1024
DEPTH = 4
FFN_HIDDEN = 2816
N_ADA = 9
EPS = 1e-6
GMLP_GROUPS = 8
HEAD_DIM = 128
N_HEADS = D_MODEL // HEAD_DIM
N_PIECES = 8

BLOCK = 128
POSITIONS = 16
CHUNKS = BLOCK // POSITIONS
CHUNK_WIDTH = POSITIONS * D_MODEL

VMEM_LIMIT = 56 * 1024 * 1024


def _sigmoid(x):
    return 1.0 / (1.0 + jnp.exp(-x))


def _sigmoid_t(x):
    return 0.5 * jnp.tanh(0.5 * x) + 0.5


def _gelu_tanh(x):
    c = 0.7978845608028654
    return 0.5 * x * (1.0 + jnp.tanh(c * (x + 0.044715 * (x * x * x))))


def _rms_modulate(x, nw, scale, shift):
    ms = jnp.mean(x * x, axis=-1, keepdims=True)
    return (x * lax.rsqrt(ms + EPS) * nw) * (1.0 + scale) + shift


def _block_token(p):
    return (p % CHUNKS) * POSITIONS + p // CHUNKS


def _gather_block_rows(x_ref, b0, n_blocks):
    pieces = [x_ref[n * CHUNKS:(n + 1) * CHUNKS, j * D_MODEL:(j + 1) * D_MODEL]
              for n in range(b0, b0 + n_blocks) for j in range(POSITIONS)]
    return jnp.concatenate(pieces, axis=0)


def _scatter_block_rows(out_ref, y, b0, n_blocks):
    for n in range(n_blocks):
        for j in range(POSITIONS):
            r = (n * POSITIONS + j) * CHUNKS
            out_ref[(b0 + n) * CHUNKS:(b0 + n + 1) * CHUNKS,
                    j * D_MODEL:(j + 1) * D_MODEL] = y[r:r + CHUNKS, :]


def _ada_kernel(c_ref, w_ref, b_ref, o_ref):
    c = c_ref[...]
    ca = c * _sigmoid(c)
    o_ref[...] = jnp.dot(ca, w_ref[...], preferred_element_type=F32,
                         precision=lax.Precision.HIGHEST) + b_ref[...]


def _ada_mod(c, ada_w, ada_b):
    b = c.shape[0]
    return pl.pallas_call(
        _ada_kernel,
        grid=(DEPTH, N_ADA),
        in_specs=[
            pl.BlockSpec((b, D_MODEL), lambda l, j: (0, 0)),
            pl.BlockSpec((None, D_MODEL, D_MODEL), lambda l, j: (l, 0, j)),
            pl.BlockSpec((None, 1, D_MODEL), lambda l, j: (l, 0, j)),
        ],
        out_specs=pl.BlockSpec((None, b, D_MODEL), lambda l, j: (l, 0, j)),
        out_shape=jax.ShapeDtypeStruct((DEPTH, b, N_ADA * D_MODEL), F32),
        name="ada_mod",
    )(c, ada_w, ada_b.reshape(DEPTH, 1, N_ADA * D_MODEL))


def _lb_kernel(x_ref, o_ref):
    x = x_ref[...]
    rows = [x[i:i + 1, :] for i in range(DEPTH)]
    m = rows[0]
    for r in rows[1:]:
        m = jnp.maximum(m, r)
    e = [jnp.exp(r - m) for r in rows]
    tot = e[0]
    for t in e[1:]:
        tot = tot + t
    p = [t / tot for t in e]
    cum = p[0]
    first = cum
    for i in range(DEPTH):
        if i > 0:
            cum = cum + p[i]
        o_ref[i:i + 1, :] = cum - first


def _lower_bounds(logits):
    return pl.pallas_call(
        _lb_kernel,
        out_shape=jax.ShapeDtypeStruct(logits.shape, F32),
        name="hgrn_lower_bounds",
    )(logits.astype(F32))


FFN_TM = 1024
FFN_CHAINS = 4
FFN_COL_CHUNKS = ((0, 1024), (1024, 2048), (2048, FFN_HIDDEN))


def _ffn_kernel(x_ref, mod_ref, nw_ref, w13_ref, w2_ref, *rest, mod_off, final, in_natural):
    if final:
        fnw_ref, o_ref = rest
    else:
        (o_ref,) = rest
    shift = mod_ref[mod_off:mod_off + 1, :]
    scale = mod_ref[mod_off + 1:mod_off + 2, :]
    gate = mod_ref[mod_off + 2:mod_off + 3, :]
    rows = FFN_TM // POSITIONS // FFN_CHAINS
    for r in range(FFN_CHAINS):
        rs = slice(r * rows, (r + 1) * rows)
        if in_natural:
            nat = x_ref[r * rows * POSITIONS:(r + 1) * rows * POSITIONS, :]
            x = pltpu.einshape("(cj)d->(jc)d", nat, j=POSITIONS)
        else:
            x = jnp.concatenate(
                [x_ref[rs, j * D_MODEL:(j + 1) * D_MODEL] for j in range(POSITIONS)], axis=0)
        hb = _rms_modulate(x, nw_ref[...], scale, shift).astype(BF16)
        acc = None
        for c0, c1 in FFN_COL_CHUNKS:
            a = jnp.dot(hb, w13_ref[:, c0:c1], preferred_element_type=F32)
            b = jnp.dot(hb, w13_ref[:, FFN_HIDDEN + c0:FFN_HIDDEN + c1],
                        preferred_element_type=F32)
            act = (a * _sigmoid_t(a) * b).astype(BF16)
            part = jnp.dot(act, w2_ref[c0:c1, :], preferred_element_type=F32)
            acc = part if acc is None else acc + part
        y = x + (0.5 * (1.0 + gate)) * acc
        if final:
            ms = jnp.mean(y * y, axis=-1, keepdims=True)
            y = y * lax.rsqrt(ms + EPS) * fnw_ref[...]
            o_ref[r * rows * POSITIONS:(r + 1) * rows * POSITIONS, :] = pltpu.einshape(
                "(jc)d->(cj)d", y, j=POSITIONS)
        else:
            for j in range(POSITIONS):
                o_ref[rs, j * D_MODEL:(j + 1) * D_MODEL] = y[j * rows:(j + 1) * rows, :]


def _ffn(h_in, mod, nw, w13, w2, *, layer, half, seq, final_w=None, in_natural=False):
    mod_off = 6 * half
    n = h_in.shape[0] if in_natural else h_in.shape[0] * POSITIONS
    tm = FFN_TM
    tiles_per_seq = seq // tm
    final = final_w is not None
    tile16 = pl.BlockSpec((tm // POSITIONS, CHUNK_WIDTH), lambda i: (i, 0))
    in_specs = [
        pl.BlockSpec((tm, D_MODEL), lambda i: (i, 0)) if in_natural else tile16,
        pl.BlockSpec((None, N_ADA, D_MODEL), lambda i: (i // tiles_per_seq, 0, 0)),
        pl.BlockSpec((1, D_MODEL), lambda i: (0, 0)),
        pl.BlockSpec((None, None, D_MODEL, 2 * FFN_HIDDEN), lambda i: (layer, half, 0, 0),
                     pipeline_mode=pl.Buffered(1)),
        pl.BlockSpec((None, None, FFN_HIDDEN, D_MODEL), lambda i: (layer, half, 0, 0),
                     pipeline_mode=pl.Buffered(1)),
    ]
    args = [h_in, mod, nw, w13, w2]
    if final:
        in_specs.append(pl.BlockSpec((1, D_MODEL), lambda i: (0, 0)))
        args.append(final_w)
    return pl.pallas_call(
        functools.partial(_ffn_kernel, mod_off=mod_off, final=final, in_natural=in_natural),
        grid=(n // tm,),
        in_specs=in_specs,
        out_specs=pl.BlockSpec((tm, D_MODEL), lambda i: (i, 0)) if final else tile16,
        out_shape=jax.ShapeDtypeStruct(
            (n, D_MODEL) if final else (n // POSITIONS, CHUNK_WIDTH), F32),
        compiler_params=pltpu.CompilerParams(
            dimension_semantics=("parallel",), vmem_limit_bytes=VMEM_LIMIT),
        name="ffn_half_step",
    )(*args)


M1_TM = 512
M1_PIECES = 2
M1_CHAINS = 2
PACKED_PARTS = 4
N_MASKS = 7


def _seg_scan_rows(x, rowid, width, reverse):
    d = 1
    y = x
    while d < width:
        if reverse:
            shifted = pltpu.roll(y, CHUNKS - d, axis=0)
            ok = (rowid & (width - 1)) < width - d
        else:
            shifted = pltpu.roll(y, d, axis=0)
            ok = (rowid & (width - 1)) >= d
        y = y * jnp.where(ok, shifted, 1.0)
        d *= 2
    return y


def _write_level_masks(mask_ref):
    pr = lax.broadcasted_iota(jnp.int32, (BLOCK, BLOCK), 0)
    pc = lax.broadcasted_iota(jnp.int32, (BLOCK, BLOCK), 1)
    same_chunk = ((pr ^ pc) & (CHUNKS - 1)) == 0
    mask_ref[0] = jnp.where(pr == pc, 1.0, 0.0)
    for beta in range(4):
        sh = 3 + beta + 1
        mask_ref[1 + beta] = jnp.where(same_chunk & ((pr >> sh) == (pc >> sh)), 1.0, 0.0)
    for gamma in range(2):
        sh = gamma + 1
        mask_ref[5 + gamma] = jnp.where(
            ((pr & (CHUNKS - 1)) >> sh) == ((pc & (CHUNKS - 1)) >> sh), 1.0, 0.0)


N_LEVELS = 7
HGRN_STAGED_BLOCKS = 1
LEVEL_TARGETS = (
    [[j for j in range(POSITIONS) if j & (1 << beta)] for beta in range(4)]
    + [list(range(POSITIONS))] * 3)
LEVEL_MASK = (1, 2, 3, 4, 5, 6, None)


def _hgrn_staged(blocks, st_ref, mask_ref, qs_scr, ks_scr, p_scr, q2_scr, tot_scr, o_ref):
    n_units = len(blocks) * N_HEADS
    rowid = lax.broadcasted_iota(jnp.int32, (CHUNKS, HEAD_DIM), 0)
    zero = jnp.zeros((CHUNKS, HEAD_DIM), F32)

    def slabs(val):
        return [val[j * CHUNKS:(j + 1) * CHUNKS, :] for j in range(POSITIONS)]

    def cat(sl):
        return jnp.concatenate(sl, axis=0)

    def mask_rows(level, j):
        return mask_ref[level, j * CHUNKS:(j + 1) * CHUNKS, :]

    for u in range(n_units):
        h = u % N_HEADS
        f, q, k, _, _ = blocks[u // N_HEADS]
        ls = slice(h * HEAD_DIM, (h + 1) * HEAD_DIM)
        fs, qs, ks = slabs(f[:, ls]), slabs(q[:, ls]), slabs(k[:, ls])
        for j in range(POSITIONS):
            p_scr[u, j * CHUNKS:(j + 1) * CHUNKS, :] = (
                jnp.sum(qs[j] * ks[j], axis=-1, keepdims=True) * mask_rows(0, j))

        for beta in range(4):
            w = 1 << beta
            a = [None] * POSITIONS
            for j in range(POSITIONS):
                if j & w:
                    a[j] = fs[j] if (j & (w - 1)) == 0 else a[j - 1] * fs[j]
            b = [None] * POSITIONS
            for j in range(POSITIONS - 1, -1, -1):
                if not (j & w) and (j & (w - 1)) != w - 1:
                    b[j] = fs[j + 1] if b[j + 1] is None else b[j + 1] * fs[j + 1]
            upper = LEVEL_TARGETS[beta]
            qs_scr[u, beta, 0:len(upper) * CHUNKS, :] = cat(
                [qs[j] * a[j] for j in upper]).astype(BF16)
            ks_scr[u, beta] = cat(
                [zero if (j & w) else (ks[j] if b[j] is None else ks[j] * b[j])
                 for j in range(POSITIONS)])

        gcum = [fs[0]]
        for j in range(1, POSITIONS):
            gcum.append(gcum[-1] * fs[j])
        hsuf = [None] * POSITIONS
        for j in range(POSITIONS - 2, -1, -1):
            hsuf[j] = fs[j + 1] if hsuf[j + 1] is None else hsuf[j + 1] * fs[j + 1]
        g = gcum[-1]
        qg = [qs[j] * gcum[j] for j in range(POSITIONS)]
        kh = [ks[j] if hsuf[j] is None else ks[j] * hsuf[j] for j in range(POSITIONS)]
        g_prev = pltpu.roll(g, 1, axis=0)
        g_next = pltpu.roll(g, CHUNKS - 1, axis=0)

        for gamma in range(3):
            w = 1 << gamma
            low = rowid & (w - 1)
            ca = _seg_scan_rows(jnp.where(low == 0, 1.0, g_prev), rowid, w, False)
            ca = jnp.where((rowid & w) != 0, ca, 0.0)
            cb = _seg_scan_rows(jnp.where(low == w - 1, 1.0, g_next), rowid, w, True)
            cb = jnp.where((rowid & w) == 0, cb, 0.0)
            qs_scr[u, 4 + gamma] = cat([x * ca for x in qg]).astype(BF16)
            ks_scr[u, 4 + gamma] = cat([x * cb for x in kh])

        cp = _seg_scan_rows(jnp.where(rowid == 0, 1.0, g_prev), rowid, CHUNKS, False)
        cs = _seg_scan_rows(jnp.where(rowid == CHUNKS - 1, 1.0, g_next), rowid, CHUNKS, True)
        q2_scr[u] = cat([x * cp for x in qg]).astype(BF16)
        ks_scr[u, N_LEVELS] = cat([x * cs for x in kh])
        tot_scr[u] = jnp.broadcast_to((cp * g)[CHUNKS - 1:CHUNKS, :], (CHUNKS, HEAD_DIM))

    for u in range(n_units):
        for lvl in range(N_LEVELS):
            targets = LEVEL_TARGETS[lvl]
            kt = jnp.transpose(ks_scr[u, lvl]).astype(BF16)
            s = jnp.dot(qs_scr[u, lvl, 0:len(targets) * CHUNKS, :], kt,
                        preferred_element_type=F32)
            for i, j in enumerate(targets):
                rs = slice(j * CHUNKS, (j + 1) * CHUNKS)
                sj = s[i * CHUNKS:(i + 1) * CHUNKS, :]
                if LEVEL_MASK[lvl] is not None:
                    sj = sj * mask_rows(LEVEL_MASK[lvl], j)
                p_scr[u, rs, :] = p_scr[u, rs, :] + sj

    for u in range(n_units):
        h = u % N_HEADS
        _, _, _, vb, orows = blocks[u // N_HEADS]
        ls = slice(h * HEAD_DIM, (h + 1) * HEAD_DIM)
        st = st_ref[h]
        lhs = jnp.concatenate([p_scr[u].astype(BF16), q2_scr[u]], axis=1)
        rhs = jnp.concatenate([vb[:, ls], st.astype(BF16)], axis=0)
        o_ref[orows, ls] = jnp.dot(lhs, rhs, preferred_element_type=F32)
        k2t = jnp.transpose(ks_scr[u, N_LEVELS]).astype(BF16)
        upd = jnp.dot(k2t, vb[:, ls], preferred_element_type=F32)
        decay_col = jnp.transpose(jnp.broadcast_to(tot_scr[u, 0:1, :], (HEAD_DIM, HEAD_DIM)))
        st_ref[h] = st * decay_col + upd


def _mix_in_kernel(x_ref, mod_ref, nw_ref, win_ref, ws_ref, bs_ref, lb_ref,
                   pk_ref, o_ref,
                   gu_scr, st_ref, mask_ref, qs_scr, ks_scr, p_scr, q2_scr, tot_scr):
    yap_ref, sog_ref, sga_ref, sgb_ref = [
        pk_ref.at[:, i * D_MODEL:(i + 1) * D_MODEL] for i in range(PACKED_PARTS)]
    tm = pk_ref.shape[0]
    nblk = tm // BLOCK // M1_CHAINS

    @pl.when(pl.program_id(1) == 0)
    def _():
        st_ref[...] = jnp.zeros_like(st_ref)

    _write_level_masks(mask_ref)
    shift = mod_ref[3:4, :]
    scale = mod_ref[4:5, :]
    row = _block_token(lax.broadcasted_iota(jnp.int32, (BLOCK, BLOCK), 0))
    col = _block_token(lax.broadcasted_iota(jnp.int32, (BLOCK, BLOCK), 1))
    causal = row >= col
    lb = lb_ref[...]

    def project_stage(chain, res):
        b0 = chain * nblk
        rows = slice(b0 * BLOCK, (b0 + nblk) * BLOCK)
        x = _gather_block_rows(x_ref, b0, nblk)
        hb = _rms_modulate(x, nw_ref[...], scale, shift).astype(BF16)

        def project(p0):
            z = jnp.dot(hb, win_ref[:, p0 * D_MODEL:(p0 + M1_PIECES) * D_MODEL],
                        preferred_element_type=F32)
            return [z[:, i * D_MODEL:(i + 1) * D_MODEL] for i in range(M1_PIECES)]

        zu, zv = project(0)
        gu_scr[rows, :] = _gelu_tanh(zu)
        res["gv"] = _gelu_tanh(zv)
        yield
        zq, zf = project(2)
        res["q"] = zq * _sigmoid_t(zq)
        th = 0.5 * jnp.tanh(0.5 * zf)
        res["f"] = lb + (1.0 - lb) * (0.5 + th)
        res["k"] = (1.0 - lb) * (0.5 - th)
        yield
        zi, zog = project(4)
        res["vb"] = zi.astype(BF16)
        sog_ref[rows, :] = _sigmoid_t(zog).astype(BF16)
        yield
        zga, zgb = project(6)
        sga_ref[rows, :] = _sigmoid_t(zga).astype(BF16)
        sgb_ref[rows, :] = _sigmoid_t(zgb).astype(BF16)
        yield

    def gmlp_stage(chain, res):
        b0 = chain * nblk
        gv = res["gv"]
        for g in range(GMLP_GROUPS):
            ls = slice(g * HEAD_DIM, (g + 1) * HEAD_DIM)
            xg = gv[:, ls]
            mu = jnp.mean(xg, axis=-1, keepdims=True)
            xc = xg - mu
            var = jnp.mean(xc * xc, axis=-1, keepdims=True)
            vn = (xc * lax.rsqrt(var + EPS)).astype(BF16)
            rhs = jnp.concatenate(
                [vn[n * BLOCK:(n + 1) * BLOCK, :] for n in range(nblk)], axis=1)
            wc = jnp.where(causal, ws_ref[g], 0.0).astype(BF16)
            mixed = jnp.dot(wc, rhs, preferred_element_type=F32)
            bias = jnp.broadcast_to(bs_ref[:, g:g + 1], (BLOCK, BLOCK))
            for n in range(nblk):
                rs = slice((b0 + n) * BLOCK, (b0 + n + 1) * BLOCK)
                mx = mixed[:, n * BLOCK:(n + 1) * BLOCK] + bias
                yap_ref[rs, ls] = (gu_scr[rs, ls] * mx).astype(BF16)

    def hgrn_stage(chain, res):
        b0 = chain * nblk
        for n in range(0, nblk, HGRN_STAGED_BLOCKS):
            blocks = []
            for m in range(n, n + HGRN_STAGED_BLOCKS):
                br = slice(m * BLOCK, (m + 1) * BLOCK)
                orows = slice((b0 + m) * BLOCK, (b0 + m + 1) * BLOCK)
                blocks.append((res["f"][br, :], res["q"][br, :], res["k"][br, :],
                               res["vb"][br, :], orows))
            _hgrn_staged(blocks, st_ref, mask_ref, qs_scr, ks_scr, p_scr, q2_scr, tot_scr, o_ref)

    for chain in range(M1_CHAINS):
        res = {}
        for _ in project_stage(chain, res):
            pass
        gmlp_stage(chain, res)
        hgrn_stage(chain, res)


def _mix_in(h16, mod, nw, w_in, ws_p, bs_p, lb, *, layer, batch, seq):
    n = h16.shape[0] * POSITIONS
    tm = M1_TM
    tiles_per_seq = seq // tm
    tile = pl.BlockSpec((tm, D_MODEL), lambda b, t: (b * tiles_per_seq + t, 0))
    const2 = lambda b, t: (0, 0)
    f32_out = jax.ShapeDtypeStruct((n, D_MODEL), F32)
    bf_out = jax.ShapeDtypeStruct((n, PACKED_PARTS * D_MODEL), BF16)
    units = HGRN_STAGED_BLOCKS * N_HEADS
    return pl.pallas_call(
        _mix_in_kernel,
        grid=(batch, tiles_per_seq),
        in_specs=[
            pl.BlockSpec((tm // POSITIONS, CHUNK_WIDTH), lambda b, t: (b * tiles_per_seq + t, 0)),
            pl.BlockSpec((None, N_ADA, D_MODEL), lambda b, t: (b, 0, 0)),
            pl.BlockSpec((1, D_MODEL), const2),
            pl.BlockSpec((None, D_MODEL, N_PIECES * D_MODEL), lambda b, t: (layer, 0, 0),
                         pipeline_mode=pl.Buffered(1)),
            pl.BlockSpec((GMLP_GROUPS, BLOCK, BLOCK), lambda b, t: (0, 0, 0)),
            pl.BlockSpec((BLOCK, GMLP_GROUPS), const2),
            pl.BlockSpec((1, D_MODEL), const2),
        ],
        out_specs=[pl.BlockSpec((tm, PACKED_PARTS * D_MODEL),
                                lambda b, t: (b * tiles_per_seq + t, 0)), tile],
        out_shape=[bf_out, f32_out],
        scratch_shapes=[
            pltpu.VMEM((tm, D_MODEL), F32),
            pltpu.VMEM((N_HEADS, HEAD_DIM, HEAD_DIM), F32),
            pltpu.VMEM((N_MASKS, BLOCK, BLOCK), F32),
            pltpu.VMEM((units, N_LEVELS, BLOCK, HEAD_DIM), BF16),
            pltpu.VMEM((units, N_LEVELS + 1, BLOCK, HEAD_DIM), F32),
            pltpu.VMEM((units, BLOCK, BLOCK), F32),
            pltpu.VMEM((units, BLOCK, HEAD_DIM), BF16),
            pltpu.VMEM((units, CHUNKS, HEAD_DIM), F32),
        ],
        compiler_params=pltpu.CompilerParams(
            dimension_semantics=("parallel", "arbitrary"), vmem_limit_bytes=VMEM_LIMIT),
        name="mixer_in_proj_hgrn2",
    )(h16, mod, nw, w_in, ws_p, bs_p, lb)


M3_TM = 1024
M3_CHAINS = 2


def _mix_out_kernel(x_ref, mod_ref, pk_ref, o_ref, wa_ref, wb_ref, wo_ref, out_ref):
    yap_ref, sog_ref, sga_ref, sgb_ref = [
        pk_ref.at[:, i * D_MODEL:(i + 1) * D_MODEL] for i in range(PACKED_PARTS)]
    nblk = pk_ref.shape[0] // BLOCK // M3_CHAINS
    gate = mod_ref[5:6, :]
    for chain in range(M3_CHAINS):
        b0 = chain * nblk
        rows = slice(b0 * BLOCK, (b0 + nblk) * BLOCK)
        o = o_ref[rows, :]
        parts = []
        for h in range(N_HEADS):
            oh = o[:, h * HEAD_DIM:(h + 1) * HEAD_DIM]
            ms = jnp.mean(oh * oh, axis=-1, keepdims=True)
            parts.append(oh * lax.rsqrt(ms + EPS))
        on = (jnp.concatenate(parts, axis=1) * sog_ref[rows, :].astype(F32)).astype(BF16)
        ya = jnp.dot(yap_ref[rows, :], wa_ref[...], preferred_element_type=F32)
        yb = jnp.dot(on, wb_ref[...], preferred_element_type=F32)
        m = (sga_ref[rows, :].astype(F32) * ya + sgb_ref[rows, :].astype(F32) * yb).astype(BF16)
        out = jnp.dot(m, wo_ref[...], preferred_element_type=F32)
        y = _gather_block_rows(x_ref, b0, nblk) + (1.0 + gate) * out
        _scatter_block_rows(out_ref, y, b0, nblk)


def _mix_out(h16, mod, pk, o, w_a, w_b, w_o, *, layer, seq):
    n = h16.shape[0] * POSITIONS
    tm = M3_TM
    tiles_per_seq = seq // tm
    tile = pl.BlockSpec((tm, D_MODEL), lambda i: (i, 0))
    tile16 = pl.BlockSpec((tm // POSITIONS, CHUNK_WIDTH), lambda i: (i, 0))
    wspec = pl.BlockSpec((None, D_MODEL, D_MODEL), lambda i: (layer, 0, 0),
                         pipeline_mode=pl.Buffered(1))
    return pl.pallas_call(
        _mix_out_kernel,
        grid=(n // tm,),
        in_specs=[
            tile16,
            pl.BlockSpec((None, N_ADA, D_MODEL), lambda i: (i // tiles_per_seq, 0, 0)),
            pl.BlockSpec((tm, PACKED_PARTS * D_MODEL), lambda i: (i, 0)),
            tile,
            wspec, wspec, wspec,
        ],
        out_specs=tile16,
        out_shape=jax.ShapeDtypeStruct(h16.shape, F32),
        compiler_params=pltpu.CompilerParams(
            dimension_semantics=("parallel",), vmem_limit_bytes=VMEM_LIMIT),
        name="mixer_out_proj",
    )(h16, mod, pk, o, w_a, w_b, w_o)


def kernel(x, c, w_in, gmlp_ws, gmlp_b, hgrn_lb_logits, w_branch_a, w_branch_b, w_out,
           ffn_w13, ffn_w2, norm_w, ada_w, ada_b, final_norm_w):
    batch, seq, d = x.shape
    assert d == D_MODEL and seq % M1_TM == 0 and seq % FFN_TM == 0 and seq % M3_TM == 0
    n = batch * seq

    mod_all = _ada_mod(c, ada_w, ada_b).reshape(DEPTH, batch, N_ADA, D_MODEL)
    lb_all = _lower_bounds(hgrn_lb_logits)
    perm = jnp.array([_block_token(p) for p in range(BLOCK)], dtype=jnp.int32)
    ws_p = gmlp_ws[:, :, perm, :][:, :, :, perm]
    bs_p = jnp.swapaxes(gmlp_b[:, :, perm], 1, 2)

    w13_b, w2_b, w_in_b = ffn_w13.astype(BF16), ffn_w2.astype(BF16), w_in.astype(BF16)
    w_a_b, w_b_b, w_o_b = (w_branch_a.astype(BF16), w_branch_b.astype(BF16),
                           w_out.astype(BF16))

    h16 = x.reshape(n, D_MODEL)
    for l in range(DEPTH):
        mod = mod_all[l]
        nw = norm_w[l].reshape(3, 1, D_MODEL)
        h16 = _ffn(h16, mod, nw[0], w13_b, w2_b, layer=l, half=0, seq=seq,
                   in_natural=(l == 0))
        pk, o = _mix_in(h16, mod, nw[1], w_in_b, ws_p[l], bs_p[l],
                        lb_all[l].reshape(1, D_MODEL), layer=l, batch=batch, seq=seq)
        h16 = _mix_out(h16, mod, pk, o, w_a_b, w_b_b, w_o_b, layer=l, seq=seq)
        final_w = final_norm_w.reshape(1, D_MODEL) if l == DEPTH - 1 else None
        h16 = _ffn(h16, mod, nw[2], w13_b, w2_b, layer=l, half=1, seq=seq, final_w=final_w)
    return h16.reshape(batch, seq, D_MODEL)
```

```python
import functools

import jax
import jax.numpy as jnp
from jax import lax
from jax.experimental import pallas as pl
from jax.experimental.pallas import tpu as pltpu

F32 = jnp.float32
BF16 = jnp.bfloat16

D_MODEL = 1024
DEPTH = 4
FFN_HIDDEN = 2816
N_ADA = 9
EPS = 1e-6
GMLP_GROUPS = 8
HEAD_DIM = 128
N_HEADS = D_MODEL // HEAD_DIM
N_PIECES = 8

BLOCK = 128
POSITIONS = 16
CHUNKS = BLOCK // POSITIONS
CHUNK_WIDTH = POSITIONS * D_MODEL

VMEM_LIMIT = 56 * 1024 * 1024


def _sigmoid(x):
    return 1.0 / (1.0 + jnp.exp(-x))


def _sigmoid_t(x):
    return 0.5 * jnp.tanh(0.5 * x) + 0.5


def _gelu_tanh(x):
    c = 0.7978845608028654
    hx = 0.5 * x
    return hx + hx * jnp.tanh(x * (c + (c * 0.044715) * (x * x)))


def _rms_modulate(x, nw, scale, shift):
    ms = jnp.mean(x * x, axis=-1, keepdims=True)
    return (x * lax.rsqrt(ms + EPS) * nw) * (1.0 + scale) + shift


def _block_token(p):
    return (p % CHUNKS) * POSITIONS + p // CHUNKS


def _gather_block_rows(x_ref, b0, n_blocks):
    pieces = [x_ref[n * CHUNKS:(n + 1) * CHUNKS, j * D_MODEL:(j + 1) * D_MODEL]
              for n in range(b0, b0 + n_blocks) for j in range(POSITIONS)]
    return jnp.concatenate(pieces, axis=0)


def _scatter_block_rows(out_ref, y, b0, n_blocks):
    for n in range(n_blocks):
        for j in range(POSITIONS):
            r = (n * POSITIONS + j) * CHUNKS
            out_ref[(b0 + n) * CHUNKS:(b0 + n + 1) * CHUNKS,
                    j * D_MODEL:(j + 1) * D_MODEL] = y[r:r + CHUNKS, :]


def _ada_kernel(c_ref, w_ref, b_ref, o_ref):
    c = c_ref[...]
    ca = c * _sigmoid(c)
    o_ref[...] = jnp.dot(ca, w_ref[...], preferred_element_type=F32,
                         precision=lax.Precision.HIGHEST) + b_ref[...]


ADA_GROUP = 3


def _ada_mod(c, ada_w, ada_b):
    b = c.shape[0]
    return pl.pallas_call(
        _ada_kernel,
        grid=(DEPTH, N_ADA // ADA_GROUP),
        in_specs=[
            pl.BlockSpec((b, D_MODEL), lambda l, j: (0, 0)),
            pl.BlockSpec((None, D_MODEL, ADA_GROUP * D_MODEL), lambda l, j: (l, 0, j)),
            pl.BlockSpec((None, 1, ADA_GROUP * D_MODEL), lambda l, j: (l, 0, j)),
        ],
        out_specs=pl.BlockSpec((None, b, ADA_GROUP * D_MODEL), lambda l, j: (l, 0, j)),
        out_shape=jax.ShapeDtypeStruct((DEPTH, b, N_ADA * D_MODEL), F32),
        compiler_params=pltpu.CompilerParams(
            dimension_semantics=("parallel", "parallel"), vmem_limit_bytes=VMEM_LIMIT),
        name="ada_mod",
    )(c, ada_w, ada_b.reshape(DEPTH, 1, N_ADA * D_MODEL))


def _lb_kernel(x_ref, o_ref):
    x = x_ref[...]
    rows = [x[i:i + 1, :] for i in range(DEPTH)]
    m = rows[0]
    for r in rows[1:]:
        m = jnp.maximum(m, r)
    e = [jnp.exp(r - m) for r in rows]
    tot = e[0]
    for t in e[1:]:
        tot = tot + t
    p = [t / tot for t in e]
    cum = p[0]
    first = cum
    for i in range(DEPTH):
        if i > 0:
            cum = cum + p[i]
        o_ref[i:i + 1, :] = cum - first


def _lower_bounds(logits):
    return pl.pallas_call(
        _lb_kernel,
        out_shape=jax.ShapeDtypeStruct(logits.shape, F32),
        name="hgrn_lower_bounds",
    )(logits.astype(F32))


FFN_TM = 1024
FFN_CHAINS = 4
FFN_COL_CHUNKS = ((0, 1024), (1024, 2048), (2048, FFN_HIDDEN))


def _ffn_kernel(x_ref, mod_ref, nw_ref, w13_ref, w2_ref, *rest, mod_off, final, in_natural):
    if final:
        fnw_ref, o_ref = rest
    else:
        (o_ref,) = rest
    shift = mod_ref[mod_off:mod_off + 1, :]
    scale = mod_ref[mod_off + 1:mod_off + 2, :]
    gate = mod_ref[mod_off + 2:mod_off + 3, :]
    rows = FFN_TM // POSITIONS // FFN_CHAINS
    for r in range(FFN_CHAINS):
        rs = slice(r * rows, (r + 1) * rows)
        if in_natural:
            nat = x_ref[r * rows * POSITIONS:(r + 1) * rows * POSITIONS, :]
            x = pltpu.einshape("(cj)d->(jc)d", nat, j=POSITIONS)
        else:
            x = jnp.concatenate(
                [x_ref[rs, j * D_MODEL:(j + 1) * D_MODEL] for j in range(POSITIONS)], axis=0)
        hb = _rms_modulate(x, nw_ref[...], scale, shift).astype(BF16)
        acc = None
        for c0, c1 in FFN_COL_CHUNKS:
            a = jnp.dot(hb, w13_ref[:, c0:c1], preferred_element_type=F32)
            b = jnp.dot(hb, w13_ref[:, FFN_HIDDEN + c0:FFN_HIDDEN + c1],
                        preferred_element_type=F32)
            act = (a * _sigmoid_t(a) * b).astype(BF16)
            part = jnp.dot(act, w2_ref[c0:c1, :], preferred_element_type=F32)
            acc = part if acc is None else acc + part
        y = x + (0.5 * (1.0 + gate)) * acc
        if final:
            ms = jnp.mean(y * y, axis=-1, keepdims=True)
            y = y * lax.rsqrt(ms + EPS) * fnw_ref[...]
            o_ref[r * rows * POSITIONS:(r + 1) * rows * POSITIONS, :] = pltpu.einshape(
                "(jc)d->(cj)d", y, j=POSITIONS)
        else:
            for j in range(POSITIONS):
                o_ref[rs, j * D_MODEL:(j + 1) * D_MODEL] = y[j * rows:(j + 1) * rows, :]


def _ffn(h_in, mod, nw, w13, w2, *, layer, half, seq, final_w=None, in_natural=False):
    mod_off = 6 * half
    n = h_in.shape[0] if in_natural else h_in.shape[0] * POSITIONS
    tm = FFN_TM
    tiles_per_seq = seq // tm
    final = final_w is not None
    tile16 = pl.BlockSpec((tm // POSITIONS, CHUNK_WIDTH), lambda i: (i, 0))
    in_specs = [
        pl.BlockSpec((tm, D_MODEL), lambda i: (i, 0)) if in_natural else tile16,
        pl.BlockSpec((None, N_ADA, D_MODEL), lambda i: (i // tiles_per_seq, 0, 0)),
        pl.BlockSpec((1, D_MODEL), lambda i: (0, 0)),
        pl.BlockSpec((None, None, D_MODEL, 2 * FFN_HIDDEN), lambda i: (layer, half, 0, 0),
                     pipeline_mode=pl.Buffered(1)),
        pl.BlockSpec((None, None, FFN_HIDDEN, D_MODEL), lambda i: (layer, half, 0, 0),
                     pipeline_mode=pl.Buffered(1)),
    ]
    args = [h_in, mod, nw, w13, w2]
    if final:
        in_specs.append(pl.BlockSpec((1, D_MODEL), lambda i: (0, 0)))
        args.append(final_w)
    return pl.pallas_call(
        functools.partial(_ffn_kernel, mod_off=mod_off, final=final, in_natural=in_natural),
        grid=(n // tm,),
        in_specs=in_specs,
        out_specs=pl.BlockSpec((tm, D_MODEL), lambda i: (i, 0)) if final else tile16,
        out_shape=jax.ShapeDtypeStruct(
            (n, D_MODEL) if final else (n // POSITIONS, CHUNK_WIDTH), F32),
        compiler_params=pltpu.CompilerParams(
            dimension_semantics=("parallel",), vmem_limit_bytes=VMEM_LIMIT),
        name="ffn_half_step",
    )(*args)


M1_TM = 512
M1_PIECES = 2
M1_CHAINS = 2
PACKED_PARTS = 4
N_MASKS = 7


def _seg_scan_rows(x, rowid, width, reverse):
    d = 1
    y = x
    while d < width:
        if reverse:
            shifted = pltpu.roll(y, CHUNKS - d, axis=0)
            ok = (rowid & (width - 1)) < width - d
        else:
            shifted = pltpu.roll(y, d, axis=0)
            ok = (rowid & (width - 1)) >= d
        y = y * jnp.where(ok, shifted, 1.0)
        d *= 2
    return y


def _write_level_masks(mask_ref):
    pr = lax.broadcasted_iota(jnp.int32, (BLOCK, BLOCK), 0)
    pc = lax.broadcasted_iota(jnp.int32, (BLOCK, BLOCK), 1)
    same_chunk = ((pr ^ pc) & (CHUNKS - 1)) == 0
    mask_ref[0] = jnp.where(pr == pc, 1.0, 0.0)
    for beta in range(4):
        sh = 3 + beta + 1
        mask_ref[1 + beta] = jnp.where(same_chunk & ((pr >> sh) == (pc >> sh)), 1.0, 0.0)
    for gamma in range(2):
        sh = gamma + 1
        mask_ref[5 + gamma] = jnp.where(
            ((pr & (CHUNKS - 1)) >> sh) == ((pc & (CHUNKS - 1)) >> sh), 1.0, 0.0)


N_LEVELS = 7
HGRN_STAGED_BLOCKS = 1
LEVEL_TARGETS = (
    [[j for j in range(POSITIONS) if j & (1 << beta)] for beta in range(4)]
    + [list(range(POSITIONS))] * 3)
LEVEL_MASK = (1, 2, 3, 4, 5, 6, None)


def _hgrn_staged(blocks, st_ref, mask_ref, qs_scr, ks_scr, p_scr, q2_scr, tot_scr, o_ref):
    n_units = len(blocks) * N_HEADS
    rowid = lax.broadcasted_iota(jnp.int32, (CHUNKS, HEAD_DIM), 0)
    zero = jnp.zeros((CHUNKS, HEAD_DIM), F32)

    def slabs(val):
        return [val[j * CHUNKS:(j + 1) * CHUNKS, :] for j in range(POSITIONS)]

    def cat(sl):
        return jnp.concatenate(sl, axis=0)

    def mask_rows(level, j):
        return mask_ref[level, j * CHUNKS:(j + 1) * CHUNKS, :]

    for u in range(n_units):
        h = u % N_HEADS
        f, q, k, _, _ = blocks[u // N_HEADS]
        ls = slice(h * HEAD_DIM, (h + 1) * HEAD_DIM)
        fs, qs, ks = slabs(f[:, ls]), slabs(q[:, ls]), slabs(k[:, ls])
        for j in range(POSITIONS):
            p_scr[u, j * CHUNKS:(j + 1) * CHUNKS, :] = (
                jnp.sum(qs[j] * ks[j], axis=-1, keepdims=True) * mask_rows(0, j))

        for beta in range(4):
            w = 1 << beta
            a = [None] * POSITIONS
            for j in range(POSITIONS):
                if j & w:
                    a[j] = fs[j] if (j & (w - 1)) == 0 else a[j - 1] * fs[j]
            b = [None] * POSITIONS
            for j in range(POSITIONS - 1, -1, -1):
                if not (j & w) and (j & (w - 1)) != w - 1:
                    b[j] = fs[j + 1] if b[j + 1] is None else b[j + 1] * fs[j + 1]
            upper = LEVEL_TARGETS[beta]
            qs_scr[u, beta, 0:len(upper) * CHUNKS, :] = cat(
                [qs[j] * a[j] for j in upper]).astype(BF16)
            ks_scr[u, beta] = cat(
                [zero if (j & w) else (ks[j] if b[j] is None else ks[j] * b[j])
                 for j in range(POSITIONS)])

        gcum = [fs[0]]
        for j in range(1, POSITIONS):
            gcum.append(gcum[-1] * fs[j])
        hsuf = [None] * POSITIONS
        for j in range(POSITIONS - 2, -1, -1):
            hsuf[j] = fs[j + 1] if hsuf[j + 1] is None else hsuf[j + 1] * fs[j + 1]
        g = gcum[-1]
        qg = [qs[j] * gcum[j] for j in range(POSITIONS)]
        kh = [ks[j] if hsuf[j] is None else ks[j] * hsuf[j] for j in range(POSITIONS)]
        g_prev = pltpu.roll(g, 1, axis=0)
        g_next = pltpu.roll(g, CHUNKS - 1, axis=0)

        for gamma in range(3):
            w = 1 << gamma
            low = rowid & (w - 1)
            ca = _seg_scan_rows(jnp.where(low == 0, 1.0, g_prev), rowid, w, False)
            ca = jnp.where((rowid & w) != 0, ca, 0.0)
            cb = _seg_scan_rows(jnp.where(low == w - 1, 1.0, g_next), rowid, w, True)
            cb = jnp.where((rowid & w) == 0, cb, 0.0)
            qs_scr[u, 4 + gamma] = cat([x * ca for x in qg]).astype(BF16)
            ks_scr[u, 4 + gamma] = cat([x * cb for x in kh])

        cp = _seg_scan_rows(jnp.where(rowid == 0, 1.0, g_prev), rowid, CHUNKS, False)
        cs = _seg_scan_rows(jnp.where(rowid == CHUNKS - 1, 1.0, g_next), rowid, CHUNKS, True)
        q2_scr[u] = cat([x * cp for x in qg]).astype(BF16)
        ks_scr[u, N_LEVELS] = cat([x * cs for x in kh])
        tot_scr[u] = jnp.broadcast_to((cp * g)[CHUNKS - 1:CHUNKS, :], (CHUNKS, HEAD_DIM))

    for u in range(n_units):
        for lvl in range(N_LEVELS):
            targets = LEVEL_TARGETS[lvl]
            kt = jnp.transpose(ks_scr[u, lvl]).astype(BF16)
            s = jnp.dot(qs_scr[u, lvl, 0:len(targets) * CHUNKS, :], kt,
                        preferred_element_type=F32)
            for i, j in enumerate(targets):
                rs = slice(j * CHUNKS, (j + 1) * CHUNKS)
                sj = s[i * CHUNKS:(i + 1) * CHUNKS, :]
                if LEVEL_MASK[lvl] is not None:
                    sj = sj * mask_rows(LEVEL_MASK[lvl], j)
                p_scr[u, rs, :] = p_scr[u, rs, :] + sj

    for u in range(n_units):
        h = u % N_HEADS
        _, _, _, vb, orows = blocks[u // N_HEADS]
        ls = slice(h * HEAD_DIM, (h + 1) * HEAD_DIM)
        st = st_ref[h]
        lhs = jnp.concatenate([p_scr[u].astype(BF16), q2_scr[u]], axis=1)
        rhs = jnp.concatenate([vb[:, ls], st.astype(BF16)], axis=0)
        o_ref[orows, ls] = jnp.dot(lhs, rhs, preferred_element_type=F32)
        k2t = jnp.transpose(ks_scr[u, N_LEVELS]).astype(BF16)
        upd = jnp.dot(k2t, vb[:, ls], preferred_element_type=F32)
        decay_col = jnp.transpose(jnp.broadcast_to(tot_scr[u, 0:1, :], (HEAD_DIM, HEAD_DIM)))
        st_ref[h] = st * decay_col + upd


def _mix_in_kernel(x_ref, mod_ref, nw_ref, win_ref, ws_ref, bs_ref, lb_ref,
                   pk_ref, o_ref,
                   gu_scr, st_ref, mask_ref, qs_scr, ks_scr, p_scr, q2_scr, tot_scr):
    yap_ref, sog_ref, sga_ref, sgb_ref = [
        pk_ref.at[:, i * D_MODEL:(i + 1) * D_MODEL] for i in range(PACKED_PARTS)]
    tm = pk_ref.shape[0]
    nblk = tm // BLOCK // M1_CHAINS

    @pl.when(pl.program_id(1) == 0)
    def _():
        st_ref[...] = jnp.zeros_like(st_ref)

    _write_level_masks(mask_ref)
    shift = mod_ref[3:4, :]
    scale = mod_ref[4:5, :]
    row = _block_token(lax.broadcasted_iota(jnp.int32, (BLOCK, BLOCK), 0))
    col = _block_token(lax.broadcasted_iota(jnp.int32, (BLOCK, BLOCK), 1))
    causal = row >= col
    lb = lb_ref[...]

    def project_stage(chain, res):
        b0 = chain * nblk
        rows = slice(b0 * BLOCK, (b0 + nblk) * BLOCK)
        x = _gather_block_rows(x_ref, b0, nblk)
        hb = _rms_modulate(x, nw_ref[...], scale, shift).astype(BF16)

        def project(p0):
            z = jnp.dot(hb, win_ref[:, p0 * D_MODEL:(p0 + M1_PIECES) * D_MODEL],
                        preferred_element_type=F32)
            return [z[:, i * D_MODEL:(i + 1) * D_MODEL] for i in range(M1_PIECES)]

        zu, zv = project(0)
        gu_scr[rows, :] = _gelu_tanh(zu)
        res["gv"] = _gelu_tanh(zv)
        yield
        zq, zf = project(2)
        res["q"] = zq * _sigmoid_t(zq)
        th = 0.5 * jnp.tanh(0.5 * zf)
        res["f"] = lb + (1.0 - lb) * (0.5 + th)
        res["k"] = (1.0 - lb) * (0.5 - th)
        yield
        zi, zog = project(4)
        res["vb"] = zi.astype(BF16)
        sog_ref[rows, :] = _sigmoid_t(zog).astype(BF16)
        yield
        zga, zgb = project(6)
        sga_ref[rows, :] = _sigmoid_t(zga).astype(BF16)
        sgb_ref[rows, :] = _sigmoid_t(zgb).astype(BF16)
        yield

    def gmlp_stage(chain, res):
        b0 = chain * nblk
        gv = res["gv"]
        for g in range(GMLP_GROUPS):
            ls = slice(g * HEAD_DIM, (g + 1) * HEAD_DIM)
            xg = gv[:, ls]
            mu = jnp.mean(xg, axis=-1, keepdims=True)
            xc = xg - mu
            var = jnp.mean(xc * xc, axis=-1, keepdims=True)
            vn = (xc * lax.rsqrt(var + EPS)).astype(BF16)
            rhs = jnp.concatenate(
                [vn[n * BLOCK:(n + 1) * BLOCK, :] for n in range(nblk)], axis=1)
            wc = jnp.where(causal, ws_ref[g], 0.0).astype(BF16)
            mixed = jnp.dot(wc, rhs, preferred_element_type=F32)
            bias = jnp.broadcast_to(bs_ref[:, g:g + 1], (BLOCK, BLOCK))
            for n in range(nblk):
                rs = slice((b0 + n) * BLOCK, (b0 + n + 1) * BLOCK)
                mx = mixed[:, n * BLOCK:(n + 1) * BLOCK] + bias
                yap_ref[rs, ls] = (gu_scr[rs, ls] * mx).astype(BF16)

    def hgrn_stage(chain, res):
        b0 = chain * nblk
        for n in range(0, nblk, HGRN_STAGED_BLOCKS):
            blocks = []
            for m in range(n, n + HGRN_STAGED_BLOCKS):
                br = slice(m * BLOCK, (m + 1) * BLOCK)
                orows = slice((b0 + m) * BLOCK, (b0 + m + 1) * BLOCK)
                blocks.append((res["f"][br, :], res["q"][br, :], res["k"][br, :],
                               res["vb"][br, :], orows))
            _hgrn_staged(blocks, st_ref, mask_ref, qs_scr, ks_scr, p_scr, q2_scr, tot_scr, o_ref)

    for chain in range(M1_CHAINS):
        res = {}
        for _ in project_stage(chain, res):
            pass
        gmlp_stage(chain, res)
        hgrn_stage(chain, res)


def _mix_in(h16, mod, nw, w_in, ws_p, bs_p, lb, *, layer, batch, seq):
    n = h16.shape[0] * POSITIONS
    tm = M1_TM
    tiles_per_seq = seq // tm
    tile = pl.BlockSpec((tm, D_MODEL), lambda b, t: (b * tiles_per_seq + t, 0))
    const2 = lambda b, t: (0, 0)
    f32_out = jax.ShapeDtypeStruct((n, D_MODEL), F32)
    bf_out = jax.ShapeDtypeStruct((n, PACKED_PARTS * D_MODEL), BF16)
    units = HGRN_STAGED_BLOCKS * N_HEADS
    return pl.pallas_call(
        _mix_in_kernel,
        grid=(batch, tiles_per_seq),
        in_specs=[
            pl.BlockSpec((tm // POSITIONS, CHUNK_WIDTH), lambda b, t: (b * tiles_per_seq + t, 0)),
            pl.BlockSpec((None, N_ADA, D_MODEL), lambda b, t: (b, 0, 0)),
            pl.BlockSpec((1, D_MODEL), const2),
            pl.BlockSpec((None, D_MODEL, N_PIECES * D_MODEL), lambda b, t: (layer, 0, 0),
                         pipeline_mode=pl.Buffered(1)),
            pl.BlockSpec((GMLP_GROUPS, BLOCK, BLOCK), lambda b, t: (0, 0, 0)),
            pl.BlockSpec((BLOCK, GMLP_GROUPS), const2),
            pl.BlockSpec((1, D_MODEL), const2),
        ],
        out_specs=[pl.BlockSpec((tm, PACKED_PARTS * D_MODEL),
                                lambda b, t: (b * tiles_per_seq + t, 0)), tile],
        out_shape=[bf_out, f32_out],
        scratch_shapes=[
            pltpu.VMEM((tm, D_MODEL), F32),
            pltpu.VMEM((N_HEADS, HEAD_DIM, HEAD_DIM), F32),
            pltpu.VMEM((N_MASKS, BLOCK, BLOCK), F32),
            pltpu.VMEM((units, N_LEVELS, BLOCK, HEAD_DIM), BF16),
            pltpu.VMEM((units, N_LEVELS + 1, BLOCK, HEAD_DIM), F32),
            pltpu.VMEM((units, BLOCK, BLOCK), F32),
            pltpu.VMEM((units, BLOCK, HEAD_DIM), BF16),
            pltpu.VMEM((units, CHUNKS, HEAD_DIM), F32),
        ],
        compiler_params=pltpu.CompilerParams(
            dimension_semantics=("parallel", "arbitrary"), vmem_limit_bytes=VMEM_LIMIT),
        name="mixer_in_proj_hgrn2",
    )(h16, mod, nw, w_in, ws_p, bs_p, lb)


M3_TM = 1024
M3_CHAINS = 2


def _mix_out_kernel(x_ref, mod_ref, pk_ref, o_ref, wa_ref, wb_ref, wo_ref, out_ref):
    yap_ref, sog_ref, sga_ref, sgb_ref = [
        pk_ref.at[:, i * D_MODEL:(i + 1) * D_MODEL] for i in range(PACKED_PARTS)]
    nblk = pk_ref.shape[0] // BLOCK // M3_CHAINS
    gate = mod_ref[5:6, :]
    for chain in range(M3_CHAINS):
        b0 = chain * nblk
        rows = slice(b0 * BLOCK, (b0 + nblk) * BLOCK)
        o = o_ref[rows, :]
        parts = []
        for h in range(N_HEADS):
            oh = o[:, h * HEAD_DIM:(h + 1) * HEAD_DIM]
            ms = jnp.mean(oh * oh, axis=-1, keepdims=True)
            parts.append(oh * lax.rsqrt(ms + EPS))
        on = (jnp.concatenate(parts, axis=1) * sog_ref[rows, :].astype(F32)).astype(BF16)
        ya = jnp.dot(yap_ref[rows, :], wa_ref[...], preferred_element_type=F32)
        yb = jnp.dot(on, wb_ref[...], preferred_element_type=F32)
        m = (sga_ref[rows, :].astype(F32) * ya + sgb_ref[rows, :].astype(F32) * yb).astype(BF16)
        out = jnp.dot(m, wo_ref[...], preferred_element_type=F32)
        y = _gather_block_rows(x_ref, b0, nblk) + (1.0 + gate) * out
        _scatter_block_rows(out_ref, y, b0, nblk)


def _mix_out(h16, mod, pk, o, w_a, w_b, w_o, *, layer, seq):
    n = h16.shape[0] * POSITIONS
    tm = M3_TM
    tiles_per_seq = seq // tm
    tile = pl.BlockSpec((tm, D_MODEL), lambda i: (i, 0))
    tile16 = pl.BlockSpec((tm // POSITIONS, CHUNK_WIDTH), lambda i: (i, 0))
    wspec = pl.BlockSpec((None, D_MODEL, D_MODEL), lambda i: (layer, 0, 0),
                         pipeline_mode=pl.Buffered(1))
    return pl.pallas_call(
        _mix_out_kernel,
        grid=(n // tm,),
        in_specs=[
            tile16,
            pl.BlockSpec((None, N_ADA, D_MODEL), lambda i: (i // tiles_per_seq, 0, 0)),
            pl.BlockSpec((tm, PACKED_PARTS * D_MODEL), lambda i: (i, 0)),
            tile,
            wspec, wspec, wspec,
        ],
        out_specs=tile16,
        out_shape=jax.ShapeDtypeStruct(h16.shape, F32),
        compiler_params=pltpu.CompilerParams(
            dimension_semantics=("parallel",), vmem_limit_bytes=VMEM_LIMIT),
        name="mixer_out_proj",
    )(h16, mod, pk, o, w_a, w_b, w_o)


def kernel(x, c, w_in, gmlp_ws, gmlp_b, hgrn_lb_logits, w_branch_a, w_branch_b, w_out,
           ffn_w13, ffn_w2, norm_w, ada_w, ada_b, final_norm_w):
    batch, seq, d = x.shape
    assert d == D_MODEL and seq % M1_TM == 0 and seq % FFN_TM == 0 and seq % M3_TM == 0
    n = batch * seq

    mod_all = _ada_mod(c, ada_w, ada_b).reshape(DEPTH, batch, N_ADA, D_MODEL)
    lb_all = _lower_bounds(hgrn_lb_logits)
    perm = jnp.array([_block_token(p) for p in range(BLOCK)], dtype=jnp.int32)
    ws_p = gmlp_ws[:, :, perm, :][:, :, :, perm]
    bs_p = jnp.swapaxes(gmlp_b[:, :, perm], 1, 2)

    w13_b, w2_b, w_in_b = ffn_w13.astype(BF16), ffn_w2.astype(BF16), w_in.astype(BF16)
    w_a_b, w_b_b, w_o_b = (w_branch_a.astype(BF16), w_branch_b.astype(BF16),
                           w_out.astype(BF16))

    h16 = x.reshape(n, D_MODEL)
    for l in range(DEPTH):
        mod = mod_all[l]
        nw = norm_w[l].reshape(3, 1, D_MODEL)
        h16 = _ffn(h16, mod, nw[0], w13_b, w2_b, layer=l, half=0, seq=seq,
                   in_natural=(l == 0))
        pk, o = _mix_in(h16, mod, nw[1], w_in_b, ws_p[l], bs_p[l],
                        lb_all[l].reshape(1, D_MODEL), layer=l, batch=batch, seq=seq)
        h16 = _mix_out(h16, mod, pk, o, w_a_b, w_b_b, w_o_b, layer=l, seq=seq)
        final_w = final_norm_w.reshape(1, D_MODEL) if l == DEPTH - 1 else None
        h16 = _ffn(h16, mod, nw[2], w13_b, w2_b, layer=l, half=1, seq=seq, final_w=final_w)
    return h16.reshape(batch, seq, D_MODEL)
```

```python
import functools

import jax
import jax.numpy as jnp
from jax import lax
from jax.experimental import pallas as pl
from jax.experimental.pallas import tpu as pltpu

F32 = jnp.float32
BF16 = jnp.bfloat16

D_MODEL = 1024
DEPTH = 4
FFN_HIDDEN = 2816
N_ADA = 9
EPS = 1e-6
GMLP_GROUPS = 8
HEAD_DIM = 128
N_HEADS = D_MODEL // HEAD_DIM
N_PIECES = 8

BLOCK = 128
POSITIONS = 16
CHUNKS = BLOCK // POSITIONS
CHUNK_WIDTH = POSITIONS * D_MODEL

VMEM_LIMIT = 56 * 1024 * 1024


def _sigmoid(x):
    return 1.0 / (1.0 + jnp.exp(-x))


def _sigmoid_t(x):
    return 0.5 * jnp.tanh(0.5 * x) + 0.5


def _gelu_tanh(x):
    c = 0.7978845608028654
    hx = 0.5 * x
    return hx + hx * jnp.tanh(x * (c + (c * 0.044715) * (x * x)))


def _rms_modulate(x, nw, scale, shift):
    ms = jnp.mean(x * x, axis=-1, keepdims=True)
    gain = nw * (1.0 + scale)
    return (x * lax.rsqrt(ms + EPS)) * gain + shift


def _block_token(p):
    return (p % CHUNKS) * POSITIONS + p // CHUNKS


def _gather_block_rows(x_ref, b0, n_blocks):
    pieces = [x_ref[n * CHUNKS:(n + 1) * CHUNKS, j * D_MODEL:(j + 1) * D_MODEL]
              for n in range(b0, b0 + n_blocks) for j in range(POSITIONS)]
    return jnp.concatenate(pieces, axis=0)


def _scatter_block_rows(out_ref, y, b0, n_blocks):
    for n in range(n_blocks):
        for j in range(POSITIONS):
            r = (n * POSITIONS + j) * CHUNKS
            out_ref[(b0 + n) * CHUNKS:(b0 + n + 1) * CHUNKS,
                    j * D_MODEL:(j + 1) * D_MODEL] = y[r:r + CHUNKS, :]


def _ada_kernel(c_ref, w_ref, b_ref, o_ref):
    c = c_ref[...]
    ca = c * _sigmoid(c)
    o_ref[...] = jnp.dot(ca, w_ref[...], preferred_element_type=F32,
                         precision=lax.Precision.HIGHEST) + b_ref[...]


ADA_GROUP = 3


def _ada_mod(c, ada_w, ada_b):
    b = c.shape[0]
    return pl.pallas_call(
        _ada_kernel,
        grid=(DEPTH, N_ADA // ADA_GROUP),
        in_specs=[
            pl.BlockSpec((b, D_MODEL), lambda l, j: (0, 0)),
            pl.BlockSpec((None, D_MODEL, ADA_GROUP * D_MODEL), lambda l, j: (l, 0, j)),
            pl.BlockSpec((None, 1, ADA_GROUP * D_MODEL), lambda l, j: (l, 0, j)),
        ],
        out_specs=pl.BlockSpec((None, b, ADA_GROUP * D_MODEL), lambda l, j: (l, 0, j)),
        out_shape=jax.ShapeDtypeStruct((DEPTH, b, N_ADA * D_MODEL), F32),
        compiler_params=pltpu.CompilerParams(
            dimension_semantics=("parallel", "parallel"), vmem_limit_bytes=VMEM_LIMIT),
        name="ada_mod",
    )(c, ada_w, ada_b.reshape(DEPTH, 1, N_ADA * D_MODEL))


def _lb_kernel(x_ref, o_ref):
    x = x_ref[...]
    rows = [x[i:i + 1, :] for i in range(DEPTH)]
    m = rows[0]
    for r in rows[1:]:
        m = jnp.maximum(m, r)
    e = [jnp.exp(r - m) for r in rows]
    tot = e[0]
    for t in e[1:]:
        tot = tot + t
    p = [t / tot for t in e]
    cum = p[0]
    first = cum
    for i in range(DEPTH):
        if i > 0:
            cum = cum + p[i]
        o_ref[i:i + 1, :] = cum - first


def _lower_bounds(logits):
    return pl.pallas_call(
        _lb_kernel,
        out_shape=jax.ShapeDtypeStruct(logits.shape, F32),
        name="hgrn_lower_bounds",
    )(logits.astype(F32))


FFN_TM = 1024
FFN_CHAINS = 4
FFN_COL_CHUNKS = ((0, 1024), (1024, 2048), (2048, FFN_HIDDEN))


def _ffn_kernel(x_ref, mod_ref, nw_ref, w13_ref, w2_ref, *rest, mod_off, final, in_natural):
    if final:
        fnw_ref, o_ref = rest
    else:
        (o_ref,) = rest
    shift = mod_ref[mod_off:mod_off + 1, :]
    scale = mod_ref[mod_off + 1:mod_off + 2, :]
    gate = mod_ref[mod_off + 2:mod_off + 3, :]
    rows = FFN_TM // POSITIONS // FFN_CHAINS
    for r in range(FFN_CHAINS):
        rs = slice(r * rows, (r + 1) * rows)
        if in_natural:
            nat = x_ref[r * rows * POSITIONS:(r + 1) * rows * POSITIONS, :]
            x = pltpu.einshape("(cj)d->(jc)d", nat, j=POSITIONS)
        else:
            x = jnp.concatenate(
                [x_ref[rs, j * D_MODEL:(j + 1) * D_MODEL] for j in range(POSITIONS)], axis=0)
        hb = _rms_modulate(x, nw_ref[...], scale, shift).astype(BF16)
        acc = None
        for c0, c1 in FFN_COL_CHUNKS:
            a = jnp.dot(hb, w13_ref[:, c0:c1], preferred_element_type=F32)
            b = jnp.dot(hb, w13_ref[:, FFN_HIDDEN + c0:FFN_HIDDEN + c1],
                        preferred_element_type=F32)
            act = (a * _sigmoid_t(a) * b).astype(BF16)
            part = jnp.dot(act, w2_ref[c0:c1, :], preferred_element_type=F32)
            acc = part if acc is None else acc + part
        y = x + (0.5 * (1.0 + gate)) * acc
        if final:
            ms = jnp.mean(y * y, axis=-1, keepdims=True)
            y = y * lax.rsqrt(ms + EPS) * fnw_ref[...]
            o_ref[r * rows * POSITIONS:(r + 1) * rows * POSITIONS, :] = pltpu.einshape(
                "(jc)d->(cj)d", y, j=POSITIONS)
        else:
            for j in range(POSITIONS):
                o_ref[rs, j * D_MODEL:(j + 1) * D_MODEL] = y[j * rows:(j + 1) * rows, :]


def _ffn(h_in, mod, nw, w13, w2, *, layer, half, seq, final_w=None, in_natural=False):
    mod_off = 6 * half
    n = h_in.shape[0] if in_natural else h_in.shape[0] * POSITIONS
    tm = FFN_TM
    tiles_per_seq = seq // tm
    final = final_w is not None
    tile16 = pl.BlockSpec((tm // POSITIONS, CHUNK_WIDTH), lambda i: (i, 0))
    in_specs = [
        pl.BlockSpec((tm, D_MODEL), lambda i: (i, 0)) if in_natural else tile16,
        pl.BlockSpec((None, N_ADA, D_MODEL), lambda i: (i // tiles_per_seq, 0, 0)),
        pl.BlockSpec((1, D_MODEL), lambda i: (0, 0)),
        pl.BlockSpec((None, None, D_MODEL, 2 * FFN_HIDDEN), lambda i: (layer, half, 0, 0),
                     pipeline_mode=pl.Buffered(1)),
        pl.BlockSpec((None, None, FFN_HIDDEN, D_MODEL), lambda i: (layer, half, 0, 0),
                     pipeline_mode=pl.Buffered(1)),
    ]
    args = [h_in, mod, nw, w13, w2]
    if final:
        in_specs.append(pl.BlockSpec((1, D_MODEL), lambda i: (0, 0)))
        args.append(final_w)
    return pl.pallas_call(
        functools.partial(_ffn_kernel, mod_off=mod_off, final=final, in_natural=in_natural),
        grid=(n // tm,),
        in_specs=in_specs,
        out_specs=pl.BlockSpec((tm, D_MODEL), lambda i: (i, 0)) if final else tile16,
        out_shape=jax.ShapeDtypeStruct(
            (n, D_MODEL) if final else (n // POSITIONS, CHUNK_WIDTH), F32),
        compiler_params=pltpu.CompilerParams(
            dimension_semantics=("parallel",), vmem_limit_bytes=VMEM_LIMIT),
        name="ffn_half_step",
    )(*args)


M1_TM = 512
M1_PIECES = 2
M1_CHAINS = 2
PACKED_PARTS = 4
N_MASKS = 7


def _seg_scan_rows(x, rowid, width, reverse):
    d = 1
    y = x
    while d < width:
        if reverse:
            shifted = pltpu.roll(y, CHUNKS - d, axis=0)
            ok = (rowid & (width - 1)) < width - d
        else:
            shifted = pltpu.roll(y, d, axis=0)
            ok = (rowid & (width - 1)) >= d
        y = y * jnp.where(ok, shifted, 1.0)
        d *= 2
    return y


def _write_level_masks(mask_ref):
    pr = lax.broadcasted_iota(jnp.int32, (BLOCK, BLOCK), 0)
    pc = lax.broadcasted_iota(jnp.int32, (BLOCK, BLOCK), 1)
    same_chunk = ((pr ^ pc) & (CHUNKS - 1)) == 0
    mask_ref[0] = jnp.where(pr == pc, 1.0, 0.0)
    for beta in range(4):
        sh = 3 + beta + 1
        mask_ref[1 + beta] = jnp.where(same_chunk & ((pr >> sh) == (pc >> sh)), 1.0, 0.0)
    for gamma in range(2):
        sh = gamma + 1
        mask_ref[5 + gamma] = jnp.where(
            ((pr & (CHUNKS - 1)) >> sh) == ((pc & (CHUNKS - 1)) >> sh), 1.0, 0.0)


N_LEVELS = 7
HGRN_STAGED_BLOCKS = 1
LEVEL_TARGETS = (
    [[j for j in range(POSITIONS) if j & (1 << beta)] for beta in range(4)]
    + [list(range(POSITIONS))] * 3)
LEVEL_MASK = (1, 2, 3, 4, 5, 6, None)


def _hgrn_staged(blocks, st_ref, mask_ref, qs_scr, ks_scr, p_scr, q2_scr, tot_scr, o_ref):
    n_units = len(blocks) * N_HEADS
    rowid = lax.broadcasted_iota(jnp.int32, (CHUNKS, HEAD_DIM), 0)
    zero = jnp.zeros((CHUNKS, HEAD_DIM), F32)

    def slabs(val):
        return [val[j * CHUNKS:(j + 1) * CHUNKS, :] for j in range(POSITIONS)]

    def cat(sl):
        return jnp.concatenate(sl, axis=0)

    def mask_rows(level, j):
        return mask_ref[level, j * CHUNKS:(j + 1) * CHUNKS, :]

    for u in range(n_units):
        h = u % N_HEADS
        f, q, k, _, _ = blocks[u // N_HEADS]
        ls = slice(h * HEAD_DIM, (h + 1) * HEAD_DIM)
        fs, qs, ks = slabs(f[:, ls]), slabs(q[:, ls]), slabs(k[:, ls])
        for j in range(POSITIONS):
            p_scr[u, j * CHUNKS:(j + 1) * CHUNKS, :] = (
                jnp.sum(qs[j] * ks[j], axis=-1, keepdims=True) * mask_rows(0, j))

        for beta in range(4):
            w = 1 << beta
            a = [None] * POSITIONS
            for j in range(POSITIONS):
                if j & w:
                    a[j] = fs[j] if (j & (w - 1)) == 0 else a[j - 1] * fs[j]
            b = [None] * POSITIONS
            for j in range(POSITIONS - 1, -1, -1):
                if not (j & w) and (j & (w - 1)) != w - 1:
                    b[j] = fs[j + 1] if b[j + 1] is None else b[j + 1] * fs[j + 1]
            upper = LEVEL_TARGETS[beta]
            qs_scr[u, beta, 0:len(upper) * CHUNKS, :] = cat(
                [qs[j] * a[j] for j in upper]).astype(BF16)
            ks_scr[u, beta] = cat(
                [zero if (j & w) else (ks[j] if b[j] is None else ks[j] * b[j])
                 for j in range(POSITIONS)])

        gcum = [fs[0]]
        for j in range(1, POSITIONS):
            gcum.append(gcum[-1] * fs[j])
        hsuf = [None] * POSITIONS
        for j in range(POSITIONS - 2, -1, -1):
            hsuf[j] = fs[j + 1] if hsuf[j + 1] is None else hsuf[j + 1] * fs[j + 1]
        g = gcum[-1]
        qg = [qs[j] * gcum[j] for j in range(POSITIONS)]
        kh = [ks[j] if hsuf[j] is None else ks[j] * hsuf[j] for j in range(POSITIONS)]
        g_prev = pltpu.roll(g, 1, axis=0)
        g_next = pltpu.roll(g, CHUNKS - 1, axis=0)

        for gamma in range(3):
            w = 1 << gamma
            low = rowid & (w - 1)
            ca = _seg_scan_rows(jnp.where(low == 0, 1.0, g_prev), rowid, w, False)
            ca = jnp.where((rowid & w) != 0, ca, 0.0)
            cb = _seg_scan_rows(jnp.where(low == w - 1, 1.0, g_next), rowid, w, True)
            cb = jnp.where((rowid & w) == 0, cb, 0.0)
            qs_scr[u, 4 + gamma] = cat([x * ca for x in qg]).astype(BF16)
            ks_scr[u, 4 + gamma] = cat([x * cb for x in kh])

        cp = _seg_scan_rows(jnp.where(rowid == 0, 1.0, g_prev), rowid, CHUNKS, False)
        cs = _seg_scan_rows(jnp.where(rowid == CHUNKS - 1, 1.0, g_next), rowid, CHUNKS, True)
        q2_scr[u] = cat([x * cp for x in qg]).astype(BF16)
        ks_scr[u, N_LEVELS] = cat([x * cs for x in kh])
        tot_scr[u] = jnp.broadcast_to((cp * g)[CHUNKS - 1:CHUNKS, :], (CHUNKS, HEAD_DIM))

    for u in range(n_units):
        for lvl in range(N_LEVELS):
            targets = LEVEL_TARGETS[lvl]
            kt = jnp.transpose(ks_scr[u, lvl]).astype(BF16)
            s = jnp.dot(qs_scr[u, lvl, 0:len(targets) * CHUNKS, :], kt,
                        preferred_element_type=F32)
            for i, j in enumerate(targets):
                rs = slice(j * CHUNKS, (j + 1) * CHUNKS)
                sj = s[i * CHUNKS:(i + 1) * CHUNKS, :]
                if LEVEL_MASK[lvl] is not None:
                    sj = sj * mask_rows(LEVEL_MASK[lvl], j)
                p_scr[u, rs, :] = p_scr[u, rs, :] + sj

    for u in range(n_units):
        h = u % N_HEADS
        _, _, _, vb, orows = blocks[u // N_HEADS]
        ls = slice(h * HEAD_DIM, (h + 1) * HEAD_DIM)
        st = st_ref[h]
        lhs = jnp.concatenate([p_scr[u].astype(BF16), q2_scr[u]], axis=1)
        rhs = jnp.concatenate([vb[:, ls], st.astype(BF16)], axis=0)
        o_ref[orows, ls] = jnp.dot(lhs, rhs, preferred_element_type=F32)
        k2t = jnp.transpose(ks_scr[u, N_LEVELS]).astype(BF16)
        upd = jnp.dot(k2t, vb[:, ls], preferred_element_type=F32)
        decay_col = jnp.transpose(jnp.broadcast_to(tot_scr[u, 0:1, :], (HEAD_DIM, HEAD_DIM)))
        st_ref[h] = st * decay_col + upd


def _mix_in_kernel(x_ref, mod_ref, nw_ref, win_ref, ws_ref, bs_ref, lb_ref,
                   pk_ref, o_ref,
                   gu_scr, st_ref, mask_ref, qs_scr, ks_scr, p_scr, q2_scr, tot_scr):
    yap_ref, sog_ref, sga_ref, sgb_ref = [
        pk_ref.at[:, i * D_MODEL:(i + 1) * D_MODEL] for i in range(PACKED_PARTS)]
    tm = pk_ref.shape[0]
    nblk = tm // BLOCK // M1_CHAINS

    @pl.when(pl.program_id(1) == 0)
    def _():
        st_ref[...] = jnp.zeros_like(st_ref)

    _write_level_masks(mask_ref)
    shift = mod_ref[3:4, :]
    scale = mod_ref[4:5, :]
    row = _block_token(lax.broadcasted_iota(jnp.int32, (BLOCK, BLOCK), 0))
    col = _block_token(lax.broadcasted_iota(jnp.int32, (BLOCK, BLOCK), 1))
    causal = row >= col
    lb = lb_ref[...]
    half_span = 0.5 * (1.0 - lb)
    f_mid = lb + half_span

    def project_stage(chain, res):
        b0 = chain * nblk
        rows = slice(b0 * BLOCK, (b0 + nblk) * BLOCK)
        x = _gather_block_rows(x_ref, b0, nblk)
        hb = _rms_modulate(x, nw_ref[...], scale, shift).astype(BF16)

        def project(p0):
            z = jnp.dot(hb, win_ref[:, p0 * D_MODEL:(p0 + M1_PIECES) * D_MODEL],
                        preferred_element_type=F32)
            return [z[:, i * D_MODEL:(i + 1) * D_MODEL] for i in range(M1_PIECES)]

        zu, zv = project(0)
        gu_scr[rows, :] = _gelu_tanh(zu)
        res["gv"] = _gelu_tanh(zv)
        yield
        zq, zf = project(2)
        hq = 0.5 * zq
        res["q"] = hq + hq * jnp.tanh(hq)
        at = half_span * jnp.tanh(0.5 * zf)
        res["f"] = f_mid + at
        res["k"] = half_span - at
        yield
        zi, zog = project(4)
        res["vb"] = zi.astype(BF16)
        sog_ref[rows, :] = _sigmoid_t(zog).astype(BF16)
        yield
        zga, zgb = project(6)
        sga_ref[rows, :] = _sigmoid_t(zga).astype(BF16)
        sgb_ref[rows, :] = _sigmoid_t(zgb).astype(BF16)
        yield

    def gmlp_stage(chain, res):
        b0 = chain * nblk
        gv = res["gv"]
        for g in range(GMLP_GROUPS):
            ls = slice(g * HEAD_DIM, (g + 1) * HEAD_DIM)
            xg = gv[:, ls]
            mu = jnp.mean(xg, axis=-1, keepdims=True)
            xc = xg - mu
            var = jnp.mean(xc * xc, axis=-1, keepdims=True)
            vn = (xc * lax.rsqrt(var + EPS)).astype(BF16)
            rhs = jnp.concatenate(
                [vn[n * BLOCK:(n + 1) * BLOCK, :] for n in range(nblk)], axis=1)
            wc = jnp.where(causal, ws_ref[g], 0.0).astype(BF16)
            mixed = jnp.dot(wc, rhs, preferred_element_type=F32)
            bias = jnp.broadcast_to(bs_ref[:, g:g + 1], (BLOCK, BLOCK))
            for n in range(nblk):
                rs = slice((b0 + n) * BLOCK, (b0 + n + 1) * BLOCK)
                mx = mixed[:, n * BLOCK:(n + 1) * BLOCK] + bias
                yap_ref[rs, ls] = (gu_scr[rs, ls] * mx).astype(BF16)

    def hgrn_stage(chain, res):
        b0 = chain * nblk
        for n in range(0, nblk, HGRN_STAGED_BLOCKS):
            blocks = []
            for m in range(n, n + HGRN_STAGED_BLOCKS):
                br = slice(m * BLOCK, (m + 1) * BLOCK)
                orows = slice((b0 + m) * BLOCK, (b0 + m + 1) * BLOCK)
                blocks.append((res["f"][br, :], res["q"][br, :], res["k"][br, :],
                               res["vb"][br, :], orows))
            _hgrn_staged(blocks, st_ref, mask_ref, qs_scr, ks_scr, p_scr, q2_scr, tot_scr, o_ref)

    for chain in range(M1_CHAINS):
        res = {}
        for _ in project_stage(chain, res):
            pass
        gmlp_stage(chain, res)
        hgrn_stage(chain, res)


def _mix_in(h16, mod, nw, w_in, ws_p, bs_p, lb, *, layer, batch, seq):
    n = h16.shape[0] * POSITIONS
    tm = M1_TM
    tiles_per_seq = seq // tm
    tile = pl.BlockSpec((tm, D_MODEL), lambda b, t: (b * tiles_per_seq + t, 0))
    const2 = lambda b, t: (0, 0)
    f32_out = jax.ShapeDtypeStruct((n, D_MODEL), F32)
    bf_out = jax.ShapeDtypeStruct((n, PACKED_PARTS * D_MODEL), BF16)
    units = HGRN_STAGED_BLOCKS * N_HEADS
    return pl.pallas_call(
        _mix_in_kernel,
        grid=(batch, tiles_per_seq),
        in_specs=[
            pl.BlockSpec((tm // POSITIONS, CHUNK_WIDTH), lambda b, t: (b * tiles_per_seq + t, 0)),
            pl.BlockSpec((None, N_ADA, D_MODEL), lambda b, t: (b, 0, 0)),
            pl.BlockSpec((1, D_MODEL), const2),
            pl.BlockSpec((None, D_MODEL, N_PIECES * D_MODEL), lambda b, t: (layer, 0, 0),
                         pipeline_mode=pl.Buffered(1)),
            pl.BlockSpec((GMLP_GROUPS, BLOCK, BLOCK), lambda b, t: (0, 0, 0)),
            pl.BlockSpec((BLOCK, GMLP_GROUPS), const2),
            pl.BlockSpec((1, D_MODEL), const2),
        ],
        out_specs=[pl.BlockSpec((tm, PACKED_PARTS * D_MODEL),
                                lambda b, t: (b * tiles_per_seq + t, 0)), tile],
        out_shape=[bf_out, f32_out],
        scratch_shapes=[
            pltpu.VMEM((tm, D_MODEL), F32),
            pltpu.VMEM((N_HEADS, HEAD_DIM, HEAD_DIM), F32),
            pltpu.VMEM((N_MASKS, BLOCK, BLOCK), F32),
            pltpu.VMEM((units, N_LEVELS, BLOCK, HEAD_DIM), BF16),
            pltpu.VMEM((units, N_LEVELS + 1, BLOCK, HEAD_DIM), F32),
            pltpu.VMEM((units, BLOCK, BLOCK), F32),
            pltpu.VMEM((units, BLOCK, HEAD_DIM), BF16),
            pltpu.VMEM((units, CHUNKS, HEAD_DIM), F32),
        ],
        compiler_params=pltpu.CompilerParams(
            dimension_semantics=("parallel", "arbitrary"), vmem_limit_bytes=VMEM_LIMIT),
        name="mixer_in_proj_hgrn2",
    )(h16, mod, nw, w_in, ws_p, bs_p, lb)


M3_TM = 1024
M3_CHAINS = 2


def _mix_out_kernel(x_ref, mod_ref, pk_ref, o_ref, wa_ref, wb_ref, wo_ref, out_ref):
    yap_ref, sog_ref, sga_ref, sgb_ref = [
        pk_ref.at[:, i * D_MODEL:(i + 1) * D_MODEL] for i in range(PACKED_PARTS)]
    nblk = pk_ref.shape[0] // BLOCK // M3_CHAINS
    gate = mod_ref[5:6, :]
    for chain in range(M3_CHAINS):
        b0 = chain * nblk
        rows = slice(b0 * BLOCK, (b0 + nblk) * BLOCK)
        o = o_ref[rows, :]
        parts = []
        for h in range(N_HEADS):
            oh = o[:, h * HEAD_DIM:(h + 1) * HEAD_DIM]
            ms = jnp.mean(oh * oh, axis=-1, keepdims=True)
            parts.append(oh * lax.rsqrt(ms + EPS))
        on = (jnp.concatenate(parts, axis=1) * sog_ref[rows, :].astype(F32)).astype(BF16)
        ya = jnp.dot(yap_ref[rows, :], wa_ref[...], preferred_element_type=F32)
        yb = jnp.dot(on, wb_ref[...], preferred_element_type=F32)
        m = (sga_ref[rows, :].astype(F32) * ya + sgb_ref[rows, :].astype(F32) * yb).astype(BF16)
        out = jnp.dot(m, wo_ref[...], preferred_element_type=F32)
        y = _gather_block_rows(x_ref, b0, nblk) + (1.0 + gate) * out
        _scatter_block_rows(out_ref, y, b0, nblk)


def _mix_out(h16, mod, pk, o, w_a, w_b, w_o, *, layer, seq):
    n = h16.shape[0] * POSITIONS
    tm = M3_TM
    tiles_per_seq = seq // tm
    tile = pl.BlockSpec((tm, D_MODEL), lambda i: (i, 0))
    tile16 = pl.BlockSpec((tm // POSITIONS, CHUNK_WIDTH), lambda i: (i, 0))
    wspec = pl.BlockSpec((None, D_MODEL, D_MODEL), lambda i: (layer, 0, 0),
                         pipeline_mode=pl.Buffered(1))
    return pl.pallas_call(
        _mix_out_kernel,
        grid=(n // tm,),
        in_specs=[
            tile16,
            pl.BlockSpec((None, N_ADA, D_MODEL), lambda i: (i // tiles_per_seq, 0, 0)),
            pl.BlockSpec((tm, PACKED_PARTS * D_MODEL), lambda i: (i, 0)),
            tile,
            wspec, wspec, wspec,
        ],
        out_specs=tile16,
        out_shape=jax.ShapeDtypeStruct(h16.shape, F32),
        compiler_params=pltpu.CompilerParams(
            dimension_semantics=("parallel",), vmem_limit_bytes=VMEM_LIMIT),
        name="mixer_out_proj",
    )(h16, mod, pk, o, w_a, w_b, w_o)


def kernel(x, c, w_in, gmlp_ws, gmlp_b, hgrn_lb_logits, w_branch_a, w_branch_b, w_out,
           ffn_w13, ffn_w2, norm_w, ada_w, ada_b, final_norm_w):
    batch, seq, d = x.shape
    assert d == D_MODEL and seq % M1_TM == 0 and seq % FFN_TM == 0 and seq % M3_TM == 0
    n = batch * seq

    mod_all = _ada_mod(c, ada_w, ada_b).reshape(DEPTH, batch, N_ADA, D_MODEL)
    lb_all = _lower_bounds(hgrn_lb_logits)
    perm = jnp.array([_block_token(p) for p in range(BLOCK)], dtype=jnp.int32)
    ws_p = gmlp_ws[:, :, perm, :][:, :, :, perm]
    bs_p = jnp.swapaxes(gmlp_b[:, :, perm], 1, 2)

    w13_b, w2_b, w_in_b = ffn_w13.astype(BF16), ffn_w2.astype(BF16), w_in.astype(BF16)
    w_a_b, w_b_b, w_o_b = (w_branch_a.astype(BF16), w_branch_b.astype(BF16),
                           w_out.astype(BF16))

    h16 = x.reshape(n, D_MODEL)
    for l in range(DEPTH):
        mod = mod_all[l]
        nw = norm_w[l].reshape(3, 1, D_MODEL)
        h16 = _ffn(h16, mod, nw[0], w13_b, w2_b, layer=l, half=0, seq=seq,
                   in_natural=(l == 0))
        pk, o = _mix_in(h16, mod, nw[1], w_in_b, ws_p[l], bs_p[l],
                        lb_all[l].reshape(1, D_MODEL), layer=l, batch=batch, seq=seq)
        h16 = _mix_out(h16, mod, pk, o, w_a_b, w_b_b, w_o_b, layer=l, seq=seq)
        final_w = final_norm_w.reshape(1, D_MODEL) if l == DEPTH - 1 else None
        h16 = _ffn(h16, mod, nw[2], w13_b, w2_b, layer=l, half=1, seq=seq, final_w=final_w)
    return h16.reshape(batch, seq, D_MODEL)
```

```python
import functools

import jax
import jax.numpy as jnp
from jax import lax
from jax.experimental import pallas as pl
from jax.experimental.pallas import tpu as pltpu

F32 = jnp.float32
BF16 = jnp.bfloat16

D_MODEL = 1024
DEPTH = 4
FFN_HIDDEN = 2816
N_ADA = 9
EPS = 1e-6
GMLP_GROUPS = 8
HEAD_DIM = 128
N_HEADS = D_MODEL // HEAD_DIM
N_PIECES = 8

BLOCK = 128
POSITIONS = 16
CHUNKS = BLOCK // POSITIONS
CHUNK_WIDTH = POSITIONS * D_MODEL

VMEM_LIMIT = 56 * 1024 * 1024


def _sigmoid(x):
    return 1.0 / (1.0 + jnp.exp(-x))


def _gelu_tanh(x):
    c = 0.7978845608028654
    hx = 0.5 * x
    return hx + hx * jnp.tanh(x * (c + (c * 0.044715) * (x * x)))


def _rms_modulate(x, nw, scale, shift):
    ms = jnp.mean(x * x, axis=-1, keepdims=True)
    gain = nw * (1.0 + scale)
    return (x * lax.rsqrt(ms + EPS)) * gain + shift


def _block_token(p):
    return (p % CHUNKS) * POSITIONS + p // CHUNKS


def _gather_block_rows(x_ref, b0, n_blocks):
    pieces = [x_ref[n * CHUNKS:(n + 1) * CHUNKS, j * D_MODEL:(j + 1) * D_MODEL]
              for n in range(b0, b0 + n_blocks) for j in range(POSITIONS)]
    return jnp.concatenate(pieces, axis=0)


def _scatter_block_rows(out_ref, y, b0, n_blocks):
    for n in range(n_blocks):
        for j in range(POSITIONS):
            r = (n * POSITIONS + j) * CHUNKS
            out_ref[(b0 + n) * CHUNKS:(b0 + n + 1) * CHUNKS,
                    j * D_MODEL:(j + 1) * D_MODEL] = y[r:r + CHUNKS, :]


def _ada_kernel(c_ref, w_ref, b_ref, o_ref):
    c = c_ref[...]
    ca = c * _sigmoid(c)
    o_ref[...] = jnp.dot(ca, w_ref[...], preferred_element_type=F32,
                         precision=lax.Precision.HIGHEST) + b_ref[...]


ADA_GROUP = 3


def _ada_mod(c, ada_w, ada_b):
    b = c.shape[0]
    return pl.pallas_call(
        _ada_kernel,
        grid=(DEPTH, N_ADA // ADA_GROUP),
        in_specs=[
            pl.BlockSpec((b, D_MODEL), lambda l, j: (0, 0)),
            pl.BlockSpec((None, D_MODEL, ADA_GROUP * D_MODEL), lambda l, j: (l, 0, j)),
            pl.BlockSpec((None, 1, ADA_GROUP * D_MODEL), lambda l, j: (l, 0, j)),
        ],
        out_specs=pl.BlockSpec((None, b, ADA_GROUP * D_MODEL), lambda l, j: (l, 0, j)),
        out_shape=jax.ShapeDtypeStruct((DEPTH, b, N_ADA * D_MODEL), F32),
        compiler_params=pltpu.CompilerParams(
            dimension_semantics=("parallel", "parallel"), vmem_limit_bytes=VMEM_LIMIT),
        name="ada_mod",
    )(c, ada_w, ada_b.reshape(DEPTH, 1, N_ADA * D_MODEL))


def _lb_kernel(x_ref, o_ref):
    x = x_ref[...]
    rows = [x[i:i + 1, :] for i in range(DEPTH)]
    m = rows[0]
    for r in rows[1:]:
        m = jnp.maximum(m, r)
    e = [jnp.exp(r - m) for r in rows]
    tot = e[0]
    for t in e[1:]:
        tot = tot + t
    p = [t / tot for t in e]
    cum = p[0]
    first = cum
    for i in range(DEPTH):
        if i > 0:
            cum = cum + p[i]
        o_ref[i:i + 1, :] = cum - first


def _lower_bounds(logits):
    return pl.pallas_call(
        _lb_kernel,
        out_shape=jax.ShapeDtypeStruct(logits.shape, F32),
        name="hgrn_lower_bounds",
    )(logits.astype(F32))


FFN_TM = 1024
FFN_CHAINS = 4
FFN_COL_CHUNKS = ((0, 1024), (1024, 2048), (2048, FFN_HIDDEN))


def _ffn_kernel(x_ref, mod_ref, nw_ref, w13_ref, w2_ref, *rest, mod_off, final, in_natural):
    if final:
        fnw_ref, o_ref = rest
    else:
        (o_ref,) = rest
    shift = mod_ref[mod_off:mod_off + 1, :]
    scale = mod_ref[mod_off + 1:mod_off + 2, :]
    gate = mod_ref[mod_off + 2:mod_off + 3, :]
    rows = FFN_TM // POSITIONS // FFN_CHAINS
    for r in range(FFN_CHAINS):
        rs = slice(r * rows, (r + 1) * rows)
        if in_natural:
            nat = x_ref[r * rows * POSITIONS:(r + 1) * rows * POSITIONS, :]
            x = pltpu.einshape("(cj)d->(jc)d", nat, j=POSITIONS)
        else:
            x = jnp.concatenate(
                [x_ref[rs, j * D_MODEL:(j + 1) * D_MODEL] for j in range(POSITIONS)], axis=0)
        hb = _rms_modulate(x, nw_ref[...], scale, shift).astype(BF16)
        acc = None
        for c0, c1 in FFN_COL_CHUNKS:
            a = jnp.dot(hb, w13_ref[:, c0:c1], preferred_element_type=F32)
            b = jnp.dot(hb, w13_ref[:, FFN_HIDDEN + c0:FFN_HIDDEN + c1],
                        preferred_element_type=F32)
            ha = 0.5 * a
            act = ((ha + ha * jnp.tanh(ha)) * b).astype(BF16)
            part = jnp.dot(act, w2_ref[c0:c1, :], preferred_element_type=F32)
            acc = part if acc is None else acc + part
        y = x + (0.5 * (1.0 + gate)) * acc
        if final:
            ms = jnp.mean(y * y, axis=-1, keepdims=True)
            y = y * lax.rsqrt(ms + EPS) * fnw_ref[...]
            o_ref[r * rows * POSITIONS:(r + 1) * rows * POSITIONS, :] = pltpu.einshape(
                "(jc)d->(cj)d", y, j=POSITIONS)
        else:
            for j in range(POSITIONS):
                o_ref[rs, j * D_MODEL:(j + 1) * D_MODEL] = y[j * rows:(j + 1) * rows, :]


def _ffn(h_in, mod, nw, w13, w2, *, layer, half, seq, final_w=None, in_natural=False):
    mod_off = 6 * half
    n = h_in.shape[0] if in_natural else h_in.shape[0] * POSITIONS
    tm = FFN_TM
    tiles_per_seq = seq // tm
    final = final_w is not None
    tile16 = pl.BlockSpec((tm // POSITIONS, CHUNK_WIDTH), lambda i: (i, 0))
    in_specs = [
        pl.BlockSpec((tm, D_MODEL), lambda i: (i, 0)) if in_natural else tile16,
        pl.BlockSpec((None, N_ADA, D_MODEL), lambda i: (i // tiles_per_seq, 0, 0)),
        pl.BlockSpec((1, D_MODEL), lambda i: (0, 0)),
        pl.BlockSpec((None, None, D_MODEL, 2 * FFN_HIDDEN), lambda i: (layer, half, 0, 0),
                     pipeline_mode=pl.Buffered(1)),
        pl.BlockSpec((None, None, FFN_HIDDEN, D_MODEL), lambda i: (layer, half, 0, 0),
                     pipeline_mode=pl.Buffered(1)),
    ]
    args = [h_in, mod, nw, w13, w2]
    if final:
        in_specs.append(pl.BlockSpec((1, D_MODEL), lambda i: (0, 0)))
        args.append(final_w)
    return pl.pallas_call(
        functools.partial(_ffn_kernel, mod_off=mod_off, final=final, in_natural=in_natural),
        grid=(n // tm,),
        in_specs=in_specs,
        out_specs=pl.BlockSpec((tm, D_MODEL), lambda i: (i, 0)) if final else tile16,
        out_shape=jax.ShapeDtypeStruct(
            (n, D_MODEL) if final else (n // POSITIONS, CHUNK_WIDTH), F32),
        compiler_params=pltpu.CompilerParams(
            dimension_semantics=("parallel",), vmem_limit_bytes=VMEM_LIMIT),
        name="ffn_half_step",
    )(*args)


M1_TM = 512
M1_PIECES = 2
M1_CHAINS = 2
PACKED_PARTS = 4
N_MASKS = 7


def _seg_scan_rows(x, rowid, width, reverse):
    d = 1
    y = x
    while d < width:
        if reverse:
            shifted = pltpu.roll(y, CHUNKS - d, axis=0)
            ok = (rowid & (width - 1)) < width - d
        else:
            shifted = pltpu.roll(y, d, axis=0)
            ok = (rowid & (width - 1)) >= d
        y = y * jnp.where(ok, shifted, 1.0)
        d *= 2
    return y


def _write_level_masks(mask_ref):
    pr = lax.broadcasted_iota(jnp.int32, (BLOCK, BLOCK), 0)
    pc = lax.broadcasted_iota(jnp.int32, (BLOCK, BLOCK), 1)
    same_chunk = ((pr ^ pc) & (CHUNKS - 1)) == 0
    mask_ref[0] = jnp.where(pr == pc, 1.0, 0.0)
    for beta in range(4):
        sh = 3 + beta + 1
        mask_ref[1 + beta] = jnp.where(same_chunk & ((pr >> sh) == (pc >> sh)), 1.0, 0.0)
    for gamma in range(2):
        sh = gamma + 1
        mask_ref[5 + gamma] = jnp.where(
            ((pr & (CHUNKS - 1)) >> sh) == ((pc & (CHUNKS - 1)) >> sh), 1.0, 0.0)


N_LEVELS = 7
HGRN_STAGED_BLOCKS = 1
LEVEL_TARGETS = (
    [[j for j in range(POSITIONS) if j & (1 << beta)] for beta in range(4)]
    + [list(range(POSITIONS))] * 3)
LEVEL_MASK = (1, 2, 3, 4, 5, 6, None)


def _hgrn_staged(blocks, st_ref, mask_ref, qs_scr, ks_scr, p_scr, q2_scr, tot_scr, o_ref):
    n_units = len(blocks) * N_HEADS
    rowid = lax.broadcasted_iota(jnp.int32, (CHUNKS, HEAD_DIM), 0)
    zero = jnp.zeros((CHUNKS, HEAD_DIM), F32)

    def slabs(val):
        return [val[j * CHUNKS:(j + 1) * CHUNKS, :] for j in range(POSITIONS)]

    def cat(sl):
        return jnp.concatenate(sl, axis=0)

    def mask_rows(level, j):
        return mask_ref[level, j * CHUNKS:(j + 1) * CHUNKS, :]

    for u in range(n_units):
        h = u % N_HEADS
        f, q, k, _, _ = blocks[u // N_HEADS]
        ls = slice(h * HEAD_DIM, (h + 1) * HEAD_DIM)
        fs, qs, ks = slabs(f[:, ls]), slabs(q[:, ls]), slabs(k[:, ls])
        for j in range(POSITIONS):
            p_scr[u, j * CHUNKS:(j + 1) * CHUNKS, :] = (
                jnp.sum(qs[j] * ks[j], axis=-1, keepdims=True) * mask_rows(0, j))

        for beta in range(4):
            w = 1 << beta
            a = [None] * POSITIONS
            for j in range(POSITIONS):
                if j & w:
                    a[j] = fs[j] if (j & (w - 1)) == 0 else a[j - 1] * fs[j]
            b = [None] * POSITIONS
            for j in range(POSITIONS - 1, -1, -1):
                if not (j & w) and (j & (w - 1)) != w - 1:
                    b[j] = fs[j + 1] if b[j + 1] is None else b[j + 1] * fs[j + 1]
            upper = LEVEL_TARGETS[beta]
            qs_scr[u, beta, 0:len(upper) * CHUNKS, :] = cat(
                [qs[j] * a[j] for j in upper]).astype(BF16)
            ks_scr[u, beta] = cat(
                [zero if (j & w) else (ks[j] if b[j] is None else ks[j] * b[j])
                 for j in range(POSITIONS)])

        gcum = [fs[0]]
        for j in range(1, POSITIONS):
            gcum.append(gcum[-1] * fs[j])
        hsuf = [None] * POSITIONS
        for j in range(POSITIONS - 2, -1, -1):
            hsuf[j] = fs[j + 1] if hsuf[j + 1] is None else hsuf[j + 1] * fs[j + 1]
        g = gcum[-1]
        qg = [qs[j] * gcum[j] for j in range(POSITIONS)]
        kh = [ks[j] if hsuf[j] is None else ks[j] * hsuf[j] for j in range(POSITIONS)]
        g_prev = pltpu.roll(g, 1, axis=0)
        g_next = pltpu.roll(g, CHUNKS - 1, axis=0)

        for gamma in range(3):
            w = 1 << gamma
            low = rowid & (w - 1)
            ca = _seg_scan_rows(jnp.where(low == 0, 1.0, g_prev), rowid, w, False)
            ca = jnp.where((rowid & w) != 0, ca, 0.0)
            cb = _seg_scan_rows(jnp.where(low == w - 1, 1.0, g_next), rowid, w, True)
            cb = jnp.where((rowid & w) == 0, cb, 0.0)
            qs_scr[u, 4 + gamma] = cat([x * ca for x in qg]).astype(BF16)
            ks_scr[u, 4 + gamma] = cat([x * cb for x in kh])

        cp = _seg_scan_rows(jnp.where(rowid == 0, 1.0, g_prev), rowid, CHUNKS, False)
        cs = _seg_scan_rows(jnp.where(rowid == CHUNKS - 1, 1.0, g_next), rowid, CHUNKS, True)
        q2_scr[u] = cat([x * cp for x in qg]).astype(BF16)
        ks_scr[u, N_LEVELS] = cat([x * cs for x in kh])
        tot_scr[u] = jnp.broadcast_to((cp * g)[CHUNKS - 1:CHUNKS, :], (CHUNKS, HEAD_DIM))

    for u in range(n_units):
        for lvl in range(N_LEVELS):
            targets = LEVEL_TARGETS[lvl]
            kt = jnp.transpose(ks_scr[u, lvl]).astype(BF16)
            s = jnp.dot(qs_scr[u, lvl, 0:len(targets) * CHUNKS, :], kt,
                        preferred_element_type=F32)
            for i, j in enumerate(targets):
                rs = slice(j * CHUNKS, (j + 1) * CHUNKS)
                sj = s[i * CHUNKS:(i + 1) * CHUNKS, :]
                if LEVEL_MASK[lvl] is not None:
                    sj = sj * mask_rows(LEVEL_MASK[lvl], j)
                p_scr[u, rs, :] = p_scr[u, rs, :] + sj

    for u in range(n_units):
        h = u % N_HEADS
        _, _, _, vb, orows = blocks[u // N_HEADS]
        ls = slice(h * HEAD_DIM, (h + 1) * HEAD_DIM)
        st = st_ref[h]
        lhs = jnp.concatenate([p_scr[u].astype(BF16), q2_scr[u]], axis=1)
        rhs = jnp.concatenate([vb[:, ls], st.astype(BF16)], axis=0)
        o_ref[orows, ls] = jnp.dot(lhs, rhs, preferred_element_type=F32)
        k2t = jnp.transpose(ks_scr[u, N_LEVELS]).astype(BF16)
        upd = jnp.dot(k2t, vb[:, ls], preferred_element_type=F32)
        decay_col = jnp.transpose(jnp.broadcast_to(tot_scr[u, 0:1, :], (HEAD_DIM, HEAD_DIM)))
        st_ref[h] = st * decay_col + upd


def _mix_in_kernel(x_ref, mod_ref, nw_ref, win_ref, ws_ref, bs_ref, lb_ref,
                   pk_ref, o_ref,
                   gu_scr, st_ref, mask_ref, qs_scr, ks_scr, p_scr, q2_scr, tot_scr):
    yap_ref, sog_ref, sga_ref, sgb_ref = [
        pk_ref.at[:, i * D_MODEL:(i + 1) * D_MODEL] for i in range(PACKED_PARTS)]
    tm = pk_ref.shape[0]
    nblk = tm // BLOCK // M1_CHAINS

    @pl.when(pl.program_id(1) == 0)
    def _():
        st_ref[...] = jnp.zeros_like(st_ref)

    _write_level_masks(mask_ref)
    shift = mod_ref[3:4, :]
    scale = mod_ref[4:5, :]
    row = _block_token(lax.broadcasted_iota(jnp.int32, (BLOCK, BLOCK), 0))
    col = _block_token(lax.broadcasted_iota(jnp.int32, (BLOCK, BLOCK), 1))
    causal = row >= col
    lb = lb_ref[...]
    half_span = 0.5 * (1.0 - lb)
    f_mid = lb + half_span

    def project_stage(chain, res):
        b0 = chain * nblk
        rows = slice(b0 * BLOCK, (b0 + nblk) * BLOCK)
        x = _gather_block_rows(x_ref, b0, nblk)
        hb = _rms_modulate(x, nw_ref[...], scale, shift).astype(BF16)

        def project(p0):
            z = jnp.dot(hb, win_ref[:, p0 * D_MODEL:(p0 + M1_PIECES) * D_MODEL],
                        preferred_element_type=F32)
            return [z[:, i * D_MODEL:(i + 1) * D_MODEL] for i in range(M1_PIECES)]

        zu, zv = project(0)
        gu_scr[rows, :] = _gelu_tanh(zu)
        res["gv"] = _gelu_tanh(zv)
        yield
        zq, zf = project(2)
        hq = 0.5 * zq
        res["q"] = hq + hq * jnp.tanh(hq)
        at = half_span * jnp.tanh(0.5 * zf)
        res["f"] = f_mid + at
        res["k"] = half_span - at
        yield
        zi, zog = project(4)
        res["vb"] = zi.astype(BF16)
        sog_ref[rows, :] = _sigmoid(zog).astype(BF16)
        yield
        zga, zgb = project(6)
        sga_ref[rows, :] = _sigmoid(zga).astype(BF16)
        sgb_ref[rows, :] = _sigmoid(zgb).astype(BF16)
        yield

    def gmlp_stage(chain, res):
        b0 = chain * nblk
        gv = res["gv"]
        for g in range(GMLP_GROUPS):
            ls = slice(g * HEAD_DIM, (g + 1) * HEAD_DIM)
            xg = gv[:, ls]
            mu = jnp.mean(xg, axis=-1, keepdims=True)
            xc = xg - mu
            var = jnp.mean(xc * xc, axis=-1, keepdims=True)
            vn = (xc * lax.rsqrt(var + EPS)).astype(BF16)
            rhs = jnp.concatenate(
                [vn[n * BLOCK:(n + 1) * BLOCK, :] for n in range(nblk)], axis=1)
            wc = jnp.where(causal, ws_ref[g], 0.0).astype(BF16)
            mixed = jnp.dot(wc, rhs, preferred_element_type=F32)
            bias = jnp.broadcast_to(bs_ref[:, g:g + 1], (BLOCK, BLOCK))
            for n in range(nblk):
                rs = slice((b0 + n) * BLOCK, (b0 + n + 1) * BLOCK)
                mx = mixed[:, n * BLOCK:(n + 1) * BLOCK] + bias
                yap_ref[rs, ls] = (gu_scr[rs, ls] * mx).astype(BF16)

    def hgrn_stage(chain, res):
        b0 = chain * nblk
        for n in range(0, nblk, HGRN_STAGED_BLOCKS):
            blocks = []
            for m in range(n, n + HGRN_STAGED_BLOCKS):
                br = slice(m * BLOCK, (m + 1) * BLOCK)
                orows = slice((b0 + m) * BLOCK, (b0 + m + 1) * BLOCK)
                blocks.append((res["f"][br, :], res["q"][br, :], res["k"][br, :],
                               res["vb"][br, :], orows))
            _hgrn_staged(blocks, st_ref, mask_ref, qs_scr, ks_scr, p_scr, q2_scr, tot_scr, o_ref)

    for chain in range(M1_CHAINS):
        res = {}
        for _ in project_stage(chain, res):
            pass
        gmlp_stage(chain, res)
        hgrn_stage(chain, res)


def _mix_in(h16, mod, nw, w_in, ws_p, bs_p, lb, *, layer, batch, seq):
    n = h16.shape[0] * POSITIONS
    tm = M1_TM
    tiles_per_seq = seq // tm
    tile = pl.BlockSpec((tm, D_MODEL), lambda b, t: (b * tiles_per_seq + t, 0))
    const2 = lambda b, t: (0, 0)
    f32_out = jax.ShapeDtypeStruct((n, D_MODEL), F32)
    bf_out = jax.ShapeDtypeStruct((n, PACKED_PARTS * D_MODEL), BF16)
    units = HGRN_STAGED_BLOCKS * N_HEADS
    return pl.pallas_call(
        _mix_in_kernel,
        grid=(batch, tiles_per_seq),
        in_specs=[
            pl.BlockSpec((tm // POSITIONS, CHUNK_WIDTH), lambda b, t: (b * tiles_per_seq + t, 0)),
            pl.BlockSpec((None, N_ADA, D_MODEL), lambda b, t: (b, 0, 0)),
            pl.BlockSpec((1, D_MODEL), const2),
            pl.BlockSpec((None, D_MODEL, N_PIECES * D_MODEL), lambda b, t: (layer, 0, 0),
                         pipeline_mode=pl.Buffered(1)),
            pl.BlockSpec((GMLP_GROUPS, BLOCK, BLOCK), lambda b, t: (0, 0, 0)),
            pl.BlockSpec((BLOCK, GMLP_GROUPS), const2),
            pl.BlockSpec((1, D_MODEL), const2),
        ],
        out_specs=[pl.BlockSpec((tm, PACKED_PARTS * D_MODEL),
                                lambda b, t: (b * tiles_per_seq + t, 0)), tile],
        out_shape=[bf_out, f32_out],
        scratch_shapes=[
            pltpu.VMEM((tm, D_MODEL), F32),
            pltpu.VMEM((N_HEADS, HEAD_DIM, HEAD_DIM), F32),
            pltpu.VMEM((N_MASKS, BLOCK, BLOCK), F32),
            pltpu.VMEM((units, N_LEVELS, BLOCK, HEAD_DIM), BF16),
            pltpu.VMEM((units, N_LEVELS + 1, BLOCK, HEAD_DIM), F32),
            pltpu.VMEM((units, BLOCK, BLOCK), F32),
            pltpu.VMEM((units, BLOCK, HEAD_DIM), BF16),
            pltpu.VMEM((units, CHUNKS, HEAD_DIM), F32),
        ],
        compiler_params=pltpu.CompilerParams(
            dimension_semantics=("parallel", "arbitrary"), vmem_limit_bytes=VMEM_LIMIT),
        name="mixer_in_proj_hgrn2",
    )(h16, mod, nw, w_in, ws_p, bs_p, lb)


M3_TM = 1024
M3_CHAINS = 2


def _mix_out_kernel(x_ref, mod_ref, pk_ref, o_ref, wa_ref, wb_ref, wo_ref, out_ref):
    yap_ref, sog_ref, sga_ref, sgb_ref = [
        pk_ref.at[:, i * D_MODEL:(i + 1) * D_MODEL] for i in range(PACKED_PARTS)]
    nblk = pk_ref.shape[0] // BLOCK // M3_CHAINS
    gate = mod_ref[5:6, :]
    for chain in range(M3_CHAINS):
        b0 = chain * nblk
        rows = slice(b0 * BLOCK, (b0 + nblk) * BLOCK)
        o = o_ref[rows, :]
        parts = []
        for h in range(N_HEADS):
            oh = o[:, h * HEAD_DIM:(h + 1) * HEAD_DIM]
            ms = jnp.mean(oh * oh, axis=-1, keepdims=True)
            parts.append(oh * lax.rsqrt(ms + EPS))
        on = (jnp.concatenate(parts, axis=1) * sog_ref[rows, :].astype(F32)).astype(BF16)
        ya = jnp.dot(yap_ref[rows, :], wa_ref[...], preferred_element_type=F32)
        yb = jnp.dot(on, wb_ref[...], preferred_element_type=F32)
        m = (sga_ref[rows, :].astype(F32) * ya + sgb_ref[rows, :].astype(F32) * yb).astype(BF16)
        out = jnp.dot(m, wo_ref[...], preferred_element_type=F32)
        y = _gather_block_rows(x_ref, b0, nblk) + (1.0 + gate) * out
        _scatter_block_rows(out_ref, y, b0, nblk)


def _mix_out(h16, mod, pk, o, w_a, w_b, w_o, *, layer, seq):
    n = h16.shape[0] * POSITIONS
    tm = M3_TM
    tiles_per_seq = seq // tm
    tile = pl.BlockSpec((tm, D_MODEL), lambda i: (i, 0))
    tile16 = pl.BlockSpec((tm // POSITIONS, CHUNK_WIDTH), lambda i: (i, 0))
    wspec = pl.BlockSpec((None, D_MODEL, D_MODEL), lambda i: (layer, 0, 0),
                         pipeline_mode=pl.Buffered(1))
    return pl.pallas_call(
        _mix_out_kernel,
        grid=(n // tm,),
        in_specs=[
            tile16,
            pl.BlockSpec((None, N_ADA, D_MODEL), lambda i: (i // tiles_per_seq, 0, 0)),
            pl.BlockSpec((tm, PACKED_PARTS * D_MODEL), lambda i: (i, 0)),
            tile,
            wspec, wspec, wspec,
        ],
        out_specs=tile16,
        out_shape=jax.ShapeDtypeStruct(h16.shape, F32),
        compiler_params=pltpu.CompilerParams(
            dimension_semantics=("parallel",), vmem_limit_bytes=VMEM_LIMIT),
        name="mixer_out_proj",
    )(h16, mod, pk, o, w_a, w_b, w_o)


def kernel(x, c, w_in, gmlp_ws, gmlp_b, hgrn_lb_logits, w_branch_a, w_branch_b, w_out,
           ffn_w13, ffn_w2, norm_w, ada_w, ada_b, final_norm_w):
    batch, seq, d = x.shape
    assert d == D_MODEL and seq % M1_TM == 0 and seq % FFN_TM == 0 and seq % M3_TM == 0
    n = batch * seq

    mod_all = _ada_mod(c, ada_w, ada_b).reshape(DEPTH, batch, N_ADA, D_MODEL)
    lb_all = _lower_bounds(hgrn_lb_logits)
    perm = jnp.array([_block_token(p) for p in range(BLOCK)], dtype=jnp.int32)
    ws_p = gmlp_ws[:, :, perm, :][:, :, :, perm]
    bs_p = jnp.swapaxes(gmlp_b[:, :, perm], 1, 2)

    w13_b, w2_b, w_in_b = ffn_w13.astype(BF16), ffn_w2.astype(BF16), w_in.astype(BF16)
    w_a_b, w_b_b, w_o_b = (w_branch_a.astype(BF16), w_branch_b.astype(BF16),
                           w_out.astype(BF16))

    h16 = x.reshape(n, D_MODEL)
    for l in range(DEPTH):
        mod = mod_all[l]
        nw = norm_w[l].reshape(3, 1, D_MODEL)
        h16 = _ffn(h16, mod, nw[0], w13_b, w2_b, layer=l, half=0, seq=seq,
                   in_natural=(l == 0))
        pk, o = _mix_in(h16, mod, nw[1], w_in_b, ws_p[l], bs_p[l],
                        lb_all[l].reshape(1, D_MODEL), layer=l, batch=batch, seq=seq)
        h16 = _mix_out(h16, mod, pk, o, w_a_b, w_b_b, w_o_b, layer=l, seq=seq)
        final_w = final_norm_w.reshape(1, D_MODEL) if l == DEPTH - 1 else None
        h16 = _ffn(h16, mod, nw[2], w13_b, w2_b, layer=l, half=1, seq=seq, final_w=final_w)
    return h16.reshape(batch, seq, D_MODEL)
```

```python
import functools

import jax
import jax.numpy as jnp
from jax import lax
from jax.experimental import pallas as pl
from jax.experimental.pallas import tpu as pltpu

F32 = jnp.float32
BF16 = jnp.bfloat16

D_MODEL = 1024
DEPTH = 4
FFN_HIDDEN = 2816
N_ADA = 9
EPS = 1e-6
GMLP_GROUPS = 8
HEAD_DIM = 128
N_HEADS = D_MODEL // HEAD_DIM
N_PIECES = 8

BLOCK = 128
POSITIONS = 16
CHUNKS = BLOCK // POSITIONS
CHUNK_WIDTH = POSITIONS * D_MODEL

VMEM_LIMIT = 56 * 1024 * 1024


def _sigmoid(x):
    return 1.0 / (1.0 + jnp.exp(-x))


def _sigmoid_t(x):
    return 0.5 * jnp.tanh(0.5 * x) + 0.5


def _gelu_tanh(x):
    c = 0.7978845608028654
    hx = 0.5 * x
    return hx + hx * jnp.tanh(x * (c + (c * 0.044715) * (x * x)))


def _rms_modulate(x, nw, scale, shift):
    ms = jnp.mean(x * x, axis=-1, keepdims=True)
    gain = nw * (1.0 + scale)
    return (x * lax.rsqrt(ms + EPS)) * gain + shift


def _block_token(p):
    return (p % CHUNKS) * POSITIONS + p // CHUNKS


def _gather_block_rows(x_ref, b0, n_blocks):
    pieces = [x_ref[n * CHUNKS:(n + 1) * CHUNKS, j * D_MODEL:(j + 1) * D_MODEL]
              for n in range(b0, b0 + n_blocks) for j in range(POSITIONS)]
    return jnp.concatenate(pieces, axis=0)


def _scatter_block_rows(out_ref, y, b0, n_blocks):
    for n in range(n_blocks):
        for j in range(POSITIONS):
            r = (n * POSITIONS + j) * CHUNKS
            out_ref[(b0 + n) * CHUNKS:(b0 + n + 1) * CHUNKS,
                    j * D_MODEL:(j + 1) * D_MODEL] = y[r:r + CHUNKS, :]


def _ada_kernel(c_ref, w_ref, b_ref, o_ref):
    c = c_ref[...]
    ca = c * _sigmoid(c)
    o_ref[...] = jnp.dot(ca, w_ref[...], preferred_element_type=F32,
                         precision=lax.Precision.HIGHEST) + b_ref[...]


ADA_GROUP = 3


def _ada_mod(c, ada_w, ada_b):
    b = c.shape[0]
    return pl.pallas_call(
        _ada_kernel,
        grid=(DEPTH, N_ADA // ADA_GROUP),
        in_specs=[
            pl.BlockSpec((b, D_MODEL), lambda l, j: (0, 0)),
            pl.BlockSpec((None, D_MODEL, ADA_GROUP * D_MODEL), lambda l, j: (l, 0, j)),
            pl.BlockSpec((None, 1, ADA_GROUP * D_MODEL), lambda l, j: (l, 0, j)),
        ],
        out_specs=pl.BlockSpec((None, b, ADA_GROUP * D_MODEL), lambda l, j: (l, 0, j)),
        out_shape=jax.ShapeDtypeStruct((DEPTH, b, N_ADA * D_MODEL), F32),
        compiler_params=pltpu.CompilerParams(
            dimension_semantics=("parallel", "parallel"), vmem_limit_bytes=VMEM_LIMIT),
        name="ada_mod",
    )(c, ada_w, ada_b.reshape(DEPTH, 1, N_ADA * D_MODEL))


def _lb_kernel(x_ref, o_ref):
    x = x_ref[...]
    rows = [x[i:i + 1, :] for i in range(DEPTH)]
    m = rows[0]
    for r in rows[1:]:
        m = jnp.maximum(m, r)
    e = [jnp.exp(r - m) for r in rows]
    tot = e[0]
    for t in e[1:]:
        tot = tot + t
    p = [t / tot for t in e]
    cum = p[0]
    first = cum
    for i in range(DEPTH):
        if i > 0:
            cum = cum + p[i]
        o_ref[i:i + 1, :] = cum - first


def _lower_bounds(logits):
    return pl.pallas_call(
        _lb_kernel,
        out_shape=jax.ShapeDtypeStruct(logits.shape, F32),
        name="hgrn_lower_bounds",
    )(logits.astype(F32))


FFN_TM = 1024
FFN_CHAINS = 4
FFN_COL_CHUNKS = ((0, 1024), (1024, 2048), (2048, FFN_HIDDEN))


def _ffn_kernel(x_ref, mod_ref, nw_ref, w13_ref, w2_ref, *rest, mod_off, final, in_natural):
    if final:
        fnw_ref, o_ref = rest
    else:
        (o_ref,) = rest
    shift = mod_ref[mod_off:mod_off + 1, :]
    scale = mod_ref[mod_off + 1:mod_off + 2, :]
    gate = mod_ref[mod_off + 2:mod_off + 3, :]
    rows = FFN_TM // POSITIONS // FFN_CHAINS
    for r in range(FFN_CHAINS):
        rs = slice(r * rows, (r + 1) * rows)
        if in_natural:
            nat = x_ref[r * rows * POSITIONS:(r + 1) * rows * POSITIONS, :]
            x = pltpu.einshape("(cj)d->(jc)d", nat, j=POSITIONS)
        else:
            x = jnp.concatenate(
                [x_ref[rs, j * D_MODEL:(j + 1) * D_MODEL] for j in range(POSITIONS)], axis=0)
        hb = _rms_modulate(x, nw_ref[...], scale, shift).astype(BF16)
        acc = None
        for c0, c1 in FFN_COL_CHUNKS:
            a = jnp.dot(hb, w13_ref[:, c0:c1], preferred_element_type=F32)
            b = jnp.dot(hb, w13_ref[:, FFN_HIDDEN + c0:FFN_HIDDEN + c1],
                        preferred_element_type=F32)
            act = (a * _sigmoid_t(a) * b).astype(BF16)
            part = jnp.dot(act, w2_ref[c0:c1, :], preferred_element_type=F32)
            acc = part if acc is None else acc + part
        y = x + (0.5 * (1.0 + gate)) * acc
        if final:
            ms = jnp.mean(y * y, axis=-1, keepdims=True)
            y = y * lax.rsqrt(ms + EPS) * fnw_ref[...]
            o_ref[r * rows * POSITIONS:(r + 1) * rows * POSITIONS, :] = pltpu.einshape(
                "(jc)d->(cj)d", y, j=POSITIONS)
        else:
            for j in range(POSITIONS):
                o_ref[rs, j * D_MODEL:(j + 1) * D_MODEL] = y[j * rows:(j + 1) * rows, :]


def _ffn(h_in, mod, nw, w13, w2, *, layer, half, seq, final_w=None, in_natural=False):
    mod_off = 6 * half
    n = h_in.shape[0] if in_natural else h_in.shape[0] * POSITIONS
    tm = FFN_TM
    tiles_per_seq = seq // tm
    final = final_w is not None
    tile16 = pl.BlockSpec((tm // POSITIONS, CHUNK_WIDTH), lambda i: (i, 0))
    in_specs = [
        pl.BlockSpec((tm, D_MODEL), lambda i: (i, 0)) if in_natural else tile16,
        pl.BlockSpec((None, N_ADA, D_MODEL), lambda i: (i // tiles_per_seq, 0, 0)),
        pl.BlockSpec((1, D_MODEL), lambda i: (0, 0)),
        pl.BlockSpec((None, None, D_MODEL, 2 * FFN_HIDDEN), lambda i: (layer, half, 0, 0),
                     pipeline_mode=pl.Buffered(1)),
        pl.BlockSpec((None, None, FFN_HIDDEN, D_MODEL), lambda i: (layer, half, 0, 0),
                     pipeline_mode=pl.Buffered(1)),
    ]
    args = [h_in, mod, nw, w13, w2]
    if final:
        in_specs.append(pl.BlockSpec((1, D_MODEL), lambda i: (0, 0)))
        args.append(final_w)
    return pl.pallas_call(
        functools.partial(_ffn_kernel, mod_off=mod_off, final=final, in_natural=in_natural),
        grid=(n // tm,),
        in_specs=in_specs,
        out_specs=pl.BlockSpec((tm, D_MODEL), lambda i: (i, 0)) if final else tile16,
        out_shape=jax.ShapeDtypeStruct(
            (n, D_MODEL) if final else (n // POSITIONS, CHUNK_WIDTH), F32),
        compiler_params=pltpu.CompilerParams(
            dimension_semantics=("parallel",), vmem_limit_bytes=VMEM_LIMIT),
        name="ffn_half_step",
    )(*args)


M1_TM = 512
M1_PIECES = 2
M1_CHAINS = 2
PACKED_PARTS = 4
N_MASKS = 7


def _seg_scan_rows(x, rowid, width, reverse):
    d = 1
    y = x
    while d < width:
        if reverse:
            shifted = pltpu.roll(y, CHUNKS - d, axis=0)
            ok = (rowid & (width - 1)) < width - d
        else:
            shifted = pltpu.roll(y, d, axis=0)
            ok = (rowid & (width - 1)) >= d
        y = y * jnp.where(ok, shifted, 1.0)
        d *= 2
    return y


def _write_level_masks(mask_ref):
    pr = lax.broadcasted_iota(jnp.int32, (BLOCK, BLOCK), 0)
    pc = lax.broadcasted_iota(jnp.int32, (BLOCK, BLOCK), 1)
    same_chunk = ((pr ^ pc) & (CHUNKS - 1)) == 0
    mask_ref[0] = jnp.where(pr == pc, 1.0, 0.0)
    for beta in range(4):
        sh = 3 + beta + 1
        mask_ref[1 + beta] = jnp.where(same_chunk & ((pr >> sh) == (pc >> sh)), 1.0, 0.0)
    for gamma in range(2):
        sh = gamma + 1
        mask_ref[5 + gamma] = jnp.where(
            ((pr & (CHUNKS - 1)) >> sh) == ((pc & (CHUNKS - 1)) >> sh), 1.0, 0.0)


N_LEVELS = 7
HGRN_STAGED_BLOCKS = 1
LEVEL_TARGETS = (
    [[j for j in range(POSITIONS) if j & (1 << beta)] for beta in range(4)]
    + [list(range(POSITIONS))] * 3)
LEVEL_MASK = (1, 2, 3, 4, 5, 6, None)


def _hgrn_staged(blocks, st_ref, mask_ref, qs_scr, ks_scr, p_scr, q2_scr, tot_scr, o_ref):
    n_units = len(blocks) * N_HEADS
    rowid = lax.broadcasted_iota(jnp.int32, (CHUNKS, HEAD_DIM), 0)
    zero = jnp.zeros((CHUNKS, HEAD_DIM), F32)

    def slabs(val):
        return [val[j * CHUNKS:(j + 1) * CHUNKS, :] for j in range(POSITIONS)]

    def cat(sl):
        return jnp.concatenate(sl, axis=0)

    def mask_rows(level, j):
        return mask_ref[level, j * CHUNKS:(j + 1) * CHUNKS, :]

    for u in range(n_units):
        h = u % N_HEADS
        f, q, k, _, _ = blocks[u // N_HEADS]
        ls = slice(h * HEAD_DIM, (h + 1) * HEAD_DIM)
        fs, qs, ks = slabs(f[:, ls]), slabs(q[:, ls]), slabs(k[:, ls])
        for j in range(POSITIONS):
            p_scr[u, j * CHUNKS:(j + 1) * CHUNKS, :] = (
                jnp.sum(qs[j] * ks[j], axis=-1, keepdims=True) * mask_rows(0, j))

        for beta in range(4):
            w = 1 << beta
            a = [None] * POSITIONS
            for j in range(POSITIONS):
                if j & w:
                    a[j] = fs[j] if (j & (w - 1)) == 0 else a[j - 1] * fs[j]
            b = [None] * POSITIONS
            for j in range(POSITIONS - 1, -1, -1):
                if not (j & w) and (j & (w - 1)) != w - 1:
                    b[j] = fs[j + 1] if b[j + 1] is None else b[j + 1] * fs[j + 1]
            upper = LEVEL_TARGETS[beta]
            qs_scr[u, beta, 0:len(upper) * CHUNKS, :] = cat(
                [qs[j] * a[j] for j in upper]).astype(BF16)
            ks_scr[u, beta] = cat(
                [zero if (j & w) else (ks[j] if b[j] is None else ks[j] * b[j])
                 for j in range(POSITIONS)])

        gcum = [fs[0]]
        for j in range(1, POSITIONS):
            gcum.append(gcum[-1] * fs[j])
        hsuf = [None] * POSITIONS
        for j in range(POSITIONS - 2, -1, -1):
            hsuf[j] = fs[j + 1] if hsuf[j + 1] is None else hsuf[j + 1] * fs[j + 1]
        g = gcum[-1]
        qg = [qs[j] * gcum[j] for j in range(POSITIONS)]
        kh = [ks[j] if hsuf[j] is None else ks[j] * hsuf[j] for j in range(POSITIONS)]
        g_prev = pltpu.roll(g, 1, axis=0)
        g_next = pltpu.roll(g, CHUNKS - 1, axis=0)

        for gamma in range(3):
            w = 1 << gamma
            low = rowid & (w - 1)
            ca = _seg_scan_rows(jnp.where(low == 0, 1.0, g_prev), rowid, w, False)
            ca = jnp.where((rowid & w) != 0, ca, 0.0)
            cb = _seg_scan_rows(jnp.where(low == w - 1, 1.0, g_next), rowid, w, True)
            cb = jnp.where((rowid & w) == 0, cb, 0.0)
            qs_scr[u, 4 + gamma] = cat([x * ca for x in qg]).astype(BF16)
            ks_scr[u, 4 + gamma] = cat([x * cb for x in kh])

        cp = _seg_scan_rows(jnp.where(rowid == 0, 1.0, g_prev), rowid, CHUNKS, False)
        cs = _seg_scan_rows(jnp.where(rowid == CHUNKS - 1, 1.0, g_next), rowid, CHUNKS, True)
        q2_scr[u] = cat([x * cp for x in qg]).astype(BF16)
        ks_scr[u, N_LEVELS] = cat([x * cs for x in kh])
        tot_scr[u] = jnp.broadcast_to((cp * g)[CHUNKS - 1:CHUNKS, :], (CHUNKS, HEAD_DIM))

    for u in range(n_units):
        for lvl in range(N_LEVELS):
            targets = LEVEL_TARGETS[lvl]
            kt = jnp.transpose(ks_scr[u, lvl]).astype(BF16)
            s = jnp.dot(qs_scr[u, lvl, 0:len(targets) * CHUNKS, :], kt,
                        preferred_element_type=F32)
            for i, j in enumerate(targets):
                rs = slice(j * CHUNKS, (j + 1) * CHUNKS)
                sj = s[i * CHUNKS:(i + 1) * CHUNKS, :]
                if LEVEL_MASK[lvl] is not None:
                    sj = sj * mask_rows(LEVEL_MASK[lvl], j)
                p_scr[u, rs, :] = p_scr[u, rs, :] + sj

    for u in range(n_units):
        h = u % N_HEADS
        _, _, _, vb, orows = blocks[u // N_HEADS]
        ls = slice(h * HEAD_DIM, (h + 1) * HEAD_DIM)
        st = st_ref[h]
        lhs = jnp.concatenate([p_scr[u].astype(BF16), q2_scr[u]], axis=1)
        rhs = jnp.concatenate([vb[:, ls], st.astype(BF16)], axis=0)
        o_ref[orows, ls] = jnp.dot(lhs, rhs, preferred_element_type=F32)
        k2t = jnp.transpose(ks_scr[u, N_LEVELS]).astype(BF16)
        upd = jnp.dot(k2t, vb[:, ls], preferred_element_type=F32)
        decay_col = jnp.transpose(jnp.broadcast_to(tot_scr[u, 0:1, :], (HEAD_DIM, HEAD_DIM)))
        st_ref[h] = st * decay_col + upd


def _mix_in_kernel(x_ref, mod_ref, nw_ref, win_ref, ws_ref, bs_ref, lb_ref,
                   pk_ref, o_ref,
                   gu_scr, st_ref, mask_ref, qs_scr, ks_scr, p_scr, q2_scr, tot_scr):
    yap_ref, sog_ref, sga_ref, sgb_ref = [
        pk_ref.at[:, i * D_MODEL:(i + 1) * D_MODEL] for i in range(PACKED_PARTS)]
    tm = pk_ref.shape[0]
    nblk = tm // BLOCK // M1_CHAINS

    @pl.when(pl.program_id(1) == 0)
    def _():
        st_ref[...] = jnp.zeros_like(st_ref)

    _write_level_masks(mask_ref)
    shift = mod_ref[3:4, :]
    scale = mod_ref[4:5, :]
    row = _block_token(lax.broadcasted_iota(jnp.int32, (BLOCK, BLOCK), 0))
    col = _block_token(lax.broadcasted_iota(jnp.int32, (BLOCK, BLOCK), 1))
    causal = row >= col
    lb = lb_ref[...]
    half_span = 0.5 * (1.0 - lb)
    f_mid = lb + half_span

    def project_stage(chain, res):
        b0 = chain * nblk
        rows = slice(b0 * BLOCK, (b0 + nblk) * BLOCK)
        x = _gather_block_rows(x_ref, b0, nblk)
        hb = _rms_modulate(x, nw_ref[...], scale, shift).astype(BF16)

        def project(p0):
            z = jnp.dot(hb, win_ref[:, p0 * D_MODEL:(p0 + M1_PIECES) * D_MODEL],
                        preferred_element_type=F32)
            return [z[:, i * D_MODEL:(i + 1) * D_MODEL] for i in range(M1_PIECES)]

        zu, zv = project(0)
        gu_scr[rows, :] = _gelu_tanh(zu)
        res["gv"] = _gelu_tanh(zv)
        yield
        zq, zf = project(2)
        hq = 0.5 * zq
        res["q"] = hq + hq * jnp.tanh(hq)
        at = half_span * jnp.tanh(0.5 * zf)
        res["f"] = f_mid + at
        res["k"] = half_span - at
        yield
        zi, zog = project(4)
        res["vb"] = zi.astype(BF16)
        sog_ref[rows, :] = jnp.tanh(0.5 * zog).astype(BF16)
        yield
        zga, zgb = project(6)
        sga_ref[rows, :] = jnp.tanh(0.5 * zga).astype(BF16)
        sgb_ref[rows, :] = jnp.tanh(0.5 * zgb).astype(BF16)
        yield

    def gmlp_stage(chain, res):
        b0 = chain * nblk
        gv = res["gv"]
        for g in range(GMLP_GROUPS):
            ls = slice(g * HEAD_DIM, (g + 1) * HEAD_DIM)
            xg = gv[:, ls]
            mu = jnp.mean(xg, axis=-1, keepdims=True)
            xc = xg - mu
            var = jnp.mean(xc * xc, axis=-1, keepdims=True)
            vn = (xc * lax.rsqrt(var + EPS)).astype(BF16)
            rhs = jnp.concatenate(
                [vn[n * BLOCK:(n + 1) * BLOCK, :] for n in range(nblk)], axis=1)
            wc = jnp.where(causal, ws_ref[g], 0.0).astype(BF16)
            mixed = jnp.dot(wc, rhs, preferred_element_type=F32)
            bias = jnp.broadcast_to(bs_ref[:, g:g + 1], (BLOCK, BLOCK))
            for n in range(nblk):
                rs = slice((b0 + n) * BLOCK, (b0 + n + 1) * BLOCK)
                mx = mixed[:, n * BLOCK:(n + 1) * BLOCK] + bias
                yap_ref[rs, ls] = (gu_scr[rs, ls] * mx).astype(BF16)

    def hgrn_stage(chain, res):
        b0 = chain * nblk
        for n in range(0, nblk, HGRN_STAGED_BLOCKS):
            blocks = []
            for m in range(n, n + HGRN_STAGED_BLOCKS):
                br = slice(m * BLOCK, (m + 1) * BLOCK)
                orows = slice((b0 + m) * BLOCK, (b0 + m + 1) * BLOCK)
                blocks.append((res["f"][br, :], res["q"][br, :], res["k"][br, :],
                               res["vb"][br, :], orows))
            _hgrn_staged(blocks, st_ref, mask_ref, qs_scr, ks_scr, p_scr, q2_scr, tot_scr, o_ref)

    for chain in range(M1_CHAINS):
        res = {}
        for _ in project_stage(chain, res):
            pass
        gmlp_stage(chain, res)
        hgrn_stage(chain, res)


def _mix_in(h16, mod, nw, w_in, ws_p, bs_p, lb, *, layer, batch, seq):
    n = h16.shape[0] * POSITIONS
    tm = M1_TM
    tiles_per_seq = seq // tm
    tile = pl.BlockSpec((tm, D_MODEL), lambda b, t: (b * tiles_per_seq + t, 0))
    const2 = lambda b, t: (0, 0)
    f32_out = jax.ShapeDtypeStruct((n, D_MODEL), F32)
    bf_out = jax.ShapeDtypeStruct((n, PACKED_PARTS * D_MODEL), BF16)
    units = HGRN_STAGED_BLOCKS * N_HEADS
    return pl.pallas_call(
        _mix_in_kernel,
        grid=(batch, tiles_per_seq),
        in_specs=[
            pl.BlockSpec((tm // POSITIONS, CHUNK_WIDTH), lambda b, t: (b * tiles_per_seq + t, 0)),
            pl.BlockSpec((None, N_ADA, D_MODEL), lambda b, t: (b, 0, 0)),
            pl.BlockSpec((1, D_MODEL), const2),
            pl.BlockSpec((None, D_MODEL, N_PIECES * D_MODEL), lambda b, t: (layer, 0, 0),
                         pipeline_mode=pl.Buffered(1)),
            pl.BlockSpec((GMLP_GROUPS, BLOCK, BLOCK), lambda b, t: (0, 0, 0)),
            pl.BlockSpec((BLOCK, GMLP_GROUPS), const2),
            pl.BlockSpec((1, D_MODEL), const2),
        ],
        out_specs=[pl.BlockSpec((tm, PACKED_PARTS * D_MODEL),
                                lambda b, t: (b * tiles_per_seq + t, 0)), tile],
        out_shape=[bf_out, f32_out],
        scratch_shapes=[
            pltpu.VMEM((tm, D_MODEL), F32),
            pltpu.VMEM((N_HEADS, HEAD_DIM, HEAD_DIM), F32),
            pltpu.VMEM((N_MASKS, BLOCK, BLOCK), F32),
            pltpu.VMEM((units, N_LEVELS, BLOCK, HEAD_DIM), BF16),
            pltpu.VMEM((units, N_LEVELS + 1, BLOCK, HEAD_DIM), F32),
            pltpu.VMEM((units, BLOCK, BLOCK), F32),
            pltpu.VMEM((units, BLOCK, HEAD_DIM), BF16),
            pltpu.VMEM((units, CHUNKS, HEAD_DIM), F32),
        ],
        compiler_params=pltpu.CompilerParams(
            dimension_semantics=("parallel", "arbitrary"), vmem_limit_bytes=VMEM_LIMIT),
        name="mixer_in_proj_hgrn2",
    )(h16, mod, nw, w_in, ws_p, bs_p, lb)


M3_TM = 1024
M3_CHAINS = 2


def _mix_out_kernel(x_ref, mod_ref, pk_ref, o_ref, wa_ref, wb_ref, wo_ref, out_ref):
    yap_ref, sog_ref, sga_ref, sgb_ref = [
        pk_ref.at[:, i * D_MODEL:(i + 1) * D_MODEL] for i in range(PACKED_PARTS)]
    nblk = pk_ref.shape[0] // BLOCK // M3_CHAINS
    gate = mod_ref[5:6, :]
    for chain in range(M3_CHAINS):
        b0 = chain * nblk
        rows = slice(b0 * BLOCK, (b0 + nblk) * BLOCK)
        o = o_ref[rows, :]
        parts = []
        for h in range(N_HEADS):
            oh = o[:, h * HEAD_DIM:(h + 1) * HEAD_DIM]
            ms = jnp.mean(oh * oh, axis=-1, keepdims=True)
            parts.append(oh * (0.5 * lax.rsqrt(ms + EPS)))
        hn = jnp.concatenate(parts, axis=1)
        on = (hn + hn * sog_ref[rows, :].astype(F32)).astype(BF16)
        hya = 0.5 * jnp.dot(yap_ref[rows, :], wa_ref[...], preferred_element_type=F32)
        hyb = 0.5 * jnp.dot(on, wb_ref[...], preferred_element_type=F32)
        m = ((hya + hya * sga_ref[rows, :].astype(F32))
             + (hyb + hyb * sgb_ref[rows, :].astype(F32))).astype(BF16)
        out = jnp.dot(m, wo_ref[...], preferred_element_type=F32)
        y = _gather_block_rows(x_ref, b0, nblk) + (1.0 + gate) * out
        _scatter_block_rows(out_ref, y, b0, nblk)


def _mix_out(h16, mod, pk, o, w_a, w_b, w_o, *, layer, seq):
    n = h16.shape[0] * POSITIONS
    tm = M3_TM
    tiles_per_seq = seq // tm
    tile = pl.BlockSpec((tm, D_MODEL), lambda i: (i, 0))
    tile16 = pl.BlockSpec((tm // POSITIONS, CHUNK_WIDTH), lambda i: (i, 0))
    wspec = pl.BlockSpec((None, D_MODEL, D_MODEL), lambda i: (layer, 0, 0),
                         pipeline_mode=pl.Buffered(1))
    return pl.pallas_call(
        _mix_out_kernel,
        grid=(n // tm,),
        in_specs=[
            tile16,
            pl.BlockSpec((None, N_ADA, D_MODEL), lambda i: (i // tiles_per_seq, 0, 0)),
            pl.BlockSpec((tm, PACKED_PARTS * D_MODEL), lambda i: (i, 0)),
            tile,
            wspec, wspec, wspec,
        ],
        out_specs=tile16,
        out_shape=jax.ShapeDtypeStruct(h16.shape, F32),
        compiler_params=pltpu.CompilerParams(
            dimension_semantics=("parallel",), vmem_limit_bytes=VMEM_LIMIT),
        name="mixer_out_proj",
    )(h16, mod, pk, o, w_a, w_b, w_o)


def kernel(x, c, w_in, gmlp_ws, gmlp_b, hgrn_lb_logits, w_branch_a, w_branch_b, w_out,
           ffn_w13, ffn_w2, norm_w, ada_w, ada_b, final_norm_w):
    batch, seq, d = x.shape
    assert d == D_MODEL and seq % M1_TM == 0 and seq % FFN_TM == 0 and seq % M3_TM == 0
    n = batch * seq

    mod_all = _ada_mod(c, ada_w, ada_b).reshape(DEPTH, batch, N_ADA, D_MODEL)
    lb_all = _lower_bounds(hgrn_lb_logits)
    perm = jnp.array([_block_token(p) for p in range(BLOCK)], dtype=jnp.int32)
    ws_p = gmlp_ws[:, :, perm, :][:, :, :, perm]
    bs_p = jnp.swapaxes(gmlp_b[:, :, perm], 1, 2)

    w13_b, w2_b, w_in_b = ffn_w13.astype(BF16), ffn_w2.astype(BF16), w_in.astype(BF16)
    w_a_b, w_b_b, w_o_b = (w_branch_a.astype(BF16), w_branch_b.astype(BF16),
                           w_out.astype(BF16))

    h16 = x.reshape(n, D_MODEL)
    for l in range(DEPTH):
        mod = mod_all[l]
        nw = norm_w[l].reshape(3, 1, D_MODEL)
        h16 = _ffn(h16, mod, nw[0], w13_b, w2_b, layer=l, half=0, seq=seq,
                   in_natural=(l == 0))
        pk, o = _mix_in(h16, mod, nw[1], w_in_b, ws_p[l], bs_p[l],
                        lb_all[l].reshape(1, D_MODEL), layer=l, batch=batch, seq=seq)
        h16 = _mix_out(h16, mod, pk, o, w_a_b, w_b_b, w_o_b, layer=l, seq=seq)
        final_w = final_norm_w.reshape(1, D_MODEL) if l == DEPTH - 1 else None
        h16 = _ffn(h16, mod, nw[2], w13_b, w2_b, layer=l, half=1, seq=seq, final_w=final_w)
    return h16.reshape(batch, seq, D_MODEL)
```

```python
import functools

import jax
import jax.numpy as jnp
from jax import lax
from jax.experimental import pallas as pl
from jax.experimental.pallas import tpu as pltpu

F32 = jnp.float32
BF16 = jnp.bfloat16

D_MODEL = 1024
DEPTH = 4
FFN_HIDDEN = 2816
N_ADA = 9
EPS = 1e-6
GMLP_GROUPS = 8
HEAD_DIM = 128
N_HEADS = D_MODEL // HEAD_DIM
N_PIECES = 8

BLOCK = 128
POSITIONS = 16
CHUNKS = BLOCK // POSITIONS
CHUNK_WIDTH = POSITIONS * D_MODEL

VMEM_LIMIT = 56 * 1024 * 1024


def _sigmoid(x):
    return 1.0 / (1.0 + jnp.exp(-x))


def _sigmoid_t(x):
    return 0.5 * jnp.tanh(0.5 * x) + 0.5


def _gelu_tanh(x):
    c = 0.7978845608028654
    hx = 0.5 * x
    return hx + hx * jnp.tanh(x * (c + (c * 0.044715) * (x * x)))


def _rms_modulate(x, nw, scale, shift):
    ms = jnp.mean(x * x, axis=-1, keepdims=True)
    gain = nw * (1.0 + scale)
    return (x * lax.rsqrt(ms + EPS)) * gain + shift


def _block_token(p):
    return (p % CHUNKS) * POSITIONS + p // CHUNKS


def _gather_block_rows(x_ref, b0, n_blocks):
    pieces = [x_ref[n * CHUNKS:(n + 1) * CHUNKS, j * D_MODEL:(j + 1) * D_MODEL]
              for n in range(b0, b0 + n_blocks) for j in range(POSITIONS)]
    return jnp.concatenate(pieces, axis=0)


def _scatter_block_rows(out_ref, y, b0, n_blocks):
    for n in range(n_blocks):
        for j in range(POSITIONS):
            r = (n * POSITIONS + j) * CHUNKS
            out_ref[(b0 + n) * CHUNKS:(b0 + n + 1) * CHUNKS,
                    j * D_MODEL:(j + 1) * D_MODEL] = y[r:r + CHUNKS, :]


def _ada_kernel(c_ref, w_ref, b_ref, o_ref):
    c = c_ref[...]
    ca = c * _sigmoid(c)
    o_ref[...] = jnp.dot(ca, w_ref[...], preferred_element_type=F32,
                         precision=lax.Precision.HIGHEST) + b_ref[...]


ADA_GROUP = 3


def _ada_mod(c, ada_w, ada_b):
    b = c.shape[0]
    return pl.pallas_call(
        _ada_kernel,
        grid=(DEPTH, N_ADA // ADA_GROUP),
        in_specs=[
            pl.BlockSpec((b, D_MODEL), lambda l, j: (0, 0)),
            pl.BlockSpec((None, D_MODEL, ADA_GROUP * D_MODEL), lambda l, j: (l, 0, j)),
            pl.BlockSpec((None, 1, ADA_GROUP * D_MODEL), lambda l, j: (l, 0, j)),
        ],
        out_specs=pl.BlockSpec((None, b, ADA_GROUP * D_MODEL), lambda l, j: (l, 0, j)),
        out_shape=jax.ShapeDtypeStruct((DEPTH, b, N_ADA * D_MODEL), F32),
        compiler_params=pltpu.CompilerParams(
            dimension_semantics=("parallel", "parallel"), vmem_limit_bytes=VMEM_LIMIT),
        name="ada_mod",
    )(c, ada_w, ada_b.reshape(DEPTH, 1, N_ADA * D_MODEL))


def _lb_kernel(x_ref, o_ref):
    x = x_ref[...]
    rows = [x[i:i + 1, :] for i in range(DEPTH)]
    m = rows[0]
    for r in rows[1:]:
        m = jnp.maximum(m, r)
    e = [jnp.exp(r - m) for r in rows]
    tot = e[0]
    for t in e[1:]:
        tot = tot + t
    p = [t / tot for t in e]
    cum = p[0]
    first = cum
    for i in range(DEPTH):
        if i > 0:
            cum = cum + p[i]
        o_ref[i:i + 1, :] = cum - first


def _lower_bounds(logits):
    return pl.pallas_call(
        _lb_kernel,
        out_shape=jax.ShapeDtypeStruct(logits.shape, F32),
        name="hgrn_lower_bounds",
    )(logits.astype(F32))


FFN_TM = 1024
FFN_CHAINS = 4
FFN_COL_CHUNKS = ((0, 1024), (1024, 2048), (2048, FFN_HIDDEN))


def _ffn_kernel(x_ref, mod_ref, nw_ref, w13_ref, w2_ref, *rest, mod_off, final, in_natural):
    if final:
        fnw_ref, o_ref = rest
    else:
        (o_ref,) = rest
    shift = mod_ref[mod_off:mod_off + 1, :]
    scale = mod_ref[mod_off + 1:mod_off + 2, :]
    gate = mod_ref[mod_off + 2:mod_off + 3, :]
    rows = FFN_TM // POSITIONS // FFN_CHAINS
    for r in range(FFN_CHAINS):
        rs = slice(r * rows, (r + 1) * rows)
        if in_natural:
            nat = x_ref[r * rows * POSITIONS:(r + 1) * rows * POSITIONS, :]
            x = pltpu.einshape("(cj)d->(jc)d", nat, j=POSITIONS)
        else:
            x = jnp.concatenate(
                [x_ref[rs, j * D_MODEL:(j + 1) * D_MODEL] for j in range(POSITIONS)], axis=0)
        hb = _rms_modulate(x, nw_ref[...], scale, shift).astype(BF16)
        acc = None
        for c0, c1 in FFN_COL_CHUNKS:
            a = jnp.dot(hb, w13_ref[:, c0:c1], preferred_element_type=F32)
            b = jnp.dot(hb, w13_ref[:, FFN_HIDDEN + c0:FFN_HIDDEN + c1],
                        preferred_element_type=F32)
            act = (a * _sigmoid_t(a) * b).astype(BF16)
            part = jnp.dot(act, w2_ref[c0:c1, :], preferred_element_type=F32)
            acc = part if acc is None else acc + part
        y = x + (0.5 * (1.0 + gate)) * acc
        if final:
            ms = jnp.mean(y * y, axis=-1, keepdims=True)
            y = y * lax.rsqrt(ms + EPS) * fnw_ref[...]
            o_ref[r * rows * POSITIONS:(r + 1) * rows * POSITIONS, :] = pltpu.einshape(
                "(jc)d->(cj)d", y, j=POSITIONS)
        else:
            for j in range(POSITIONS):
                o_ref[rs, j * D_MODEL:(j + 1) * D_MODEL] = y[j * rows:(j + 1) * rows, :]


def _ffn(h_in, mod, nw, w13, w2, *, layer, half, seq, final_w=None, in_natural=False):
    mod_off = 6 * half
    n = h_in.shape[0] if in_natural else h_in.shape[0] * POSITIONS
    tm = FFN_TM
    tiles_per_seq = seq // tm
    final = final_w is not None
    tile16 = pl.BlockSpec((tm // POSITIONS, CHUNK_WIDTH), lambda i: (i, 0))
    in_specs = [
        pl.BlockSpec((tm, D_MODEL), lambda i: (i, 0)) if in_natural else tile16,
        pl.BlockSpec((None, N_ADA, D_MODEL), lambda i: (i // tiles_per_seq, 0, 0)),
        pl.BlockSpec((1, D_MODEL), lambda i: (0, 0)),
        pl.BlockSpec((None, None, D_MODEL, 2 * FFN_HIDDEN), lambda i: (layer, half, 0, 0),
                     pipeline_mode=pl.Buffered(1)),
        pl.BlockSpec((None, None, FFN_HIDDEN, D_MODEL), lambda i: (layer, half, 0, 0),
                     pipeline_mode=pl.Buffered(1)),
    ]
    args = [h_in, mod, nw, w13, w2]
    if final:
        in_specs.append(pl.BlockSpec((1, D_MODEL), lambda i: (0, 0)))
        args.append(final_w)
    return pl.pallas_call(
        functools.partial(_ffn_kernel, mod_off=mod_off, final=final, in_natural=in_natural),
        grid=(n // tm,),
        in_specs=in_specs,
        out_specs=pl.BlockSpec((tm, D_MODEL), lambda i: (i, 0)) if final else tile16,
        out_shape=jax.ShapeDtypeStruct(
            (n, D_MODEL) if final else (n // POSITIONS, CHUNK_WIDTH), F32),
        compiler_params=pltpu.CompilerParams(
            dimension_semantics=("parallel",), vmem_limit_bytes=VMEM_LIMIT),
        name="ffn_half_step",
    )(*args)


M1_TM = 512
M1_PIECES = 2
M1_CHAINS = 2
PACKED_PARTS = 5
N_MASKS = 7


def _seg_scan_rows(x, rowid, width, reverse):
    d = 1
    y = x
    while d < width:
        if reverse:
            shifted = pltpu.roll(y, CHUNKS - d, axis=0)
            ok = (rowid & (width - 1)) < width - d
        else:
            shifted = pltpu.roll(y, d, axis=0)
            ok = (rowid & (width - 1)) >= d
        y = y * jnp.where(ok, shifted, 1.0)
        d *= 2
    return y


def _write_level_masks(mask_ref):
    pr = lax.broadcasted_iota(jnp.int32, (BLOCK, BLOCK), 0)
    pc = lax.broadcasted_iota(jnp.int32, (BLOCK, BLOCK), 1)
    same_chunk = ((pr ^ pc) & (CHUNKS - 1)) == 0
    mask_ref[0] = jnp.where(pr == pc, 1.0, 0.0)
    for beta in range(4):
        sh = 3 + beta + 1
        mask_ref[1 + beta] = jnp.where(same_chunk & ((pr >> sh) == (pc >> sh)), 1.0, 0.0)
    for gamma in range(2):
        sh = gamma + 1
        mask_ref[5 + gamma] = jnp.where(
            ((pr & (CHUNKS - 1)) >> sh) == ((pc & (CHUNKS - 1)) >> sh), 1.0, 0.0)


N_LEVELS = 7
HGRN_STAGED_BLOCKS = 1
LEVEL_TARGETS = (
    [[j for j in range(POSITIONS) if j & (1 << beta)] for beta in range(4)]
    + [list(range(POSITIONS))] * 3)
LEVEL_MASK = (1, 2, 3, 4, 5, 6, None)


def _hgrn_staged(blocks, st_ref, mask_ref, qs_scr, ks_scr, p_scr, q2_scr, tot_scr, o_ref):
    n_units = len(blocks) * N_HEADS
    rowid = lax.broadcasted_iota(jnp.int32, (CHUNKS, HEAD_DIM), 0)
    zero = jnp.zeros((CHUNKS, HEAD_DIM), F32)

    def slabs(val):
        return [val[j * CHUNKS:(j + 1) * CHUNKS, :] for j in range(POSITIONS)]

    def cat(sl):
        return jnp.concatenate(sl, axis=0)

    def mask_rows(level, j):
        return mask_ref[level, j * CHUNKS:(j + 1) * CHUNKS, :]

    for u in range(n_units):
        h = u % N_HEADS
        f, q, k, _, _ = blocks[u // N_HEADS]
        ls = slice(h * HEAD_DIM, (h + 1) * HEAD_DIM)
        fs, qs, ks = slabs(f[:, ls]), slabs(q[:, ls]), slabs(k[:, ls])
        for j in range(POSITIONS):
            p_scr[u, j * CHUNKS:(j + 1) * CHUNKS, :] = (
                jnp.sum(qs[j] * ks[j], axis=-1, keepdims=True) * mask_rows(0, j))

        for beta in range(4):
            w = 1 << beta
            a = [None] * POSITIONS
            for j in range(POSITIONS):
                if j & w:
                    a[j] = fs[j] if (j & (w - 1)) == 0 else a[j - 1] * fs[j]
            b = [None] * POSITIONS
            for j in range(POSITIONS - 1, -1, -1):
                if not (j & w) and (j & (w - 1)) != w - 1:
                    b[j] = fs[j + 1] if b[j + 1] is None else b[j + 1] * fs[j + 1]
            upper = LEVEL_TARGETS[beta]
            qs_scr[u, beta, 0:len(upper) * CHUNKS, :] = cat(
                [qs[j] * a[j] for j in upper]).astype(BF16)
            ks_scr[u, beta] = cat(
                [zero if (j & w) else (ks[j] if b[j] is None else ks[j] * b[j])
                 for j in range(POSITIONS)])

        gcum = [fs[0]]
        for j in range(1, POSITIONS):
            gcum.append(gcum[-1] * fs[j])
        hsuf = [None] * POSITIONS
        for j in range(POSITIONS - 2, -1, -1):
            hsuf[j] = fs[j + 1] if hsuf[j + 1] is None else hsuf[j + 1] * fs[j + 1]
        g = gcum[-1]
        qg = [qs[j] * gcum[j] for j in range(POSITIONS)]
        kh = [ks[j] if hsuf[j] is None else ks[j] * hsuf[j] for j in range(POSITIONS)]
        g_prev = pltpu.roll(g, 1, axis=0)
        g_next = pltpu.roll(g, CHUNKS - 1, axis=0)

        for gamma in range(3):
            w = 1 << gamma
            low = rowid & (w - 1)
            ca = _seg_scan_rows(jnp.where(low == 0, 1.0, g_prev), rowid, w, False)
            ca = jnp.where((rowid & w) != 0, ca, 0.0)
            cb = _seg_scan_rows(jnp.where(low == w - 1, 1.0, g_next), rowid, w, True)
            cb = jnp.where((rowid & w) == 0, cb, 0.0)
            qs_scr[u, 4 + gamma] = cat([x * ca for x in qg]).astype(BF16)
            ks_scr[u, 4 + gamma] = cat([x * cb for x in kh])

        cp = _seg_scan_rows(jnp.where(rowid == 0, 1.0, g_prev), rowid, CHUNKS, False)
        cs = _seg_scan_rows(jnp.where(rowid == CHUNKS - 1, 1.0, g_next), rowid, CHUNKS, True)
        q2_scr[u] = cat([x * cp for x in qg]).astype(BF16)
        ks_scr[u, N_LEVELS] = cat([x * cs for x in kh])
        tot_scr[u] = jnp.broadcast_to((cp * g)[CHUNKS - 1:CHUNKS, :], (CHUNKS, HEAD_DIM))

    for u in range(n_units):
        for lvl in range(N_LEVELS):
            targets = LEVEL_TARGETS[lvl]
            kt = jnp.transpose(ks_scr[u, lvl]).astype(BF16)
            s = jnp.dot(qs_scr[u, lvl, 0:len(targets) * CHUNKS, :], kt,
                        preferred_element_type=F32)
            for i, j in enumerate(targets):
                rs = slice(j * CHUNKS, (j + 1) * CHUNKS)
                sj = s[i * CHUNKS:(i + 1) * CHUNKS, :]
                if LEVEL_MASK[lvl] is not None:
                    sj = sj * mask_rows(LEVEL_MASK[lvl], j)
                p_scr[u, rs, :] = p_scr[u, rs, :] + sj

    for u in range(n_units):
        h = u % N_HEADS
        _, _, _, vb, orows = blocks[u // N_HEADS]
        ls = slice(h * HEAD_DIM, (h + 1) * HEAD_DIM)
        st = st_ref[h]
        lhs = jnp.concatenate([p_scr[u].astype(BF16), q2_scr[u]], axis=1)
        rhs = jnp.concatenate([vb[:, ls], st.astype(BF16)], axis=0)
        o_ref[orows, ls] = jnp.dot(lhs, rhs, preferred_element_type=F32)
        k2t = jnp.transpose(ks_scr[u, N_LEVELS]).astype(BF16)
        upd = jnp.dot(k2t, vb[:, ls], preferred_element_type=F32)
        decay_col = jnp.transpose(jnp.broadcast_to(tot_scr[u, 0:1, :], (HEAD_DIM, HEAD_DIM)))
        st_ref[h] = st * decay_col + upd


def _mix_in_kernel(x_ref, mod_ref, nw_ref, win_ref, ws_ref, bs_ref, lb_ref,
                   pk_ref, o_ref,
                   st_ref, mask_ref, qs_scr, ks_scr, p_scr, q2_scr, tot_scr):
    zu_ref, mx_ref, sog_ref, sga_ref, sgb_ref = [
        pk_ref.at[:, i * D_MODEL:(i + 1) * D_MODEL] for i in range(PACKED_PARTS)]
    tm = pk_ref.shape[0]
    nblk = tm // BLOCK // M1_CHAINS

    @pl.when(pl.program_id(1) == 0)
    def _():
        st_ref[...] = jnp.zeros_like(st_ref)

    _write_level_masks(mask_ref)
    shift = mod_ref[3:4, :]
    scale = mod_ref[4:5, :]
    row = _block_token(lax.broadcasted_iota(jnp.int32, (BLOCK, BLOCK), 0))
    col = _block_token(lax.broadcasted_iota(jnp.int32, (BLOCK, BLOCK), 1))
    causal = row >= col
    lb = lb_ref[...]
    half_span = 0.5 * (1.0 - lb)
    f_mid = lb + half_span

    def project_stage(chain, res):
        b0 = chain * nblk
        rows = slice(b0 * BLOCK, (b0 + nblk) * BLOCK)
        x = _gather_block_rows(x_ref, b0, nblk)
        hb = _rms_modulate(x, nw_ref[...], scale, shift).astype(BF16)

        def project(p0):
            z = jnp.dot(hb, win_ref[:, p0 * D_MODEL:(p0 + M1_PIECES) * D_MODEL],
                        preferred_element_type=F32)
            return [z[:, i * D_MODEL:(i + 1) * D_MODEL] for i in range(M1_PIECES)]

        zu, zv = project(0)
        zu_ref[rows, :] = zu.astype(BF16)
        res["gv"] = _gelu_tanh(zv)
        yield
        zq, zf = project(2)
        hq = 0.5 * zq
        res["q"] = hq + hq * jnp.tanh(hq)
        at = half_span * jnp.tanh(0.5 * zf)
        res["f"] = f_mid + at
        res["k"] = half_span - at
        yield
        zi, zog = project(4)
        res["vb"] = zi.astype(BF16)
        sog_ref[rows, :] = jnp.tanh(0.5 * zog).astype(BF16)
        yield
        zga, zgb = project(6)
        sga_ref[rows, :] = jnp.tanh(0.5 * zga).astype(BF16)
        sgb_ref[rows, :] = jnp.tanh(0.5 * zgb).astype(BF16)
        yield

    def gmlp_stage(chain, res):
        b0 = chain * nblk
        gv = res["gv"]
        for g in range(GMLP_GROUPS):
            ls = slice(g * HEAD_DIM, (g + 1) * HEAD_DIM)
            xg = gv[:, ls]
            mu = jnp.mean(xg, axis=-1, keepdims=True)
            xc = xg - mu
            var = jnp.mean(xc * xc, axis=-1, keepdims=True)
            vn = (xc * lax.rsqrt(var + EPS)).astype(BF16)
            rhs = jnp.concatenate(
                [vn[n * BLOCK:(n + 1) * BLOCK, :] for n in range(nblk)], axis=1)
            wc = jnp.where(causal, ws_ref[g], 0.0).astype(BF16)
            mixed = jnp.dot(wc, rhs, preferred_element_type=F32)
            bias = jnp.broadcast_to(bs_ref[:, g:g + 1], (BLOCK, BLOCK))
            for n in range(nblk):
                rs = slice((b0 + n) * BLOCK, (b0 + n + 1) * BLOCK)
                mx = mixed[:, n * BLOCK:(n + 1) * BLOCK] + bias
                mx_ref[rs, ls] = mx.astype(BF16)

    def hgrn_stage(chain, res):
        b0 = chain * nblk
        for n in range(0, nblk, HGRN_STAGED_BLOCKS):
            blocks = []
            for m in range(n, n + HGRN_STAGED_BLOCKS):
                br = slice(m * BLOCK, (m + 1) * BLOCK)
                orows = slice((b0 + m) * BLOCK, (b0 + m + 1) * BLOCK)
                blocks.append((res["f"][br, :], res["q"][br, :], res["k"][br, :],
                               res["vb"][br, :], orows))
            _hgrn_staged(blocks, st_ref, mask_ref, qs_scr, ks_scr, p_scr, q2_scr, tot_scr, o_ref)

    for chain in range(M1_CHAINS):
        res = {}
        for _ in project_stage(chain, res):
            pass
        gmlp_stage(chain, res)
        hgrn_stage(chain, res)


def _mix_in(h16, mod, nw, w_in, ws_p, bs_p, lb, *, layer, batch, seq):
    n = h16.shape[0] * POSITIONS
    tm = M1_TM
    tiles_per_seq = seq // tm
    tile = pl.BlockSpec((tm, D_MODEL), lambda b, t: (b * tiles_per_seq + t, 0))
    const2 = lambda b, t: (0, 0)
    f32_out = jax.ShapeDtypeStruct((n, D_MODEL), F32)
    bf_out = jax.ShapeDtypeStruct((n, PACKED_PARTS * D_MODEL), BF16)
    units = HGRN_STAGED_BLOCKS * N_HEADS
    return pl.pallas_call(
        _mix_in_kernel,
        grid=(batch, tiles_per_seq),
        in_specs=[
            pl.BlockSpec((tm // POSITIONS, CHUNK_WIDTH), lambda b, t: (b * tiles_per_seq + t, 0)),
            pl.BlockSpec((None, N_ADA, D_MODEL), lambda b, t: (b, 0, 0)),
            pl.BlockSpec((1, D_MODEL), const2),
            pl.BlockSpec((None, D_MODEL, N_PIECES * D_MODEL), lambda b, t: (layer, 0, 0),
                         pipeline_mode=pl.Buffered(1)),
            pl.BlockSpec((GMLP_GROUPS, BLOCK, BLOCK), lambda b, t: (0, 0, 0)),
            pl.BlockSpec((BLOCK, GMLP_GROUPS), const2),
            pl.BlockSpec((1, D_MODEL), const2),
        ],
        out_specs=[pl.BlockSpec((tm, PACKED_PARTS * D_MODEL),
                                lambda b, t: (b * tiles_per_seq + t, 0)), tile],
        out_shape=[bf_out, f32_out],
        scratch_shapes=[
            pltpu.VMEM((N_HEADS, HEAD_DIM, HEAD_DIM), F32),
            pltpu.VMEM((N_MASKS, BLOCK, BLOCK), F32),
            pltpu.VMEM((units, N_LEVELS, BLOCK, HEAD_DIM), BF16),
            pltpu.VMEM((units, N_LEVELS + 1, BLOCK, HEAD_DIM), F32),
            pltpu.VMEM((units, BLOCK, BLOCK), F32),
            pltpu.VMEM((units, BLOCK, HEAD_DIM), BF16),
            pltpu.VMEM((units, CHUNKS, HEAD_DIM), F32),
        ],
        compiler_params=pltpu.CompilerParams(
            dimension_semantics=("parallel", "arbitrary"), vmem_limit_bytes=VMEM_LIMIT),
        name="mixer_in_proj_hgrn2",
    )(h16, mod, nw, w_in, ws_p, bs_p, lb)


M3_TM = 1024
M3_CHAINS = 2


def _mix_out_kernel(x_ref, mod_ref, pk_ref, o_ref, wa_ref, wb_ref, wo_ref, out_ref):
    zu_ref, mx_ref, sog_ref, sga_ref, sgb_ref = [
        pk_ref.at[:, i * D_MODEL:(i + 1) * D_MODEL] for i in range(PACKED_PARTS)]
    nblk = pk_ref.shape[0] // BLOCK // M3_CHAINS
    gate = mod_ref[5:6, :]
    for chain in range(M3_CHAINS):
        b0 = chain * nblk
        rows = slice(b0 * BLOCK, (b0 + nblk) * BLOCK)
        o = o_ref[rows, :]
        parts = []
        for h in range(N_HEADS):
            oh = o[:, h * HEAD_DIM:(h + 1) * HEAD_DIM]
            ms = jnp.mean(oh * oh, axis=-1, keepdims=True)
            parts.append(oh * (0.5 * lax.rsqrt(ms + EPS)))
        hn = jnp.concatenate(parts, axis=1)
        on = (hn + hn * sog_ref[rows, :].astype(F32)).astype(BF16)
        yap = (_gelu_tanh(zu_ref[rows, :].astype(F32))
               * mx_ref[rows, :].astype(F32)).astype(BF16)
        hya = 0.5 * jnp.dot(yap, wa_ref[...], preferred_element_type=F32)
        hyb = 0.5 * jnp.dot(on, wb_ref[...], preferred_element_type=F32)
        m = ((hya + hya * sga_ref[rows, :].astype(F32))
             + (hyb + hyb * sgb_ref[rows, :].astype(F32))).astype(BF16)
        out = jnp.dot(m, wo_ref[...], preferred_element_type=F32)
        y = _gather_block_rows(x_ref, b0, nblk) + (1.0 + gate) * out
        _scatter_block_rows(out_ref, y, b0, nblk)


def _mix_out(h16, mod, pk, o, w_a, w_b, w_o, *, layer, seq):
    n = h16.shape[0] * POSITIONS
    tm = M3_TM
    tiles_per_seq = seq // tm
    tile = pl.BlockSpec((tm, D_MODEL), lambda i: (i, 0))
    tile16 = pl.BlockSpec((tm // POSITIONS, CHUNK_WIDTH), lambda i: (i, 0))
    wspec = pl.BlockSpec((None, D_MODEL, D_MODEL), lambda i: (layer, 0, 0),
                         pipeline_mode=pl.Buffered(1))
    return pl.pallas_call(
        _mix_out_kernel,
        grid=(n // tm,),
        in_specs=[
            tile16,
            pl.BlockSpec((None, N_ADA, D_MODEL), lambda i: (i // tiles_per_seq, 0, 0)),
            pl.BlockSpec((tm, PACKED_PARTS * D_MODEL), lambda i: (i, 0)),
            tile,
            wspec, wspec, wspec,
        ],
        out_specs=tile16,
        out_shape=jax.ShapeDtypeStruct(h16.shape, F32),
        compiler_params=pltpu.CompilerParams(
            dimension_semantics=("parallel",), vmem_limit_bytes=VMEM_LIMIT),
        name="mixer_out_proj",
    )(h16, mod, pk, o, w_a, w_b, w_o)


def kernel(x, c, w_in, gmlp_ws, gmlp_b, hgrn_lb_logits, w_branch_a, w_branch_b, w_out,
           ffn_w13, ffn_w2, norm_w, ada_w, ada_b, final_norm_w):
    batch, seq, d = x.shape
    assert d == D_MODEL and seq % M1_TM == 0 and seq % FFN_TM == 0 and seq % M3_TM == 0
    n = batch * seq

    mod_all = _ada_mod(c, ada_w, ada_b).reshape(DEPTH, batch, N_ADA, D_MODEL)
    lb_all = _lower_bounds(hgrn_lb_logits)
    perm = jnp.array([_block_token(p) for p in range(BLOCK)], dtype=jnp.int32)
    ws_p = gmlp_ws[:, :, perm, :][:, :, :, perm]
    bs_p = jnp.swapaxes(gmlp_b[:, :, perm], 1, 2)

    w13_b, w2_b, w_in_b = ffn_w13.astype(BF16), ffn_w2.astype(BF16), w_in.astype(BF16)
    w_a_b, w_b_b, w_o_b = (w_branch_a.astype(BF16), w_branch_b.astype(BF16),
                           w_out.astype(BF16))

    h16 = x.reshape(n, D_MODEL)
    for l in range(DEPTH):
        mod = mod_all[l]
        nw = norm_w[l].reshape(3, 1, D_MODEL)
        h16 = _ffn(h16, mod, nw[0], w13_b, w2_b, layer=l, half=0, seq=seq,
                   in_natural=(l == 0))
        pk, o = _mix_in(h16, mod, nw[1], w_in_b, ws_p[l], bs_p[l],
                        lb_all[l].reshape(1, D_MODEL), layer=l, batch=batch, seq=seq)
        h16 = _mix_out(h16, mod, pk, o, w_a_b, w_b_b, w_o_b, layer=l, seq=seq)
        final_w = final_norm_w.reshape(1, D_MODEL) if l == DEPTH - 1 else None
        h16 = _ffn(h16, mod, nw[2], w13_b, w2_b, layer=l, half=1, seq=seq, final_w=final_w)
    return h16.reshape(batch, seq, D_MODEL)
```
